```python
import math
import jax, jax.numpy as jnp
from jax import lax
import numpy as np

D_MODEL = 1024
BATCH = 8
SEQ = 4096
DEPTH = 4

CHUNK = 64
N_META = 16
Q_BLOCK = 128
HEAD_DIM = 64
SB_HEADS = 8
MLA_HEADS = 8
MLA_NOPE = 64
MLA_ROPE = 32
MLA_V = 64
MLA_Q_RANK = 256
MLA_KV_RANK = 256
ROPE_BASE = 10000.0
SWA_HEADS = 8
SWA_KV_HEADS = 2
WINDOW = 128
WINDOW_CHUNKS = WINDOW // CHUNK
BAND_BACK = WINDOW + CHUNK
FOX_HEADS = 8
FORGET_BIAS_INIT = 3.0

A_WIDTH = SB_HEADS * HEAD_DIM
B_WIDTH = MLA_HEADS * MLA_V
C_WIDTH = SWA_HEADS * HEAD_DIM
C_KV_WIDTH = SWA_KV_HEADS * HEAD_DIM
D_WIDTH = FOX_HEADS * HEAD_DIM
EVEN_WIDTH = A_WIDTH + B_WIDTH
ODD_WIDTH = C_WIDTH + D_WIDTH
EVEN_IN = 3 * A_WIDTH + MLA_Q_RANK + MLA_KV_RANK + MLA_ROPE + EVEN_WIDTH
ODD_IN = C_WIDTH + 2 * C_KV_WIDTH + 3 * D_WIDTH + FOX_HEADS + ODD_WIDTH
N_EVEN = (DEPTH + 1) // 2
N_ODD = DEPTH // 2

DN_ALPHA = (2 * DEPTH) ** 0.25
DN_BETA = (8 * DEPTH) ** -0.25
LN_EPS = 1e-5
RMS_EPS = 1e-6
NEG = -1e30

kernel_name = 'hybrid_chunk_causal_sb_mla_swa_fox'


def _split(t, sizes):
    out, start = [], 0
    for w in sizes:
        out.append(t[..., start:start + w])
        start += w
    return out


def _layer_norm(x, g, b):
    xf = x.astype(jnp.float32)
    mu = jnp.mean(xf, -1, keepdims=True)
    var = jnp.mean(jnp.square(xf - mu), -1, keepdims=True)
    return ((xf - mu) * lax.rsqrt(var + LN_EPS) * g.astype(jnp.float32) + b.astype(jnp.float32)).astype(x.dtype)


def _rms_norm(x, g):
    xf = x.astype(jnp.float32)
    return (xf * lax.rsqrt(jnp.mean(xf * xf, -1, keepdims=True) + RMS_EPS) * g.astype(jnp.float32)).astype(x.dtype)


def _rope(x, pos):
    half = x.shape[-1] // 2
    inv = ROPE_BASE ** (-jnp.arange(half, dtype=jnp.float32) / half)
    ang = pos.astype(jnp.float32)[:, None] * inv[None, :]
    cos = jnp.cos(ang)[None, :, None, :]
    sin = jnp.sin(ang)[None, :, None, :]
    xf = x.astype(jnp.float32)
    x1, x2 = xf[..., :half], xf[..., half:]
    return jnp.concatenate([x1 * cos - x2 * sin, x2 * cos + x1 * sin], -1).astype(x.dtype)


def _chunk_index(pos):
    return jnp.where(pos < N_META, 0, 1 + (pos - N_META) // CHUNK)


def _alibi_slopes(n):
    return jnp.asarray([2.0 ** (-8.0 * (h + 1) / n) for h in range(n)], dtype=jnp.float32)


def _stick_breaking(q, k, v):
    n = q.shape[1]
    scale = HEAD_DIM ** -0.5
    outs = []
    for q0 in range(0, n, Q_BLOCK):
        q1 = q0 + Q_BLOCK
        z = jnp.einsum('bqhd,bkhd->bhqk', q[:, q0:q1], k[:, :q1]).astype(jnp.float32) * scale
        past = jnp.arange(q1)[None, :] < jnp.arange(q0, q1)[:, None]
        log_stay = jnp.where(past, jax.nn.log_sigmoid(-z), 0.0)
        log_after = lax.cumsum(log_stay, axis=3, reverse=True) - log_stay
        w = jnp.where(past, jnp.exp(jax.nn.log_sigmoid(z) + log_after), 0.0)
        outs.append(jnp.einsum('bhqk,bkhd->bqhd', w.astype(v.dtype), v[:, :q1]))
    return jnp.concatenate(outs, axis=1)


def _mla(q_nope, q_rope, k_nope, k_rope, v, chunk):
    n = q_nope.shape[1]
    scale = (MLA_NOPE + MLA_ROPE) ** -0.5
    outs = []
    for q0 in range(0, n, Q_BLOCK):
        q1 = q0 + Q_BLOCK
        k1 = min(n, q1 + CHUNK)
        s = (jnp.einsum('bqhd,bkhd->bhqk', q_nope[:, q0:q1], k_nope[:, :k1])
             + jnp.einsum('bqhr,bkr->bhqk', q_rope[:, q0:q1], k_rope[:, :k1])).astype(jnp.float32) * scale
        vis = chunk[None, :k1] <= chunk[q0:q1, None]
        p = jax.nn.softmax(jnp.where(vis, s, NEG), axis=-1)
        outs.append(jnp.einsum('bhqk,bkhd->bqhd', p.astype(v.dtype), v[:, :k1]))
    return jnp.concatenate(outs, axis=1)


def _swa_sinks(q, k, v, sinks, chunk):
    b, n, hq, d = q.shape
    group = SWA_HEADS // SWA_KV_HEADS
    qg = q.reshape(b, n, SWA_KV_HEADS, group, d)
    slopes = _alibi_slopes(SWA_HEADS).reshape(SWA_KV_HEADS, group)
    sink = sinks.astype(jnp.float32).reshape(SWA_KV_HEADS, group)
    scale = HEAD_DIM ** -0.5
    outs = []
    for q0 in range(0, n, Q_BLOCK):
        q1 = q0 + Q_BLOCK
        k0 = max(N_META, q0 - BAND_BACK)
        k1 = min(n, q1 + CHUNK)
        kidx = jnp.concatenate([jnp.arange(N_META), jnp.arange(k0, k1)])
        kk = jnp.concatenate([k[:, :N_META], k[:, k0:k1]], axis=1)
        vv = jnp.concatenate([v[:, :N_META], v[:, k0:k1]], axis=1)
        tq = jnp.arange(q0, q1)
        dist = jnp.abs(tq[:, None] - kidx[None, :]).astype(jnp.float32)
        s = (jnp.einsum('bqgnd,bkgd->bgnqk', qg[:, q0:q1], kk).astype(jnp.float32) * scale
             - slopes[:, :, None, None] * dist)
        ct = chunk[q0:q1][:, None]
        cs = chunk[kidx][None, :]
        vis = (cs <= ct) & ((ct - cs <= WINDOW_CHUNKS) | (kidx[None, :] < N_META))
        s = jnp.where(vis, s, NEG)
        sink_col = jnp.broadcast_to(sink[None, :, :, None, None], s.shape[:-1] + (1,))
        p = jax.nn.softmax(jnp.concatenate([s, sink_col], axis=-1), axis=-1)[..., :-1]
        o = jnp.einsum('bgnqk,bkgd->bqgnd', p.astype(v.dtype), vv)
        outs.append(o.reshape(b, q1 - q0, hq, d))
    return jnp.concatenate(outs, axis=1)


def _forgetting(q, k, v, log_f):
    n = q.shape[1]
    scale = HEAD_DIM ** -0.5
    cum = jnp.cumsum(log_f, axis=1).transpose(0, 2, 1)
    outs = []
    for q0 in range(0, n, Q_BLOCK):
        q1 = q0 + Q_BLOCK
        s = (jnp.einsum('bqhd,bkhd->bhqk', q[:, q0:q1], k[:, :q1]).astype(jnp.float32) * scale
             + cum[:, :, q0:q1, None] - cum[:, :, None, :q1])
        vis = jnp.arange(q1)[None, :] <= jnp.arange(q0, q1)[:, None]
        p = jax.nn.softmax(jnp.where(vis, s, NEG), axis=-1)
        outs.append(jnp.einsum('bhqk,bkhd->bqhd', p.astype(v.dtype), v[:, :q1]))
    return jnp.concatenate(outs, axis=1)


def _even_mixer(h, w_in, g_cq, g_ckv, w_uq, w_ukv, w_out, chunk):
    b, n, _ = h.shape
    proj = h @ w_in
    qa, ka, va, cq, ckv, kr, gate = _split(
        proj, [A_WIDTH, A_WIDTH, A_WIDTH, MLA_Q_RANK, MLA_KV_RANK, MLA_ROPE, EVEN_WIDTH])
    heads = lambda t, nh: t.reshape(b, n, nh, -1)
    o_a = _stick_breaking(heads(qa, SB_HEADS), heads(ka, SB_HEADS), heads(va, SB_HEADS))
    pos = jnp.arange(n)
    q = (_rms_norm(cq, g_cq) @ w_uq).reshape(b, n, MLA_HEADS, MLA_NOPE + MLA_ROPE)
    kv = (_rms_norm(ckv, g_ckv) @ w_ukv).reshape(b, n, MLA_HEADS, MLA_NOPE + MLA_V)
    q_rope = _rope(q[..., MLA_NOPE:], pos)
    k_rope = _rope(kr[:, :, None, :], pos)[:, :, 0]
    o_b = _mla(q[..., :MLA_NOPE], q_rope, kv[..., :MLA_NOPE], k_rope, kv[..., MLA_NOPE:], chunk)
    mixed = jnp.concatenate([o_a.reshape(b, n, A_WIDTH), o_b.reshape(b, n, B_WIDTH)], axis=-1)
    return (mixed * jax.nn.silu(gate)) @ w_out


def _odd_mixer(h, w_in, b_forget, sinks, w_out, chunk):
    b, n, _ = h.shape
    proj = h @ w_in
    qc, kc, vc, qd, kd, vd, fz, gate = _split(
        proj, [C_WIDTH, C_KV_WIDTH, C_KV_WIDTH, D_WIDTH, D_WIDTH, D_WIDTH, FOX_HEADS, ODD_WIDTH])
    heads = lambda t, nh: t.reshape(b, n, nh, -1)
    o_c = _swa_sinks(heads(qc, SWA_HEADS), heads(kc, SWA_KV_HEADS), heads(vc, SWA_KV_HEADS), sinks, chunk)
    log_f = jax.nn.log_sigmoid(fz.astype(jnp.float32) + b_forget.astype(jnp.float32))
    o_d = _forgetting(heads(qd, FOX_HEADS), heads(kd, FOX_HEADS), heads(vd, FOX_HEADS), log_f)
    mixed = jnp.concatenate([o_c.reshape(b, n, C_WIDTH), o_d.reshape(b, n, D_WIDTH)], axis=-1)
    return (mixed * jax.nn.silu(gate)) @ w_out


def setup_inputs(seed: int = 0) -> dict:
    key = jax.random.key(seed)
    ks = jax.random.split(key, 15)
    nrm = lambda k, shape, scale: jax.random.normal(k, shape, jnp.float32) * scale
    return {
        'x': nrm(ks[0], (BATCH, SEQ, D_MODEL), 1.0),
        'meta_tokens': nrm(ks[1], (N_META, D_MODEL), 1.0),
        'w_in_even': nrm(ks[2], (N_EVEN, D_MODEL, EVEN_IN), D_MODEL ** -0.5),
        'g_cq': 1.0 + nrm(ks[3], (N_EVEN, MLA_Q_RANK), 0.02),
        'g_ckv': 1.0 + nrm(ks[4], (N_EVEN, MLA_KV_RANK), 0.02),
        'w_uq': nrm(ks[5], (N_EVEN, MLA_Q_RANK, MLA_HEADS * (MLA_NOPE + MLA_ROPE)), MLA_Q_RANK ** -0.5),
        'w_ukv': nrm(ks[6], (N_EVEN, MLA_KV_RANK, MLA_HEADS * (MLA_NOPE + MLA_V)), MLA_KV_RANK ** -0.5),
        'w_out_even': nrm(ks[7], (N_EVEN, EVEN_WIDTH, D_MODEL), DN_BETA * EVEN_WIDTH ** -0.5),
        'w_in_odd': nrm(ks[8], (N_ODD, D_MODEL, ODD_IN), D_MODEL ** -0.5),
        'b_forget': FORGET_BIAS_INIT + nrm(ks[9], (N_ODD, FOX_HEADS), 0.1),
        'sink_logits': nrm(ks[10], (N_ODD, SWA_HEADS), 0.5),
        'w_out_odd': nrm(ks[11], (N_ODD, ODD_WIDTH, D_MODEL), DN_BETA * ODD_WIDTH ** -0.5),
        'ln_gain': 1.0 + nrm(ks[12], (DEPTH, D_MODEL), 0.02),
        'ln_bias': nrm(ks[13], (DEPTH, D_MODEL), 0.02),
    }


def reference(x, meta_tokens, w_in_even, g_cq, g_ckv, w_uq, w_ukv, w_out_even,
              w_in_odd, b_forget, sink_logits, w_out_odd, ln_gain, ln_bias):
    b, s, d = x.shape
    n = s + N_META
    n_pad = -(-n // Q_BLOCK) * Q_BLOCK
    h = jnp.concatenate([jnp.broadcast_to(meta_tokens[None].astype(x.dtype), (b, N_META, d)), x,
                         jnp.zeros((b, n_pad - n, d), x.dtype)], axis=1)
    chunk = _chunk_index(jnp.arange(n_pad))
    for layer in range(DEPTH):
        i = layer // 2
        if layer % 2 == 0:
            y = _even_mixer(h, w_in_even[i], g_cq[i], g_ckv[i], w_uq[i], w_ukv[i], w_out_even[i], chunk)
        else:
            y = _odd_mixer(h, w_in_odd[i], b_forget[i], sink_logits[i], w_out_odd[i], chunk)
        h = _layer_norm(DN_ALPHA * h + y, ln_gain[layer], ln_bias[layer])
    return h[:, N_META:N_META + s]
```

```python
import functools
import math

import jax
import jax.numpy as jnp
import numpy as np
from jax import lax
from jax.experimental import pallas as pl
from jax.experimental.pallas import tpu as pltpu

D_MODEL = 1024
CHUNK = 64
N_META = 16
HEAD_DIM = 64
N_HEADS = 8
MLA_NOPE = 64
MLA_ROPE = 32
MLA_RANK = 256
SWA_KV_HEADS = 2
WINDOW_CHUNKS = 2
ROPE_BASE = 10000.0
DEPTH = 4
DN_ALPHA = (2 * DEPTH) ** 0.25
LN_EPS = 1e-5
RMS_EPS = 1e-6
NEG = -1e30
NO_KEY = 1 << 30

LANES = 128
BLK = 128
LEAD = BLK - N_META
HALF = LANES // 2
VMEM_LIMIT = 56 * 1024 * 1024

F32 = jnp.float32
BF16 = jnp.bfloat16


def _nt_dot(a, b):
    return lax.dot_general(a, b, (((1,), (1,)), ((), ())), preferred_element_type=F32)


def _dot(a, b):
    return jnp.dot(a, b, preferred_element_type=F32)


def _chunk_of(pos):
    return jnp.maximum((pos >> 6) - 1, 0)


def _log_sigmoid(x):
    return jnp.minimum(x, 0.0) - jnp.log(1.0 + jnp.exp(-jnp.abs(x)))


def _sb_kernel(q_ref, k_ref, v_ref, o_ref, u_ref, *, nb):
    r2 = lax.broadcasted_iota(jnp.int32, (2 * BLK, 2 * BLK), 0)
    c2 = lax.broadcasted_iota(jnp.int32, (2 * BLK, 2 * BLK), 1)
    u_ref[...] = jnp.where((c2 >= BLK) | ((r2 & (BLK - 1)) > c2), 1.0, 0.0).astype(BF16)

    lane = lax.broadcasted_iota(jnp.int32, (BLK, BLK), 1)
    rowi = lax.broadcasted_iota(jnp.int32, (BLK, BLK), 0)

    def q_block(m, _):
        r0 = pl.multiple_of(m * BLK, BLK)
        q2 = q_ref[0, pl.ds(r0, BLK), :]
        qpos = m * BLK + rowi
        accs = []
        for h in range(2):
            qh = jnp.where((lane >= HALF * h) & (lane < HALF * (h + 1)), q2.astype(F32), 0.0).astype(BF16)

            def block(jb, carry, acc, masked, lo=LEAD, qh=qh):
                k0 = pl.multiple_of(jb * BLK, BLK)
                kb = k_ref[0, pl.ds(k0, BLK), :]
                vb = v_ref[0, pl.ds(k0, BLK), :]
                z = _nt_dot(qh, kb)
                nz = -z
                soft = jnp.log(1.0 + jnp.exp(jnp.minimum(z, nz)))
                ls_neg = jnp.minimum(nz, 0.0) - soft
                ls_pos = z + ls_neg
                if masked:
                    kpos = jb * BLK + lane
                    msk = (kpos >= lo) & (kpos < qpos)
                    ls_neg = jnp.where(msk, ls_neg, 0.0)
                hi = ls_neg.astype(BF16)
                rest = (ls_neg - hi.astype(F32)).astype(BF16)
                cs = _dot(jnp.concatenate([hi, rest], axis=1), u_ref[...])
                log_after = cs[:, :BLK] + carry
                carry = carry + cs[:, BLK:]
                w = jnp.exp(ls_pos + log_after)
                if masked:
                    w = jnp.where(msk, w, 0.0)
                acc = acc + _dot(w.astype(BF16), vb)
                return carry, acc

            zero = jnp.zeros((BLK, BLK), F32)
            carry, acc = block(m, zero, zero, True)

            def body(i, c, block=block):
                return block(m - 1 - i, c[0], c[1], False)

            carry, acc = lax.fori_loop(0, jnp.maximum(m - 1, 0), body, (carry, acc))
            carry, acc = block(0, carry, acc, True, lo=jnp.where(m >= 1, LEAD, NO_KEY))
            accs.append(acc)
        o_ref[0, pl.ds(r0, BLK), :] = jnp.where(lane < HALF, accs[0], accs[1])
        return 0

    lax.fori_loop(0, nb, q_block, 0)


def _sb_attention(q, k, v):
    b, n, w = q.shape
    nb = n // BLK
    spec = pl.BlockSpec((1, n, LANES), lambda i, j: (i, 0, j))
    return pl.pallas_call(
        functools.partial(_sb_kernel, nb=nb),
        grid=(b, w // LANES),
        in_specs=[spec, spec, spec],
        out_specs=spec,
        out_shape=jax.ShapeDtypeStruct((b, n, w), F32),
        scratch_shapes=[pltpu.VMEM((2 * BLK, 2 * BLK), BF16)],
        compiler_params=pltpu.CompilerParams(
            dimension_semantics=("parallel", "parallel"), vmem_limit_bytes=VMEM_LIMIT),
        name="sb_attention",
    )(q, k, v)


def _softmax_kernel(q_ref, k_ref, v_ref, o_ref, s_ref, *, nb, mode):
    lane = lax.broadcasted_iota(jnp.int32, (BLK, BLK), 1)
    rowi = lax.broadcasted_iota(jnp.int32, (BLK, BLK), 0)

    def q_block(m, _):
        r0 = pl.multiple_of(m * BLK, BLK)
        qpos = m * BLK + rowi
        outs = []
        for h in range(2):
            cols = slice(h * LANES, (h + 1) * LANES)
            qh = q_ref[0, pl.ds(r0, BLK), cols]

            def scores(jb, qh=qh, cols=cols):
                k0 = pl.multiple_of(jb * BLK, BLK)
                return _nt_dot(qh, k_ref[0, pl.ds(k0, BLK), cols])

            def masked_scores(jb, lo, scores=scores):
                kpos = jb * BLK + lane
                if mode == "chunk":
                    msk = (kpos >= lo) & (_chunk_of(kpos) <= _chunk_of(qpos))
                else:
                    msk = (kpos >= lo) & (kpos <= qpos)
                return jnp.where(msk, scores(jb), NEG)

            s_first = masked_scores(0, LEAD)
            s_diag = masked_scores(m, jnp.where(m >= 1, 0, NO_KEY))
            s_ref[0] = s_first
            s_ref[nb] = s_diag

            def p1(j, mx, scores=scores):
                s = scores(j)
                s_ref[j] = s
                return jnp.maximum(mx, s)

            mx = lax.fori_loop(1, jnp.maximum(m, 1), p1, jnp.maximum(s_first, s_diag))
            m_row = jnp.max(mx, axis=1, keepdims=True)

            def pv(slot, jb, cols=cols):
                k0 = pl.multiple_of(jb * BLK, BLK)
                p = jnp.exp(s_ref[slot] - m_row).astype(BF16)
                return _dot(p, v_ref[0, pl.ds(k0, BLK), cols])

            acc = pv(0, 0) + pv(nb, m)
            acc = lax.fori_loop(1, jnp.maximum(m, 1), lambda j, a, pv=pv: a + pv(j, j), acc)
            outs.append(acc / pltpu.roll(acc, HALF, 1))
        o_ref[0, pl.ds(r0, BLK), :] = jnp.where(lane < HALF, outs[0], outs[1])
        return 0

    lax.fori_loop(0, nb, q_block, 0)


def _softmax_attention(q, k, v, mode):
    b, n, w = q.shape
    nb = n // BLK
    in_spec = pl.BlockSpec((1, n, 2 * LANES), lambda i, j: (i, 0, j))
    return pl.pallas_call(
        functools.partial(_softmax_kernel, nb=nb, mode=mode),
        grid=(b, w // (2 * LANES)),
        in_specs=[in_spec, in_spec, in_spec],
        out_specs=pl.BlockSpec((1, n, LANES), lambda i, j: (i, 0, j)),
        out_shape=jax.ShapeDtypeStruct((b, n, w // 2), F32),
        scratch_shapes=[pltpu.VMEM((nb + 1, BLK, BLK), F32)],
        compiler_params=pltpu.CompilerParams(
            dimension_semantics=("parallel", "parallel"), vmem_limit_bytes=VMEM_LIMIT),
        name="softmax_attention_" + mode,
    )(q, k, v)


def _swa_kernel(sink_ref, q_ref, k_ref, v_ref, o_ref, *, nb):
    lane = lax.broadcasted_iota(jnp.int32, (BLK, BLK), 1)
    rowi = lax.broadcasted_iota(jnp.int32, (BLK, BLK), 0)
    group = N_HEADS // SWA_KV_HEADS

    def q_block(m, _):
        r0 = pl.multiple_of(m * BLK, BLK)
        qpos = m * BLK + rowi
        cq = _chunk_of(qpos)
        slots = ((0, jnp.where(m >= 1, LEAD, NO_KEY)),
                 (jnp.maximum(m - 1, 0), jnp.where(m >= 2, 0, NO_KEY)),
                 (m, LEAD))
        biases = []
        for jb, lo in slots:
            kpos = jb * BLK + lane
            ck = _chunk_of(kpos)
            msk = (kpos >= lo) & (ck <= cq) & ((cq - ck <= WINDOW_CHUNKS) | (kpos < BLK))
            dist = jnp.abs(qpos - kpos).astype(F32)
            biases.append((msk, dist))
        for pair in range(N_HEADS // 2):
            q2 = q_ref[0, pl.ds(r0, BLK), pair * LANES:(pair + 1) * LANES]
            outs = []
            for h in range(2):
                head = 2 * pair + h
                g = head // group
                slope = 2.0 ** (-8.0 * (head + 1) / N_HEADS)
                qh = jnp.where((lane >= HALF * h) & (lane < HALF * (h + 1)), q2.astype(F32), 0.0).astype(BF16)
                ss = []
                for (jb, _), (msk, dist) in zip(slots, biases):
                    k0 = pl.multiple_of(jb * BLK, BLK)
                    s = _nt_dot(qh, k_ref[0, pl.ds(k0, BLK), g * LANES:(g + 1) * LANES]) - slope * dist
                    ss.append(jnp.where(msk, s, NEG))
                sink = sink_ref[head]
                mx = jnp.maximum(jnp.maximum(ss[0], ss[1]), ss[2])
                m_row = jnp.maximum(jnp.max(mx, axis=1, keepdims=True), sink)
                acc = jnp.zeros((BLK, BLK), F32)
                vcol = (2 * g + h) * LANES
                for (jb, _), s in zip(slots, ss):
                    k0 = pl.multiple_of(jb * BLK, BLK)
                    p = jnp.exp(s - m_row).astype(BF16)
                    acc = acc + _dot(p, v_ref[0, pl.ds(k0, BLK), vcol:vcol + LANES])
                denom = pltpu.roll(acc, HALF, 1) + jnp.exp(sink - m_row)
                outs.append(acc / denom)
            o_ref[0, pl.ds(r0, BLK), pair * LANES:(pair + 1) * LANES] = jnp.where(
                lane < HALF, outs[0], outs[1])
        return 0

    lax.fori_loop(0, nb, q_block, 0)


def _swa_attention(sinks, q, k, v):
    b, n, w = q.shape
    nb = n // BLK
    return pl.pallas_call(
        functools.partial(_swa_kernel, nb=nb),
        grid=(b,),
        in_specs=[
            pl.BlockSpec(memory_space=pltpu.SMEM),
            pl.BlockSpec((1, n, w), lambda i: (i, 0, 0)),
            pl.BlockSpec((1, n, k.shape[2]), lambda i: (i, 0, 0)),
            pl.BlockSpec((1, n, v.shape[2]), lambda i: (i, 0, 0)),
        ],
        out_specs=pl.BlockSpec((1, n, w), lambda i: (i, 0, 0)),
        out_shape=jax.ShapeDtypeStruct((b, n, w), F32),
        compiler_params=pltpu.CompilerParams(
            dimension_semantics=("parallel",), vmem_limit_bytes=VMEM_LIMIT),
        name="swa_attention",
    )(sinks, q, k, v)


def _even_proj_kernel(x_ref, wsb_ref, wc_ref, wkr_ref, wg_ref, gq_ref, gkv_ref,
                      wa_ref, wb_ref, wk_ref, wv_ref, pk_ref, cq_ref, sq_ref, ck_ref, sk_ref,
                      qs_ref, ks_ref, vs_ref, qm_ref, km_ref, vm_ref, gate_ref):
    xb = x_ref[0].astype(BF16)
    w = N_HEADS * HEAD_DIM
    sb = _dot(xb, wsb_ref[...])
    qs_ref[0] = sb[:, :w].astype(BF16)
    ks_ref[0] = sb[:, w:2 * w].astype(BF16)
    vs_ref[0] = sb[:, 2 * w:].astype(BF16)
    gate_ref[0] = _dot(xb, wg_ref[...])

    c = _dot(xb, wc_ref[...])

    def rms(t, g):
        return (t * lax.rsqrt(jnp.mean(t * t, axis=-1, keepdims=True) + RMS_EPS) * g).astype(BF16)

    xq = rms(c[:, :MLA_RANK], gq_ref[...])
    xkv = rms(c[:, MLA_RANK:], gkv_ref[...])
    qm = _dot(xq, wa_ref[...]) * cq_ref[...] + _dot(xq, wb_ref[...]) * sq_ref[...]
    qm_ref[0] = qm.astype(BF16)
    kr = _dot(xb, wkr_ref[...])
    krr = (kr[:, :LANES] * ck_ref[...] + kr[:, LANES:] * sk_ref[...]).astype(BF16)
    km_ref[0] = (_dot(xkv, wk_ref[...]) + _dot(krr, pk_ref[...])).astype(BF16)
    col = lax.broadcasted_iota(jnp.int32, (1, N_HEADS * LANES), 1)
    ones_v = jnp.where(((col >> 6) & 1) != ((col >> 7) & 1), 1.0, 0.0)
    vm_ref[0] = (_dot(xkv, wv_ref[...]) + ones_v).astype(BF16)


def _even_proj(h, wts, tabs, tm):
    b, n, d = h.shape
    grid = (n // tm, b)
    row = lambda i, j: (j, i, 0)
    const = lambda i, j: (0, 0)
    tab = lambda i, j: (i, 0)
    full = lambda a: pl.BlockSpec(a.shape, const)
    w_sb, w_c, w_kr, w_g, g_q, g_kv, w_a, w_b, w_k, w_v, p_k = wts
    c_q, s_q, c_k, s_k = tabs
    wide = N_HEADS * LANES
    outs = [(N_HEADS * HEAD_DIM, BF16)] * 3 + [(wide, BF16)] * 3 + [(d, F32)]
    return pl.pallas_call(
        _even_proj_kernel,
        grid=grid,
        in_specs=[pl.BlockSpec((1, tm, d), row)] + [full(a) for a in wts] + [
            pl.BlockSpec((tm, wide), tab), pl.BlockSpec((tm, wide), tab),
            pl.BlockSpec((tm, LANES), tab), pl.BlockSpec((tm, LANES), tab)],
        out_specs=[pl.BlockSpec((1, tm, c), row) for c, _ in outs],
        out_shape=[jax.ShapeDtypeStruct((b, n, c), t) for c, t in outs],
        compiler_params=pltpu.CompilerParams(
            dimension_semantics=("parallel", "parallel"), vmem_limit_bytes=VMEM_LIMIT),
        name="even_proj",
    )(h, *wts, *tabs)


def _odd_proj_kernel(x_ref, wqc_ref, wkc_ref, wvc_ref, wqf_ref, wkf_ref, wvf_ref, wfz_ref, wg_ref,
                     bf_ref, pf_ref,
                     qc_ref, kc_ref, vc_ref, qf_ref, kf_ref, vf_ref, gate_ref, carry_ref, *, tm):
    i = pl.program_id(1)
    xb = x_ref[0].astype(BF16)
    qc_ref[0] = _dot(xb, wqc_ref[...]).astype(BF16)
    kc_ref[0] = _dot(xb, wkc_ref[...]).astype(BF16)
    gate_ref[0] = _dot(xb, wg_ref[...])

    colc = lax.broadcasted_iota(jnp.int32, (1, 2 * SWA_KV_HEADS * LANES), 1)
    colf = lax.broadcasted_iota(jnp.int32, (1, N_HEADS * LANES), 1)
    ones_c = jnp.where(((colc >> 6) & 1) != ((colc >> 7) & 1), 1.0, 0.0)
    ones_f = jnp.where(((colf >> 6) & 1) != ((colf >> 7) & 1), 1.0, 0.0)
    vc_ref[0] = (_dot(xb, wvc_ref[...]) + ones_c).astype(BF16)
    vf_ref[0] = (_dot(xb, wvf_ref[...]) + ones_f).astype(BF16)
    lf_lane = colf & (LANES - 1)
    ones_q = jnp.where((lf_lane >= HALF) & (lf_lane < HALF + 3), 1.0, 0.0)
    qf_ref[0] = (_dot(xb, wqf_ref[...]) + ones_q).astype(BF16)

    fz = _dot(xb, wfz_ref[...]) + bf_ref[...]
    r = lax.broadcasted_iota(jnp.int32, (tm, LANES), 0) + i * tm
    c = lax.broadcasted_iota(jnp.int32, (tm, LANES), 1)
    lf = jnp.where((c < N_HEADS) & (r >= LEAD), _log_sigmoid(fz), 0.0)

    def split3(t):
        a = t.astype(BF16)
        rem = t - a.astype(F32)
        bb = rem.astype(BF16)
        cc = (rem - bb.astype(F32)).astype(BF16)
        return jnp.concatenate([a, bb, cc], axis=1)

    rr = lax.broadcasted_iota(jnp.int32, (tm, tm), 0)
    cc = lax.broadcasted_iota(jnp.int32, (tm, tm), 1)
    tri = jnp.where(cc <= rr, 1.0, 0.0).astype(BF16)

    @pl.when(i == 0)
    def _():
        carry_ref[...] = jnp.zeros_like(carry_ref)

    c3 = _dot(tri, split3(lf))
    cum = c3[:, :LANES] + c3[:, LANES:2 * LANES] + c3[:, 2 * LANES:] + carry_ref[0:1, :]
    carry_ref[...] = jnp.broadcast_to(cum[tm - 1:tm, :], carry_ref.shape)
    kf_ref[0] = (_dot(xb, wkf_ref[...]) + _dot(split3(-cum), pf_ref[...])).astype(BF16)


def _odd_proj(h, wts, tm):
    b, n, d = h.shape
    row = lambda j, i: (j, i, 0)
    const = lambda j, i: (0, 0)
    full = lambda a: pl.BlockSpec(a.shape, const)
    wide = N_HEADS * LANES
    outs = [(N_HEADS * HEAD_DIM, BF16), (SWA_KV_HEADS * LANES, BF16), (2 * SWA_KV_HEADS * LANES, BF16),
            (wide, BF16), (wide, BF16), (wide, BF16), (d, F32)]
    return pl.pallas_call(
        functools.partial(_odd_proj_kernel, tm=tm),
        grid=(b, n // tm),
        in_specs=[pl.BlockSpec((1, tm, d), row)] + [full(a) for a in wts],
        out_specs=[pl.BlockSpec((1, tm, c), row) for c, _ in outs],
        out_shape=[jax.ShapeDtypeStruct((b, n, c), t) for c, t in outs],
        scratch_shapes=[pltpu.VMEM((8, LANES), F32)],
        compiler_params=pltpu.CompilerParams(
            dimension_semantics=("parallel", "arbitrary"), vmem_limit_bytes=VMEM_LIMIT),
        name="odd_proj",
    )(h, *wts)


def _out_kernel(ma_ref, mb_ref, gate_ref, h_ref, w_ref, g_ref, b_ref, o_ref):
    half = ma_ref.shape[-1]
    gate = gate_ref[...]
    act = gate * jax.nn.sigmoid(gate)
    ya = _dot((ma_ref[...] * act[:, :half]).astype(BF16), w_ref[:half, :])
    yb = _dot((mb_ref[...] * act[:, half:]).astype(BF16), w_ref[half:, :])
    r = DN_ALPHA * h_ref[...] + (ya + yb)
    mu = jnp.mean(r, axis=-1, keepdims=True)
    cen = r - mu
    var = jnp.mean(cen * cen, axis=-1, keepdims=True)
    o_ref[...] = cen * lax.rsqrt(var + LN_EPS) * g_ref[...] + b_ref[...]


def _out_proj(ma, mb, gate, h, w, g, bias, tm):
    rows, d = h.shape
    half = ma.shape[-1]
    row = lambda i: (i, 0)
    const = lambda i: (0, 0)
    return pl.pallas_call(
        _out_kernel,
        grid=(rows // tm,),
        in_specs=[pl.BlockSpec((tm, half), row), pl.BlockSpec((tm, half), row),
                  pl.BlockSpec((tm, d), row), pl.BlockSpec((tm, d), row),
                  pl.BlockSpec(w.shape, const), pl.BlockSpec((1, d), const), pl.BlockSpec((1, d), const)],
        out_specs=pl.BlockSpec((tm, d), row),
        out_shape=jax.ShapeDtypeStruct((rows, d), F32),
        compiler_params=pltpu.CompilerParams(
            dimension_semantics=("parallel",), vmem_limit_bytes=VMEM_LIMIT),
        name="out_proj_ln",
    )(ma, mb, gate, h, w, g, bias)


def _pad_heads(w, odd_shift):
    k = w.shape[0]
    w = w.reshape(k, N_HEADS, HEAD_DIM)
    z = jnp.zeros_like(w)
    lo = jnp.concatenate([w, z], axis=-1)
    if odd_shift:
        hi = jnp.concatenate([z, w], axis=-1)
        odd = (jnp.arange(N_HEADS) % 2 == 1)[None, :, None]
        lo = jnp.where(odd, hi, lo)
    return lo.reshape(k, N_HEADS * LANES)


def _even_weights(w_in, g_cq, g_ckv, w_uq, w_ukv):
    a = N_HEADS * HEAD_DIM
    o = 0
    wq, wk, wv = w_in[:, :a], w_in[:, a:2 * a], w_in[:, 2 * a:3 * a]
    o = 3 * a
    w_c = w_in[:, o:o + 2 * MLA_RANK]
    o += 2 * MLA_RANK
    w_kr = w_in[:, o:o + MLA_ROPE]
    o += MLA_ROPE
    w_g = w_in[:, o:]
    w_sb = jnp.concatenate([wq * HEAD_DIM ** -0.5, wk, wv], axis=1)
    hr = MLA_ROPE // 2
    zpad = jnp.zeros((w_in.shape[0], LANES - MLA_ROPE), w_in.dtype)
    w_kr2 = jnp.concatenate([w_kr, zpad, w_kr[:, hr:], w_kr[:, :hr], zpad], axis=1)

    uq = w_uq.reshape(MLA_RANK, N_HEADS, MLA_NOPE + MLA_ROPE)
    nope, r1, r2 = uq[..., :MLA_NOPE], uq[..., MLA_NOPE:MLA_NOPE + hr], uq[..., MLA_NOPE + hr:]
    z32 = jnp.zeros((MLA_RANK, N_HEADS, LANES - MLA_NOPE - MLA_ROPE), w_uq.dtype)
    w_a = jnp.concatenate([nope, r1, r2, z32], axis=-1).reshape(MLA_RANK, N_HEADS * LANES)
    w_b = jnp.concatenate([jnp.zeros_like(nope), r2, r1, z32], axis=-1).reshape(MLA_RANK, N_HEADS * LANES)
    ukv = w_ukv.reshape(MLA_RANK, N_HEADS, MLA_NOPE + HEAD_DIM)
    w_k = _pad_heads(ukv[..., :MLA_NOPE].reshape(MLA_RANK, -1), False)
    w_v = _pad_heads(ukv[..., MLA_NOPE:].reshape(MLA_RANK, -1), True)
    p_k = np.zeros((LANES, N_HEADS * LANES), np.float32)
    for h in range(N_HEADS):
        for t in range(MLA_ROPE):
            p_k[t, h * LANES + MLA_NOPE + t] = 1.0
    bf = lambda t: t.astype(BF16)
    return (bf(w_sb), bf(w_c), bf(w_kr2), bf(w_g), g_cq[None, :], g_ckv[None, :],
            bf(w_a), bf(w_b), bf(w_k), bf(w_v), jnp.asarray(p_k, BF16))


def _rope_tables(n):
    hr = MLA_ROPE // 2
    pos = jnp.arange(n, dtype=F32) - LEAD
    inv = ROPE_BASE ** (-jnp.arange(hr, dtype=F32) / hr)
    ang = pos[:, None] * inv[None, :]
    cos, sin = jnp.cos(ang), jnp.sin(ang)
    ones = jnp.ones((n, MLA_NOPE), F32)
    z64 = jnp.zeros((n, MLA_NOPE), F32)
    z32 = jnp.zeros((n, LANES - MLA_NOPE - MLA_ROPE), F32)
    scale = (MLA_NOPE + MLA_ROPE) ** -0.5
    c_head = jnp.concatenate([ones, cos, cos, z32], axis=1) * scale
    s_head = jnp.concatenate([z64, -sin, sin, z32], axis=1) * scale
    c_q = jnp.tile(c_head, (1, N_HEADS))
    s_q = jnp.tile(s_head, (1, N_HEADS))
    z96 = jnp.zeros((n, LANES - MLA_ROPE), F32)
    c_k = jnp.concatenate([cos, cos, z96], axis=1)
    s_k = jnp.concatenate([-sin, sin, z96], axis=1)
    return c_q, s_q, c_k, s_k


def _odd_weights(w_in, b_forget):
    a = N_HEADS * HEAD_DIM
    kvw = SWA_KV_HEADS * HEAD_DIM
    o = 0
    wqc = w_in[:, :a]
    o = a
    wkc = w_in[:, o:o + kvw]
    o += kvw
    wvc = w_in[:, o:o + kvw]
    o += kvw
    wqd, wkd, wvd = w_in[:, o:o + a], w_in[:, o + a:o + 2 * a], w_in[:, o + 2 * a:o + 3 * a]
    o += 3 * a
    wfz = w_in[:, o:o + N_HEADS]
    o += N_HEADS
    w_g = w_in[:, o:]
    scale = HEAD_DIM ** -0.5
    k0, k1 = wkc[:, :HEAD_DIM], wkc[:, HEAD_DIM:]
    v0, v1 = wvc[:, :HEAD_DIM], wvc[:, HEAD_DIM:]
    z = jnp.zeros_like(v0)
    w_kc = jnp.concatenate([k0, k0, k1, k1], axis=1)
    w_vc = jnp.concatenate([v0, z, z, v0, v1, z, z, v1], axis=1)
    w_fz = jnp.concatenate([wfz, jnp.zeros((w_in.shape[0], LANES - N_HEADS), w_in.dtype)], axis=1)
    b_f = jnp.concatenate([b_forget, jnp.zeros((LANES - N_HEADS,), b_forget.dtype)])[None, :]
    p_f = np.zeros((3 * LANES, N_HEADS * LANES), np.float32)
    for t in range(3):
        for h in range(N_HEADS):
            p_f[t * LANES + h, h * LANES + HALF + t] = 1.0
    bf = lambda t: t.astype(BF16)
    return (bf(wqc * scale), bf(w_kc), bf(w_vc), bf(_pad_heads(wqd * scale, False)),
            bf(_pad_heads(wkd, False)), bf(_pad_heads(wvd, True)), bf(w_fz), bf(w_g),
            b_f.astype(F32), jnp.asarray(p_f, BF16))


def _row_tile(nb, cap):
    g = max(t for t in range(1, cap + 1) if nb % t == 0)
    return g * BLK


def kernel(x, meta_tokens, w_in_even, g_cq, g_ckv, w_uq, w_ukv, w_out_even,
           w_in_odd, b_forget, sink_logits, w_out_odd, ln_gain, ln_bias):
    b, s, d = x.shape
    assert s % BLK == 0 and d == D_MODEL
    n = BLK + s
    nb = n // BLK
    h = jnp.concatenate([jnp.zeros((b, LEAD, d), x.dtype),
                         jnp.broadcast_to(meta_tokens[None].astype(x.dtype), (b, N_META, d)), x], axis=1)
    tabs = _rope_tables(n)
    tm_proj = _row_tile(nb, 3)
    tm_out = _row_tile(b * nb, 4)
    for layer in range(DEPTH):
        i = layer // 2
        if layer % 2 == 0:
            wts = _even_weights(w_in_even[i], g_cq[i], g_ckv[i], w_uq[i], w_ukv[i])
            qs, ks, vs, qm, km, vm, gate = _even_proj(h, wts, tabs, tm_proj)
            mix_a = _sb_attention(qs, ks, vs)
            mix_b = _softmax_attention(qm, km, vm, "chunk")
            w_out = w_out_even[i]
        else:
            wts = _odd_weights(w_in_odd[i], b_forget[i])
            qc, kc, vc, qf, kf, vf, gate = _odd_proj(h, wts, tm_proj)
            mix_a = _swa_attention(sink_logits[i].astype(F32), qc, kc, vc)
            mix_b = _softmax_attention(qf, kf, vf, "causal")
            w_out = w_out_odd[i]
        rows = b * n
        h = _out_proj(mix_a.reshape(rows, -1), mix_b.reshape(rows, -1), gate.reshape(rows, d),
                      h.reshape(rows, d), w_out.astype(BF16), ln_gain[layer][None, :],
                      ln_bias[layer][None, :], tm_out).reshape(b, n, d)
    return h[:, BLK:]
```

```python
import functools

import jax
import jax.numpy as jnp
import numpy as np
from jax import lax
from jax.experimental import pallas as pl
from jax.experimental.pallas import tpu as pltpu

D_MODEL = 1024
CHUNK = 64
N_META = 16
HEAD_DIM = 64
N_HEADS = 8
MLA_NOPE = 64
MLA_ROPE = 32
MLA_RANK = 256
SWA_KV_HEADS = 2
WINDOW_CHUNKS = 2
ROPE_BASE = 10000.0
DEPTH = 4
DN_ALPHA = (2 * DEPTH) ** 0.25
LN_EPS = 1e-5
RMS_EPS = 1e-6
NEG = -1e30
NO_KEY = 1 << 30
SIGN_BIT = -(1 << 31)

LANES = 128
BLK = 128
TB = 4
TQ = TB * BLK
LEAD = BLK - N_META
HALF = LANES // 2
VMEM_LIMIT = 56 * 1024 * 1024

F32 = jnp.float32
BF16 = jnp.bfloat16


def _dot(a, b):
    return jnp.dot(a, b, preferred_element_type=F32)


def _chunk_of(pos):
    return jnp.maximum((pos >> 6) - 1, 0)


def _log_sigmoid(x):
    return jnp.minimum(x, 0.0) - jnp.log(1.0 + jnp.exp(-jnp.abs(x)))


def _al(x):
    return x if isinstance(x, int) else pl.multiple_of(x, BLK)


def _iotas(rows):
    return (lax.broadcasted_iota(jnp.int32, (rows, BLK), 0),
            lax.broadcasted_iota(jnp.int32, (rows, BLK), 1))


def _sb_kernel(q_ref, kt_ref, v_ref, o_ref, u_ref, carry_ref, acc_ref, *, nt):
    r2 = lax.broadcasted_iota(jnp.int32, (2 * BLK, 2 * BLK), 0)
    c2 = lax.broadcasted_iota(jnp.int32, (2 * BLK, 2 * BLK), 1)
    u_ref[...] = jnp.where(((r2 >= BLK) == (c2 >= BLK)) & ((r2 & (BLK - 1)) > (c2 & (BLK - 1))),
                           1.0, 0.0).astype(BF16)
    zero_kt = jnp.zeros((HALF, BLK), BF16)

    def step(row0, off, rows, jb, masked, lo=LEAD):
        r0 = _al(row0 + off)
        k0 = _al(jb * BLK)
        sl = slice(off, off + rows)
        ktb = kt_ref[0, :, pl.ds(k0, BLK)]
        kbd = jnp.concatenate([jnp.concatenate([ktb[:HALF], zero_kt], axis=1),
                               jnp.concatenate([zero_kt, ktb[HALF:]], axis=1)], axis=0)
        z2 = _dot(q_ref[0, pl.ds(r0, rows), :], kbd)
        vst = jnp.concatenate([v_ref[0, pl.ds(k0, BLK), :LANES],
                               v_ref[0, pl.ds(k0, BLK), LANES:]], axis=0)
        if masked:
            rowi, lane = _iotas(rows)
            kpos = k0 + lane
            msk = (kpos >= lo) & (kpos < r0 + rowi)
        stays, log_betas = [], []
        for h in range(2):
            z = z2[:, h * BLK:(h + 1) * BLK]
            neg_abs = lax.bitcast_convert_type(
                lax.bitcast_convert_type(z, jnp.int32) | SIGN_BIT, F32)
            stay = jnp.maximum(z, 0.0) + jnp.log(1.0 + jnp.exp(neg_abs))
            log_betas.append(z - stay)
            stays.append(jnp.where(msk, stay, 0.0) if masked else stay)
        cs = _dot(jnp.concatenate([s.astype(BF16) for s in stays], axis=1), u_ref[...])
        ws = []
        for h in range(2):
            carry = carry_ref[h, sl, :]
            w = jnp.exp(log_betas[h] - (cs[:, h * BLK:(h + 1) * BLK] + carry))
            carry_ref[h, sl, :] = carry + jnp.sum(stays[h], axis=1, keepdims=True)
            if masked:
                w = jnp.where(msk, w, 0.0)
            ws.append(w.astype(BF16))
        acc_ref[sl, :] += _dot(jnp.concatenate(ws, axis=1), vst)

    def reset(rows):
        carry_ref[:, :rows, :] = jnp.zeros((2, rows, BLK), F32)
        acc_ref[:rows, :] = jnp.zeros((rows, BLK), F32)

    reset(BLK)
    step(0, 0, BLK, 0, True)
    o_ref[0, :BLK, :] = acc_ref[:BLK, :]

    def tile(t, _):
        row0 = BLK + t * TQ
        jb0 = 1 + t * TB
        reset(TQ)
        for d in reversed(range(TB)):
            step(row0, d * BLK, TQ - d * BLK, jb0 + d, True)

        def body(i, c):
            for u in range(TB):
                step(row0, 0, TQ, jb0 - 1 - (i * TB + u), False)
            return c

        lax.fori_loop(0, t, body, 0)
        step(row0, 0, TQ, 0, True)
        o_ref[0, pl.ds(_al(row0), TQ), :] = acc_ref[...]
        return 0

    lax.fori_loop(0, nt, tile, 0)


def _sb_attention(q, kt, v):
    b, n, w = q.shape
    nt = (n - BLK) // TQ
    return pl.pallas_call(
        functools.partial(_sb_kernel, nt=nt),
        grid=(b, w // LANES),
        in_specs=[pl.BlockSpec((1, n, LANES), lambda i, j: (i, 0, j)),
                  pl.BlockSpec((1, LANES, n), lambda i, j: (i, j, 0)),
                  pl.BlockSpec((1, n, 2 * LANES), lambda i, j: (i, 0, j))],
        out_specs=pl.BlockSpec((1, n, LANES), lambda i, j: (i, 0, j)),
        out_shape=jax.ShapeDtypeStruct((b, n, w), F32),
        scratch_shapes=[pltpu.VMEM((2 * BLK, 2 * BLK), BF16),
                        pltpu.VMEM((2, TQ, BLK), F32),
                        pltpu.VMEM((TQ, BLK), F32)],
        compiler_params=pltpu.CompilerParams(
            dimension_semantics=("parallel", "parallel"), vmem_limit_bytes=VMEM_LIMIT),
        name="sb_attention",
    )(q, kt, v)


def _softmax_kernel(q_ref, kt_ref, v_ref, o_ref, s_ref, sd_ref, mx_ref, mb_ref, acc_ref, *, nt, mode):
    def visible(kpos, qpos, lo):
        if mode == "chunk":
            return (kpos >= lo) & (_chunk_of(kpos) <= _chunk_of(qpos))
        return (kpos >= lo) & (kpos <= qpos)

    def chunk_loop(n_chunks, chunk_fn):
        def pair(i, carry):
            chunk_fn(2 * i)
            chunk_fn(2 * i + 1)
            return carry

        lax.fori_loop(0, n_chunks >> 1, pair, 0)

        @pl.when((n_chunks & 1) == 1)
        def _():
            chunk_fn(n_chunks - 1)

    def emit(row0, rows, n_chunks, frames):
        frs = [slice(h * LANES, (h + 1) * LANES) for h in range(2)]
        r0 = _al(row0)
        rowi, lane = _iotas(rows)

        def chunk_scores(c, h):
            k0 = _al(BLK + c * TQ)
            return _dot(q_ref[0, pl.ds(r0, rows), frs[h]], kt_ref[0, frs[h], pl.ds(k0, TQ)])

        def fold_max(c, h, s):
            s_ref[h, :, pl.ds(_al(BLK + c * TQ), TQ)] = s
            m4 = jnp.maximum(jnp.maximum(s[:, :BLK], s[:, BLK:2 * BLK]),
                             jnp.maximum(s[:, 2 * BLK:3 * BLK], s[:, 3 * BLK:]))
            mx_ref[h] = jnp.maximum(mx_ref[h], m4)

        for h, fr in enumerate(frs):
            s0 = _dot(q_ref[0, pl.ds(r0, rows), fr], kt_ref[0, fr, :BLK])
            s0 = jnp.where(visible(lane, r0 + rowi, LEAD), s0, NEG)
            sd_ref[h, :rows, :] = s0
            mx_ref[h, :rows, :] = s0
        if frames:
            qpos = r0 + lax.broadcasted_iota(jnp.int32, (rows, TQ), 0)
            kpos = BLK + n_chunks * TQ + lax.broadcasted_iota(jnp.int32, (rows, TQ), 1)
            own = visible(kpos, qpos, 0)
            for h in range(2):
                fold_max(n_chunks, h, jnp.where(own, chunk_scores(n_chunks, h), NEG))

            def p1(c):
                for h in range(2):
                    fold_max(c, h, chunk_scores(c, h))

            chunk_loop(n_chunks, p1)

        for h, fr in enumerate(frs):
            m_row = jnp.max(mx_ref[h, :rows, :], axis=1, keepdims=True)
            mb_ref[h, :rows, :] = jnp.broadcast_to(m_row, (rows, BLK))
            p0 = jnp.exp(sd_ref[h, :rows, :] - mb_ref[h, :rows, :]).astype(BF16)
            acc_ref[h, :rows, :] = _dot(p0, v_ref[0, :BLK, fr])

        if frames:
            def p2(c):
                k0 = _al(BLK + c * TQ)
                for h, fr in enumerate(frs):
                    mb = mb_ref[h]
                    p = jnp.exp(s_ref[h, :, pl.ds(k0, TQ)]
                                - jnp.concatenate([mb] * TB, axis=1)).astype(BF16)
                    acc_ref[h] += _dot(p, v_ref[0, pl.ds(k0, TQ), fr])

            chunk_loop(n_chunks + 1, p2)

        outs = []
        for h in range(2):
            acc = acc_ref[h, :rows, :]
            outs.append(acc / pltpu.roll(acc, HALF, 1))
        o_ref[0, pl.ds(r0, rows), :] = jnp.where(lane < HALF, outs[0], outs[1])

    emit(0, BLK, 0, False)

    def tile(t, _):
        emit(BLK + t * TQ, TQ, t, True)
        return 0

    lax.fori_loop(0, nt, tile, 0)


def _softmax_attention(q, kt, v, mode):
    b, n, w = q.shape
    nt = (n - BLK) // TQ
    return pl.pallas_call(
        functools.partial(_softmax_kernel, nt=nt, mode=mode),
        grid=(b, w // (2 * LANES)),
        in_specs=[pl.BlockSpec((1, n, 2 * LANES), lambda i, j: (i, 0, j)),
                  pl.BlockSpec((1, 2 * LANES, n), lambda i, j: (i, j, 0)),
                  pl.BlockSpec((1, n, 2 * LANES), lambda i, j: (i, 0, j))],
        out_specs=pl.BlockSpec((1, n, LANES), lambda i, j: (i, 0, j)),
        out_shape=jax.ShapeDtypeStruct((b, n, w // 2), F32),
        scratch_shapes=[pltpu.VMEM((2, TQ, n), F32),
                        pltpu.VMEM((2, TQ, BLK), F32),
                        pltpu.VMEM((2, TQ, BLK), F32),
                        pltpu.VMEM((2, TQ, BLK), F32),
                        pltpu.VMEM((2, TQ, BLK), F32)],
        compiler_params=pltpu.CompilerParams(
            dimension_semantics=("parallel", "parallel"), vmem_limit_bytes=VMEM_LIMIT),
        name="softmax_attention_" + mode,
    )(q, kt, v)


def _swa_kernel(sink_ref, q_ref, kt_ref, v_ref, o_ref, *, nb):
    rowi, lane = _iotas(BLK)
    group = N_HEADS // SWA_KV_HEADS
    zero_kt = jnp.zeros((HALF, BLK), BF16)
    r2 = lax.broadcasted_iota(jnp.int32, (2 * BLK, BLK), 0)
    c2 = lax.broadcasted_iota(jnp.int32, (2 * BLK, BLK), 1)
    ones_st = jnp.where((r2 >= BLK) == (c2 >= HALF), 1.0, 0.0).astype(BF16)

    def q_block(m, _):
        r0 = pl.multiple_of(m * BLK, BLK)
        qpos = r0 + rowi
        cq = _chunk_of(qpos)
        slots = ((0, jnp.where(m >= 1, LEAD, NO_KEY)),
                 (jnp.maximum(m - 1, 0), jnp.where(m >= 2, 0, NO_KEY)),
                 (m, LEAD))
        biases = []
        for jb, lo in slots:
            kpos = jb * BLK + lane
            ck = _chunk_of(kpos)
            msk = (kpos >= lo) & (ck <= cq) & ((cq - ck <= WINDOW_CHUNKS) | (kpos < BLK))
            biases.append((msk, jnp.abs(qpos - kpos).astype(F32)))
        for g in range(SWA_KV_HEADS):
            kbds, wcats = [], []
            for jb, _ in slots:
                k0 = _al(jb * BLK)
                ktg =kt_ref[0, g * HALF:(g + 1) * HALF, pl.ds(k0, BLK)]
                kbds.append(jnp.concatenate([jnp.concatenate([ktg, zero_kt], axis=1),
                                             jnp.concatenate([zero_kt, ktg], axis=1)], axis=0))
                vst = jnp.concatenate([v_ref[0, pl.ds(k0, BLK), (2 * g) * LANES:(2 * g + 1) * LANES],
                                       v_ref[0, pl.ds(k0, BLK), (2 * g + 1) * LANES:(2 * g + 2) * LANES]],
                                      axis=0)
                wcats.append(jnp.concatenate([vst, ones_st], axis=1))
            for pr in range(group // 2):
                pair = g * (group // 2) + pr
                q2 = q_ref[0, pl.ds(r0, BLK), pair * LANES:(pair + 1) * LANES]
                zs = [_dot(q2, kbd) for kbd in kbds]
                ps, sink_terms = [[], []], []
                for h in range(2):
                    head = 2 * pair + h
                    slope = 2.0 ** (-8.0 * (head + 1) / N_HEADS)
                    ss = [jnp.where(msk, z[:, h * BLK:(h + 1) * BLK] - slope * dist, NEG)
                          for z, (msk, dist) in zip(zs, biases)]
                    sink = sink_ref[head]
                    mx = jnp.maximum(jnp.maximum(ss[0], ss[1]), ss[2])
                    m_row = jnp.maximum(jnp.max(mx, axis=1, keepdims=True), sink)
                    ps[h] = [jnp.exp(s - m_row).astype(BF16) for s in ss]
                    sink_terms.append(jnp.exp(sink - m_row))
                res = jnp.zeros((BLK, 2 * BLK), F32)
                for i in range(len(slots)):
                    res = res + _dot(jnp.concatenate([ps[0][i], ps[1][i]], axis=1), wcats[i])
                denom = res[:, BLK:] + jnp.where(lane < HALF, sink_terms[0], sink_terms[1])
                o_ref[0, pl.ds(r0, BLK), pair * LANES:(pair + 1) * LANES] = res[:, :BLK] / denom
        return 0

    lax.fori_loop(0, nb, q_block, 0)


def _swa_attention(sinks, q, kt, v):
    b, n, w = q.shape
    nb = n // BLK
    return pl.pallas_call(
        functools.partial(_swa_kernel, nb=nb),
        grid=(b,),
        in_specs=[
            pl.BlockSpec(memory_space=pltpu.SMEM),
            pl.BlockSpec((1, n, w), lambda i: (i, 0, 0)),
            pl.BlockSpec((1, kt.shape[1], n), lambda i: (i, 0, 0)),
            pl.BlockSpec((1, n, v.shape[2]), lambda i: (i, 0, 0)),
        ],
        out_specs=pl.BlockSpec((1, n, w), lambda i: (i, 0, 0)),
        out_shape=jax.ShapeDtypeStruct((b, n, w), F32),
        compiler_params=pltpu.CompilerParams(
            dimension_semantics=("parallel",), vmem_limit_bytes=VMEM_LIMIT),
        name="swa_attention",
    )(sinks, q, kt, v)


def _even_proj_kernel(x_ref, wsb_ref, wc_ref, wkr_ref, wg_ref, gq_ref, gkv_ref,
                      wa_ref, wb_ref, wk_ref, wv_ref, pk_ref, cq_ref, sq_ref, ck_ref, sk_ref,
                      qs_ref, kst_ref, vs_ref, qm_ref, kmt_ref, vm_ref, gate_ref):
    xb = x_ref[0].astype(BF16)
    w = N_HEADS * HEAD_DIM
    sb = _dot(xb, wsb_ref[...])
    qs_ref[0] = sb[:, :w].astype(BF16)
    kst_ref[0] = sb[:, w:2 * w].T.astype(BF16)
    vs_ref[0] = sb[:, 2 * w:].astype(BF16)
    gate_ref[0] = _dot(xb, wg_ref[...])

    c = _dot(xb, wc_ref[...])

    def rms(t, g):
        return (t * lax.rsqrt(jnp.mean(t * t, axis=-1, keepdims=True) + RMS_EPS) * g).astype(BF16)

    xq = rms(c[:, :MLA_RANK], gq_ref[...])
    xkv = rms(c[:, MLA_RANK:], gkv_ref[...])
    qm = _dot(xq, wa_ref[...]) * cq_ref[...] + _dot(xq, wb_ref[...]) * sq_ref[...]
    qm_ref[0] = qm.astype(BF16)
    kr = _dot(xb, wkr_ref[...])
    krr = (kr[:, :LANES] * ck_ref[...] + kr[:, LANES:] * sk_ref[...]).astype(BF16)
    kmt_ref[0] = (_dot(xkv, wk_ref[...]) + _dot(krr, pk_ref[...])).T.astype(BF16)
    col = lax.broadcasted_iota(jnp.int32, (1, N_HEADS * LANES), 1)
    ones_v = jnp.where(((col >> 6) & 1) != ((col >> 7) & 1), 1.0, 0.0)
    vm_ref[0] = (_dot(xkv, wv_ref[...]) + ones_v).astype(BF16)


def _even_proj(h, wts, tabs, tm):
    b, n, d = h.shape
    grid = (n // tm, b)
    row = lambda i, j: (j, i, 0)
    col = lambda i, j: (j, 0, i)
    const = lambda i, j: (0, 0)
    tab = lambda i, j: (i, 0)
    full = lambda a: pl.BlockSpec(a.shape, const)
    wide = N_HEADS * LANES
    narrow = N_HEADS * HEAD_DIM
    outs = [(narrow, BF16, False), (narrow, BF16, True), (wide, BF16, False),
            (wide, BF16, False), (wide, BF16, True), (wide, BF16, False), (d, F32, False)]
    return pl.pallas_call(
        _even_proj_kernel,
        grid=grid,
        in_specs=[pl.BlockSpec((1, tm, d), row)] + [full(a) for a in wts] + [
            pl.BlockSpec((tm, wide), tab), pl.BlockSpec((tm, wide), tab),
            pl.BlockSpec((tm, LANES), tab), pl.BlockSpec((tm, LANES), tab)],
        out_specs=[pl.BlockSpec((1, c, tm), col) if tr else pl.BlockSpec((1, tm, c), row)
                   for c, _, tr in outs],
        out_shape=[jax.ShapeDtypeStruct((b, c, n) if tr else (b, n, c), t) for c, t, tr in outs],
        compiler_params=pltpu.CompilerParams(
            dimension_semantics=("parallel", "parallel"), vmem_limit_bytes=VMEM_LIMIT),
        name="even_proj",
    )(h, *wts, *tabs)


def _odd_proj_kernel(x_ref, wqc_ref, wkc_ref, wvc_ref, wqf_ref, wkf_ref, wvf_ref, wfz_ref, wg_ref,
                     bf_ref, pf_ref,
                     qc_ref, kct_ref, vc_ref, qf_ref, kft_ref, vf_ref, gate_ref, carry_ref, *, tm):
    i = pl.program_id(1)
    xb = x_ref[0].astype(BF16)
    qc_ref[0] = _dot(xb, wqc_ref[...]).astype(BF16)
    kct_ref[0] = _dot(xb, wkc_ref[...]).T.astype(BF16)
    vc_ref[0] = _dot(xb, wvc_ref[...]).astype(BF16)
    gate_ref[0] = _dot(xb, wg_ref[...])

    colf = lax.broadcasted_iota(jnp.int32, (1, N_HEADS * LANES), 1)
    ones_f = jnp.where(((colf >> 6) & 1) != ((colf >> 7) & 1), 1.0, 0.0)
    vf_ref[0] = (_dot(xb, wvf_ref[...]) + ones_f).astype(BF16)
    lf_lane = colf & (LANES - 1)
    ones_q = jnp.where((lf_lane >= HALF) & (lf_lane < HALF + 3), 1.0, 0.0)
    qf_ref[0] = (_dot(xb, wqf_ref[...]) + ones_q).astype(BF16)

    fz = _dot(xb, wfz_ref[...]) + bf_ref[...]
    r = lax.broadcasted_iota(jnp.int32, (tm, LANES), 0) + i * tm
    c = lax.broadcasted_iota(jnp.int32, (tm, LANES), 1)
    lf = jnp.where((c < N_HEADS) & (r >= LEAD), _log_sigmoid(fz), 0.0)

    def split3(t):
        a = t.astype(BF16)
        rem = t - a.astype(F32)
        bb = rem.astype(BF16)
        cc = (rem - bb.astype(F32)).astype(BF16)
        return jnp.concatenate([a, bb, cc], axis=1)

    rr = lax.broadcasted_iota(jnp.int32, (tm, tm), 0)
    cc = lax.broadcasted_iota(jnp.int32, (tm, tm), 1)
    tri = jnp.where(cc <= rr, 1.0, 0.0).astype(BF16)

    @pl.when(i == 0)
    def _():
        carry_ref[...] = jnp.zeros_like(carry_ref)

    c3 = _dot(tri, split3(lf))
    cum = c3[:, :LANES] + c3[:, LANES:2 * LANES] + c3[:, 2 * LANES:] + carry_ref[0:1, :]
    carry_ref[...] = jnp.broadcast_to(cum[tm - 1:tm, :], carry_ref.shape)
    kft_ref[0] = (_dot(xb, wkf_ref[...]) + _dot(split3(-cum), pf_ref[...])).T.astype(BF16)


def _odd_proj(h, wts, tm):
    b, n, d = h.shape
    row = lambda j, i: (j, i, 0)
    col = lambda j, i: (j, 0, i)
    const = lambda j, i: (0, 0)
    full = lambda a: pl.BlockSpec(a.shape, const)
    wide = N_HEADS * LANES
    narrow = N_HEADS * HEAD_DIM
    outs = [(narrow, BF16, False), (SWA_KV_HEADS * HEAD_DIM, BF16, True), (narrow, BF16, False),
            (wide, BF16, False), (wide, BF16, True), (wide, BF16, False), (d, F32, False)]
    return pl.pallas_call(
        functools.partial(_odd_proj_kernel, tm=tm),
        grid=(b, n // tm),
        in_specs=[pl.BlockSpec((1, tm, d), row)] + [full(a) for a in wts],
        out_specs=[pl.BlockSpec((1, c, tm), col) if tr else pl.BlockSpec((1, tm, c), row)
                   for c, _, tr in outs],
        out_shape=[jax.ShapeDtypeStruct((b, c, n) if tr else (b, n, c), t) for c, t, tr in outs],
        scratch_shapes=[pltpu.VMEM((8, LANES), F32)],
        compiler_params=pltpu.CompilerParams(
            dimension_semantics=("parallel", "arbitrary"), vmem_limit_bytes=VMEM_LIMIT),
        name="odd_proj",
    )(h, *wts)


def _out_kernel(ma_ref, mb_ref, gate_ref, h_ref, w_ref, g_ref, b_ref, o_ref):
    half = ma_ref.shape[-1]
    gate = gate_ref[...]
    act = gate * jax.nn.sigmoid(gate)
    ya = _dot((ma_ref[...] * act[:, :half]).astype(BF16), w_ref[:half, :])
    yb = _dot((mb_ref[...] * act[:, half:]).astype(BF16), w_ref[half:, :])
    r = DN_ALPHA * h_ref[...] + (ya + yb)
    mu = jnp.mean(r, axis=-1, keepdims=True)
    cen = r - mu
    var = jnp.mean(cen * cen, axis=-1, keepdims=True)
    o_ref[...] = cen * lax.rsqrt(var + LN_EPS) * g_ref[...] + b_ref[...]


def _out_proj(ma, mb, gate, h, w, g, bias, tm):
    rows, d = h.shape
    half = ma.shape[-1]
    row = lambda i: (i, 0)
    const = lambda i: (0, 0)
    return pl.pallas_call(
        _out_kernel,
        grid=(rows // tm,),
        in_specs=[pl.BlockSpec((tm, half), row), pl.BlockSpec((tm, half), row),
                  pl.BlockSpec((tm, d), row), pl.BlockSpec((tm, d), row),
                  pl.BlockSpec(w.shape, const), pl.BlockSpec((1, d), const), pl.BlockSpec((1, d), const)],
        out_specs=pl.BlockSpec((tm, d), row),
        out_shape=jax.ShapeDtypeStruct((rows, d), F32),
        compiler_params=pltpu.CompilerParams(
            dimension_semantics=("parallel",), vmem_limit_bytes=VMEM_LIMIT),
        name="out_proj_ln",
    )(ma, mb, gate, h, w, g, bias)


def _pad_heads(w, odd_shift):
    k = w.shape[0]
    w = w.reshape(k, N_HEADS, HEAD_DIM)
    z = jnp.zeros_like(w)
    lo = jnp.concatenate([w, z], axis=-1)
    if odd_shift:
        hi = jnp.concatenate([z, w], axis=-1)
        odd = (jnp.arange(N_HEADS) % 2 == 1)[None, :, None]
        lo = jnp.where(odd, hi, lo)
    return lo.reshape(k, N_HEADS * LANES)


def _even_weights(w_in, g_cq, g_ckv, w_uq, w_ukv):
    a = N_HEADS * HEAD_DIM
    wq, wk, wv = w_in[:, :a], w_in[:, a:2 * a], w_in[:, 2 * a:3 * a]
    o = 3 * a
    w_c = w_in[:, o:o + 2 * MLA_RANK]
    o += 2 * MLA_RANK
    w_kr = w_in[:, o:o + MLA_ROPE]
    o += MLA_ROPE
    w_g = w_in[:, o:]
    w_sb = jnp.concatenate([wq * HEAD_DIM ** -0.5, wk, _pad_heads(wv, True)], axis=1)
    hr = MLA_ROPE // 2
    zpad = jnp.zeros((w_in.shape[0], LANES - MLA_ROPE), w_in.dtype)
    w_kr2 = jnp.concatenate([w_kr, zpad, w_kr[:, hr:], w_kr[:, :hr], zpad], axis=1)

    uq = w_uq.reshape(MLA_RANK, N_HEADS, MLA_NOPE + MLA_ROPE)
    nope, r1, r2 = uq[..., :MLA_NOPE], uq[..., MLA_NOPE:MLA_NOPE + hr], uq[..., MLA_NOPE + hr:]
    z32 = jnp.zeros((MLA_RANK, N_HEADS, LANES - MLA_NOPE - MLA_ROPE), w_uq.dtype)
    w_a = jnp.concatenate([nope, r1, r2, z32], axis=-1).reshape(MLA_RANK, N_HEADS * LANES)
    w_b = jnp.concatenate([jnp.zeros_like(nope), r2, r1, z32], axis=-1).reshape(MLA_RANK, N_HEADS * LANES)
    ukv = w_ukv.reshape(MLA_RANK, N_HEADS, MLA_NOPE + HEAD_DIM)
    w_k = _pad_heads(ukv[..., :MLA_NOPE].reshape(MLA_RANK, -1), False)
    w_v = _pad_heads(ukv[..., MLA_NOPE:].reshape(MLA_RANK, -1), True)
    p_k = np.zeros((LANES, N_HEADS * LANES), np.float32)
    for h in range(N_HEADS):
        for t in range(MLA_ROPE):
            p_k[t, h * LANES + MLA_NOPE + t] = 1.0
    bf = lambda t: t.astype(BF16)
    return (bf(w_sb), bf(w_c), bf(w_kr2), bf(w_g), g_cq[None, :], g_ckv[None, :],
            bf(w_a), bf(w_b), bf(w_k), bf(w_v), jnp.asarray(p_k, BF16))


def _rope_tables(n):
    hr = MLA_ROPE // 2
    pos = jnp.arange(n, dtype=F32) - LEAD
    inv = ROPE_BASE ** (-jnp.arange(hr, dtype=F32) / hr)
    ang = pos[:, None] * inv[None, :]
    cos, sin = jnp.cos(ang), jnp.sin(ang)
    ones = jnp.ones((n, MLA_NOPE), F32)
    z64 = jnp.zeros((n, MLA_NOPE), F32)
    z32 = jnp.zeros((n, LANES - MLA_NOPE - MLA_ROPE), F32)
    scale = (MLA_NOPE + MLA_ROPE) ** -0.5
    c_head = jnp.concatenate([ones, cos, cos, z32], axis=1) * scale
    s_head = jnp.concatenate([z64, -sin, sin, z32], axis=1) * scale
    c_q = jnp.tile(c_head, (1, N_HEADS))
    s_q = jnp.tile(s_head, (1, N_HEADS))
    z96 = jnp.zeros((n, LANES - MLA_ROPE), F32)
    c_k = jnp.concatenate([cos, cos, z96], axis=1)
    s_k = jnp.concatenate([-sin, sin, z96], axis=1)
    return c_q, s_q, c_k, s_k


def _odd_weights(w_in, b_forget):
    a = N_HEADS * HEAD_DIM
    kvw = SWA_KV_HEADS * HEAD_DIM
    wqc = w_in[:, :a]
    o = a
    wkc = w_in[:, o:o + kvw]
    o += kvw
    wvc = w_in[:, o:o + kvw]
    o += kvw
    wqd, wkd, wvd = w_in[:, o:o + a], w_in[:, o + a:o + 2 * a], w_in[:, o + 2 * a:o + 3 * a]
    o += 3 * a
    wfz = w_in[:, o:o + N_HEADS]
    o += N_HEADS
    w_g = w_in[:, o:]
    scale = HEAD_DIM ** -0.5
    v0, v1 = wvc[:, :HEAD_DIM], wvc[:, HEAD_DIM:]
    z = jnp.zeros_like(v0)
    w_vc = jnp.concatenate([v0, z, z, v0, v1, z, z, v1], axis=1)
    w_fz = jnp.concatenate([wfz, jnp.zeros((w_in.shape[0], LANES - N_HEADS), w_in.dtype)], axis=1)
    b_f = jnp.concatenate([b_forget, jnp.zeros((LANES - N_HEADS,), b_forget.dtype)])[None, :]
    p_f = np.zeros((3 * LANES, N_HEADS * LANES), np.float32)
    for t in range(3):
        for h in range(N_HEADS):
            p_f[t * LANES + h, h * LANES + HALF + t] = 1.0
    bf = lambda t: t.astype(BF16)
    return (bf(wqc * scale), bf(wkc), bf(w_vc), bf(_pad_heads(wqd * scale, False)),
            bf(_pad_heads(wkd, False)), bf(_pad_heads(wvd, True)), bf(w_fz), bf(w_g),
            b_f.astype(F32), jnp.asarray(p_f, BF16))


def _row_tile(nb, cap):
    g = max(t for t in range(1, cap + 1) if nb % t == 0)
    return g * BLK


def kernel(x, meta_tokens, w_in_even, g_cq, g_ckv, w_uq, w_ukv, w_out_even,
           w_in_odd, b_forget, sink_logits, w_out_odd, ln_gain, ln_bias):
    b, s, d = x.shape
    assert s % TQ == 0 and d == D_MODEL
    n = BLK + s
    nb = n // BLK
    h = jnp.concatenate([jnp.zeros((b, LEAD, d), x.dtype),
                         jnp.broadcast_to(meta_tokens[None].astype(x.dtype), (b, N_META, d)), x], axis=1)
    tabs = _rope_tables(n)
    tm_proj = _row_tile(nb, 3)
    tm_out = _row_tile(b * nb, 4)
    for layer in range(DEPTH):
        i = layer // 2
        if layer % 2 == 0:
            wts = _even_weights(w_in_even[i], g_cq[i], g_ckv[i], w_uq[i], w_ukv[i])
            qs, kst, vs, qm, kmt, vm, gate = _even_proj(h, wts, tabs, tm_proj)
            mix_a = _sb_attention(qs, kst, vs)
            mix_b = _softmax_attention(qm, kmt, vm, "chunk")
            w_out = w_out_even[i]
        else:
            wts = _odd_weights(w_in_odd[i], b_forget[i])
            qc, kct, vc, qf, kft, vf, gate = _odd_proj(h, wts, tm_proj)
            mix_a = _swa_attention(sink_logits[i].astype(F32), qc, kct, vc)
            mix_b = _softmax_attention(qf, kft, vf, "causal")
            w_out = w_out_odd[i]
        rows = b * n
        h = _out_proj(mix_a.reshape(rows, -1), mix_b.reshape(rows, -1), gate.reshape(rows, d),
                      h.reshape(rows, d), w_out.astype(BF16), ln_gain[layer][None, :],
                      ln_bias[layer][None, :], tm_out).reshape(b, n, d)
    return h[:, BLK:]
```

```python
import functools

import jax
import jax.numpy as jnp
import numpy as np
from jax import lax
from jax.experimental import pallas as pl
from jax.experimental.pallas import tpu as pltpu

D_MODEL = 1024
CHUNK = 64
N_META = 16
HEAD_DIM = 64
N_HEADS = 8
MLA_NOPE = 64
MLA_ROPE = 32
MLA_RANK = 256
SWA_KV_HEADS = 2
WINDOW_CHUNKS = 2
ROPE_BASE = 10000.0
DEPTH = 4
DN_ALPHA = (2 * DEPTH) ** 0.25
LN_EPS = 1e-5
RMS_EPS = 1e-6
NEG = -1e30
NO_KEY = 1 << 30
SIGN_BIT = -(1 << 31)

LANES = 128
BLK = 128
TB = 4
TQ = TB * BLK
LEAD = BLK - N_META
HALF = LANES // 2
VMEM_LIMIT = 56 * 1024 * 1024

F32 = jnp.float32
BF16 = jnp.bfloat16


def _dot(a, b):
    return jnp.dot(a, b, preferred_element_type=F32)


def _chunk_of(pos):
    return jnp.maximum((pos >> 6) - 1, 0)


def _log_sigmoid(x):
    return jnp.minimum(x, 0.0) - jnp.log(1.0 + jnp.exp(-jnp.abs(x)))


def _al(x):
    return x if isinstance(x, int) else pl.multiple_of(x, BLK)


def _gated(mixed, gate):
    return (mixed * (gate * jax.nn.sigmoid(gate))).astype(BF16)


def _iotas(rows):
    return (lax.broadcasted_iota(jnp.int32, (rows, BLK), 0),
            lax.broadcasted_iota(jnp.int32, (rows, BLK), 1))


def _sb_kernel(q_ref, kt_ref, v_ref, g_ref, o_ref, u_ref, carry_ref, acc_ref, *, nt):
    r2 = lax.broadcasted_iota(jnp.int32, (2 * BLK, 2 * BLK), 0)
    c2 = lax.broadcasted_iota(jnp.int32, (2 * BLK, 2 * BLK), 1)
    u_ref[...] = jnp.where(((r2 >= BLK) == (c2 >= BLK)) & ((r2 & (BLK - 1)) > (c2 & (BLK - 1))),
                           1.0, 0.0).astype(BF16)
    zero_kt = jnp.zeros((HALF, BLK), BF16)

    def step(row0, off, rows, jb, masked, lo=LEAD):
        r0 = _al(row0 + off)
        k0 = _al(jb * BLK)
        sl = slice(off, off + rows)
        ktb = kt_ref[0, :, pl.ds(k0, BLK)]
        kbd = jnp.concatenate([jnp.concatenate([ktb[:HALF], zero_kt], axis=1),
                               jnp.concatenate([zero_kt, ktb[HALF:]], axis=1)], axis=0)
        z2 = _dot(q_ref[0, pl.ds(r0, rows), :], kbd)
        vst = jnp.concatenate([v_ref[0, pl.ds(k0, BLK), :LANES],
                               v_ref[0, pl.ds(k0, BLK), LANES:]], axis=0)
        if masked:
            rowi, lane = _iotas(rows)
            kpos = k0 + lane
            msk = (kpos >= lo) & (kpos < r0 + rowi)
        stays, log_betas = [], []
        for h in range(2):
            z = z2[:, h * BLK:(h + 1) * BLK]
            neg_abs = lax.bitcast_convert_type(
                lax.bitcast_convert_type(z, jnp.int32) | SIGN_BIT, F32)
            stay = jnp.maximum(z, 0.0) + jnp.log(1.0 + jnp.exp(neg_abs))
            log_betas.append(z - stay)
            stays.append(jnp.where(msk, stay, 0.0) if masked else stay)
        cs = _dot(jnp.concatenate([s.astype(BF16) for s in stays], axis=1), u_ref[...])
        ws = []
        for h in range(2):
            carry = carry_ref[h, sl, :]
            w = jnp.exp(log_betas[h] - (cs[:, h * BLK:(h + 1) * BLK] + carry))
            carry_ref[h, sl, :] = carry + jnp.sum(stays[h], axis=1, keepdims=True)
            if masked:
                w = jnp.where(msk, w, 0.0)
            ws.append(w.astype(BF16))
        acc_ref[sl, :] += _dot(jnp.concatenate(ws, axis=1), vst)

    def reset(rows):
        carry_ref[:, :rows, :] = jnp.zeros((2, rows, BLK), F32)
        acc_ref[:rows, :] = jnp.zeros((rows, BLK), F32)

    reset(BLK)
    step(0, 0, BLK, 0, True)
    o_ref[0, :BLK, :] = _gated(acc_ref[:BLK, :], g_ref[0, :BLK, :])

    def tile(t, _):
        row0 = BLK + t * TQ
        jb0 = 1 + t * TB
        reset(TQ)
        for d in reversed(range(TB)):
            step(row0, d * BLK, TQ - d * BLK, jb0 + d, True)

        def body(i, c):
            for u in range(TB):
                step(row0, 0, TQ, jb0 - 1 - (i * TB + u), False)
            return c

        lax.fori_loop(0, t, body, 0)
        step(row0, 0, TQ, 0, True)
        o_ref[0, pl.ds(_al(row0), TQ), :] = _gated(acc_ref[...], g_ref[0, pl.ds(_al(row0), TQ), :])
        return 0

    lax.fori_loop(0, nt, tile, 0)


def _sb_attention(q, kt, v, gate):
    b, n, w = q.shape
    nt = (n - BLK) // TQ
    return pl.pallas_call(
        functools.partial(_sb_kernel, nt=nt),
        grid=(b, w // LANES),
        in_specs=[pl.BlockSpec((1, n, LANES), lambda i, j: (i, 0, j)),
                  pl.BlockSpec((1, LANES, n), lambda i, j: (i, j, 0)),
                  pl.BlockSpec((1, n, 2 * LANES), lambda i, j: (i, 0, j)),
                  pl.BlockSpec((1, n, LANES), lambda i, j: (i, 0, j))],
        out_specs=pl.BlockSpec((1, n, LANES), lambda i, j: (i, 0, j)),
        out_shape=jax.ShapeDtypeStruct((b, n, w), BF16),
        scratch_shapes=[pltpu.VMEM((2 * BLK, 2 * BLK), BF16),
                        pltpu.VMEM((2, TQ, BLK), F32),
                        pltpu.VMEM((TQ, BLK), F32)],
        compiler_params=pltpu.CompilerParams(
            dimension_semantics=("parallel", "parallel"), vmem_limit_bytes=VMEM_LIMIT),
        name="sb_attention",
    )(q, kt, v, gate)


def _softmax_kernel(q_ref, kt_ref, v_ref, g_ref, o_ref, s_ref, sd_ref, mx_ref, mb_ref, acc_ref, *, nt, mode):
    def visible(kpos, qpos, lo):
        if mode == "chunk":
            return (kpos >= lo) & (_chunk_of(kpos) <= _chunk_of(qpos))
        return (kpos >= lo) & (kpos <= qpos)

    def chunk_loop(n_chunks, chunk_fn):
        def pair(i, carry):
            chunk_fn(2 * i)
            chunk_fn(2 * i + 1)
            return carry

        lax.fori_loop(0, n_chunks >> 1, pair, 0)

        @pl.when((n_chunks & 1) == 1)
        def _():
            chunk_fn(n_chunks - 1)

    def emit(row0, rows, n_chunks, frames):
        frs = [slice(h * LANES, (h + 1) * LANES) for h in range(2)]
        r0 = _al(row0)
        rowi, lane = _iotas(rows)

        def chunk_scores(c, h):
            k0 = _al(BLK + c * TQ)
            return _dot(q_ref[0, pl.ds(r0, rows), frs[h]], kt_ref[0, frs[h], pl.ds(k0, TQ)])

        def fold_max(c, h, s):
            s_ref[h, :, pl.ds(_al(BLK + c * TQ), TQ)] = s
            m4 = jnp.maximum(jnp.maximum(s[:, :BLK], s[:, BLK:2 * BLK]),
                             jnp.maximum(s[:, 2 * BLK:3 * BLK], s[:, 3 * BLK:]))
            mx_ref[h] = jnp.maximum(mx_ref[h], m4)

        for h, fr in enumerate(frs):
            s0 = _dot(q_ref[0, pl.ds(r0, rows), fr], kt_ref[0, fr, :BLK])
            s0 = jnp.where(visible(lane, r0 + rowi, LEAD), s0, NEG)
            sd_ref[h, :rows, :] = s0
            mx_ref[h, :rows, :] = s0
        if frames:
            qpos = r0 + lax.broadcasted_iota(jnp.int32, (rows, TQ), 0)
            kpos = BLK + n_chunks * TQ + lax.broadcasted_iota(jnp.int32, (rows, TQ), 1)
            own = visible(kpos, qpos, 0)
            for h in range(2):
                fold_max(n_chunks, h, jnp.where(own, chunk_scores(n_chunks, h), NEG))

            def p1(c):
                for h in range(2):
                    fold_max(c, h, chunk_scores(c, h))

            chunk_loop(n_chunks, p1)

        for h, fr in enumerate(frs):
            m_row = jnp.max(mx_ref[h, :rows, :], axis=1, keepdims=True)
            mb_ref[h, :rows, :] = jnp.broadcast_to(m_row, (rows, BLK))
            p0 = jnp.exp(sd_ref[h, :rows, :] - mb_ref[h, :rows, :]).astype(BF16)
            acc_ref[h, :rows, :] = _dot(p0, v_ref[0, :BLK, fr])

        if frames:
            def p2(c):
                k0 = _al(BLK + c * TQ)
                for h, fr in enumerate(frs):
                    mb = mb_ref[h]
                    p = jnp.exp(s_ref[h, :, pl.ds(k0, TQ)]
                                - jnp.concatenate([mb] * TB, axis=1)).astype(BF16)
                    acc_ref[h] += _dot(p, v_ref[0, pl.ds(k0, TQ), fr])

            chunk_loop(n_chunks + 1, p2)

        outs = []
        for h in range(2):
            acc = acc_ref[h, :rows, :]
            outs.append(acc / pltpu.roll(acc, HALF, 1))
        o_ref[0, pl.ds(r0, rows), :] = _gated(jnp.where(lane < HALF, outs[0], outs[1]),
                                              g_ref[0, pl.ds(r0, rows), :])

    emit(0, BLK, 0, False)

    def tile(t, _):
        emit(BLK + t * TQ, TQ, t, True)
        return 0

    lax.fori_loop(0, nt, tile, 0)


def _softmax_attention(q, kt, v, gate, mode):
    b, n, w = q.shape
    nt = (n - BLK) // TQ
    pairs = w // (2 * LANES)
    return pl.pallas_call(
        functools.partial(_softmax_kernel, nt=nt, mode=mode),
        grid=(b, pairs),
        in_specs=[pl.BlockSpec((1, n, 2 * LANES), lambda i, j: (i, 0, j)),
                  pl.BlockSpec((1, 2 * LANES, n), lambda i, j: (i, j, 0)),
                  pl.BlockSpec((1, n, 2 * LANES), lambda i, j: (i, 0, j)),
                  pl.BlockSpec((1, n, LANES), lambda i, j: (i, 0, pairs + j))],
        out_specs=pl.BlockSpec((1, n, LANES), lambda i, j: (i, 0, j)),
        out_shape=jax.ShapeDtypeStruct((b, n, w // 2), BF16),
        scratch_shapes=[pltpu.VMEM((2, TQ, n), F32),
                        pltpu.VMEM((2, TQ, BLK), F32),
                        pltpu.VMEM((2, TQ, BLK), F32),
                        pltpu.VMEM((2, TQ, BLK), F32),
                        pltpu.VMEM((2, TQ, BLK), F32)],
        compiler_params=pltpu.CompilerParams(
            dimension_semantics=("parallel", "parallel"), vmem_limit_bytes=VMEM_LIMIT),
        name="softmax_attention_" + mode,
    )(q, kt, v, gate)


def _swa_kernel(sink_ref, q_ref, kt_ref, v_ref, g_ref, o_ref, *, nb):
    rowi, lane = _iotas(BLK)
    group = N_HEADS // SWA_KV_HEADS
    zero_kt = jnp.zeros((HALF, BLK), BF16)
    r2 = lax.broadcasted_iota(jnp.int32, (2 * BLK, BLK), 0)
    c2 = lax.broadcasted_iota(jnp.int32, (2 * BLK, BLK), 1)
    ones_st = jnp.where((r2 >= BLK) == (c2 >= HALF), 1.0, 0.0).astype(BF16)

    def q_block(m, _):
        r0 = pl.multiple_of(m * BLK, BLK)
        qpos = r0 + rowi
        cq = _chunk_of(qpos)
        slots = ((0, jnp.where(m >= 1, LEAD, NO_KEY)),
                 (jnp.maximum(m - 1, 0), jnp.where(m >= 2, 0, NO_KEY)),
                 (m, LEAD))
        biases = []
        for jb, lo in slots:
            kpos = jb * BLK + lane
            ck = _chunk_of(kpos)
            msk = (kpos >= lo) & (ck <= cq) & ((cq - ck <= WINDOW_CHUNKS) | (kpos < BLK))
            biases.append((msk, jnp.abs(qpos - kpos).astype(F32)))
        for g in range(SWA_KV_HEADS):
            kbds, wcats = [], []
            for jb, _ in slots:
                k0 = _al(jb * BLK)
                ktg =kt_ref[0, g * HALF:(g + 1) * HALF, pl.ds(k0, BLK)]
                kbds.append(jnp.concatenate([jnp.concatenate([ktg, zero_kt], axis=1),
                                             jnp.concatenate([zero_kt, ktg], axis=1)], axis=0))
                vst = jnp.concatenate([v_ref[0, pl.ds(k0, BLK), (2 * g) * LANES:(2 * g + 1) * LANES],
                                       v_ref[0, pl.ds(k0, BLK), (2 * g + 1) * LANES:(2 * g + 2) * LANES]],
                                      axis=0)
                wcats.append(jnp.concatenate([vst, ones_st], axis=1))
            for pr in range(group // 2):
                pair = g * (group // 2) + pr
                q2 = q_ref[0, pl.ds(r0, BLK), pair * LANES:(pair + 1) * LANES]
                zs = [_dot(q2, kbd) for kbd in kbds]
                ps, sink_terms = [[], []], []
                for h in range(2):
                    head = 2 * pair + h
                    slope = 2.0 ** (-8.0 * (head + 1) / N_HEADS)
                    ss = [jnp.where(msk, z[:, h * BLK:(h + 1) * BLK] - slope * dist, NEG)
                          for z, (msk, dist) in zip(zs, biases)]
                    sink = sink_ref[head]
                    mx = jnp.maximum(jnp.maximum(ss[0], ss[1]), ss[2])
                    m_row = jnp.maximum(jnp.max(mx, axis=1, keepdims=True), sink)
                    ps[h] = [jnp.exp(s - m_row).astype(BF16) for s in ss]
                    sink_terms.append(jnp.exp(sink - m_row))
                res = jnp.zeros((BLK, 2 * BLK), F32)
                for i in range(len(slots)):
                    res = res + _dot(jnp.concatenate([ps[0][i], ps[1][i]], axis=1), wcats[i])
                denom = res[:, BLK:] + jnp.where(lane < HALF, sink_terms[0], sink_terms[1])
                cols = slice(pair * LANES, (pair + 1) * LANES)
                o_ref[0, pl.ds(r0, BLK), cols] = _gated(res[:, :BLK] / denom, g_ref[0, pl.ds(r0, BLK), cols])
        return 0

    lax.fori_loop(0, nb, q_block, 0)


def _swa_attention(sinks, q, kt, v, gate):
    b, n, w = q.shape
    nb = n // BLK
    return pl.pallas_call(
        functools.partial(_swa_kernel, nb=nb),
        grid=(b,),
        in_specs=[
            pl.BlockSpec(memory_space=pltpu.SMEM),
            pl.BlockSpec((1, n, w), lambda i: (i, 0, 0)),
            pl.BlockSpec((1, kt.shape[1], n), lambda i: (i, 0, 0)),
            pl.BlockSpec((1, n, v.shape[2]), lambda i: (i, 0, 0)),
            pl.BlockSpec((1, n, w), lambda i: (i, 0, 0)),
        ],
        out_specs=pl.BlockSpec((1, n, w), lambda i: (i, 0, 0)),
        out_shape=jax.ShapeDtypeStruct((b, n, w), BF16),
        compiler_params=pltpu.CompilerParams(
            dimension_semantics=("parallel",), vmem_limit_bytes=VMEM_LIMIT),
        name="swa_attention",
    )(sinks, q, kt, v, gate)


def _spread_heads(x, fill):
    lane = lax.broadcasted_iota(jnp.int32, (x.shape[0], LANES), 1)
    blocks = []
    for p in range(x.shape[1] // LANES):
        xp = x[:, p * LANES:(p + 1) * LANES]
        for odd in range(2):
            blk = 2 * p + odd
            if isinstance(fill, float):
                f = fill
            elif callable(fill):
                f = fill(lane)
            else:
                f = fill[:, blk * LANES:(blk + 1) * LANES]
            blocks.append(jnp.where((lane >= HALF) if odd else (lane < HALF), xp, f))
    return jnp.concatenate(blocks, axis=1)


def _even_proj_kernel(x_ref, wsb_ref, wc_ref, wkr_ref, wg_ref, gq_ref, gkv_ref,
                      wa_ref, wb_ref, wkv_ref, pk_ref, cq_ref, sq_ref, ck_ref, sk_ref,
                      qs_ref, kst_ref, vs_ref, qm_ref, kmt_ref, vm_ref, gate_ref):
    xb = x_ref[0].astype(BF16)
    w = N_HEADS * HEAD_DIM
    sb = _dot(xb, wsb_ref[...])
    qs_ref[0] = sb[:, :w].astype(BF16)
    kst_ref[0] = sb[:, w:2 * w].T.astype(BF16)
    vs_ref[0] = _spread_heads(sb[:, 2 * w:], 0.0).astype(BF16)
    gate_ref[0] = _dot(xb, wg_ref[...])

    c = _dot(xb, wc_ref[...])

    def rms(t, g):
        return (t * lax.rsqrt(jnp.mean(t * t, axis=-1, keepdims=True) + RMS_EPS) * g).astype(BF16)

    xq = rms(c[:, :MLA_RANK], gq_ref[...])
    xkv = rms(c[:, MLA_RANK:], gkv_ref[...])
    qm = _dot(xq, wa_ref[...]) * cq_ref[...] + _dot(xq, wb_ref[...]) * sq_ref[...]
    qm_ref[0] = qm.astype(BF16)
    kr = _dot(xb, wkr_ref[...])
    krr = (kr[:, :LANES] * ck_ref[...] + kr[:, LANES:] * sk_ref[...]).astype(BF16)
    kv = _dot(xkv, wkv_ref[...])
    kr_placed = _dot(krr, pk_ref[...])
    lane = lax.broadcasted_iota(jnp.int32, (kv.shape[0], LANES), 1)
    kblocks, vblocks = [], []
    for hd in range(N_HEADS):
        blk = slice(hd * LANES, (hd + 1) * LANES)
        v_half = (lane >= HALF) if hd % 2 else (lane < HALF)
        kblocks.append(jnp.where(v_half, kr_placed[:, blk], kv[:, blk]))
        vblocks.append(jnp.where(v_half, kv[:, blk], 1.0))
    kmt_ref[0] = jnp.concatenate(kblocks, axis=1).T.astype(BF16)
    vm_ref[0] = jnp.concatenate(vblocks, axis=1).astype(BF16)


def _even_proj(h, wts, tabs, tm):
    b, n, d = h.shape
    grid = (n // tm, b)
    row = lambda i, j: (j, i, 0)
    col = lambda i, j: (j, 0, i)
    const = lambda i, j: (0, 0)
    tab = lambda i, j: (i, 0)
    full = lambda a: pl.BlockSpec(a.shape, const)
    wide = N_HEADS * LANES
    narrow = N_HEADS * HEAD_DIM
    outs = [(narrow, BF16, False), (narrow, BF16, True), (wide, BF16, False),
            (wide, BF16, False), (wide, BF16, True), (wide, BF16, False), (d, F32, False)]
    return pl.pallas_call(
        _even_proj_kernel,
        grid=grid,
        in_specs=[pl.BlockSpec((1, tm, d), row)] + [full(a) for a in wts] + [
            pl.BlockSpec((tm, wide), tab), pl.BlockSpec((tm, wide), tab),
            pl.BlockSpec((tm, LANES), tab), pl.BlockSpec((tm, LANES), tab)],
        out_specs=[pl.BlockSpec((1, c, tm), col) if tr else pl.BlockSpec((1, tm, c), row)
                   for c, _, tr in outs],
        out_shape=[jax.ShapeDtypeStruct((b, c, n) if tr else (b, n, c), t) for c, t, tr in outs],
        compiler_params=pltpu.CompilerParams(
            dimension_semantics=("parallel", "parallel"), vmem_limit_bytes=VMEM_LIMIT),
        name="even_proj",
    )(h, *wts, *tabs)


def _odd_proj_kernel(x_ref, wqc_ref, wkc_ref, wvc_ref, wqf_ref, wkf_ref, wvf_ref, wfz_ref, wg_ref,
                     bf_ref, pf_ref,
                     qc_ref, kct_ref, vc_ref, qf_ref, kft_ref, vf_ref, gate_ref, carry_ref, *, tm):
    i = pl.program_id(1)
    xb = x_ref[0].astype(BF16)
    qc_ref[0] = _dot(xb, wqc_ref[...]).astype(BF16)
    kct_ref[0] = _dot(xb, wkc_ref[...]).T.astype(BF16)
    vc_ref[0] = _dot(xb, wvc_ref[...]).astype(BF16)
    gate_ref[0] = _dot(xb, wg_ref[...])

    vf_ref[0] = _spread_heads(_dot(xb, wvf_ref[...]), 1.0).astype(BF16)
    qf_ref[0] = _spread_heads(_dot(xb, wqf_ref[...]),
                              lambda lane: jnp.where((lane & (HALF - 1)) < 3, 1.0, 0.0)).astype(BF16)

    fz = _dot(xb, wfz_ref[...]) + bf_ref[...]
    r = lax.broadcasted_iota(jnp.int32, (tm, LANES), 0) + i * tm
    c = lax.broadcasted_iota(jnp.int32, (tm, LANES), 1)
    lf = jnp.where((c < N_HEADS) & (r >= LEAD), _log_sigmoid(fz), 0.0)

    def split3(t):
        a = t.astype(BF16)
        rem = t - a.astype(F32)
        bb = rem.astype(BF16)
        cc = (rem - bb.astype(F32)).astype(BF16)
        return jnp.concatenate([a, bb, cc], axis=1)

    rr = lax.broadcasted_iota(jnp.int32, (tm, tm), 0)
    cc = lax.broadcasted_iota(jnp.int32, (tm, tm), 1)
    tri = jnp.where(cc <= rr, 1.0, 0.0).astype(BF16)

    @pl.when(i == 0)
    def _():
        carry_ref[...] = jnp.zeros_like(carry_ref)

    c3 = _dot(tri, split3(lf))
    cum = c3[:, :LANES] + c3[:, LANES:2 * LANES] + c3[:, 2 * LANES:] + carry_ref[0:1, :]
    carry_ref[...] = jnp.broadcast_to(cum[tm - 1:tm, :], carry_ref.shape)
    kft_ref[0] = _spread_heads(_dot(xb, wkf_ref[...]), _dot(split3(-cum), pf_ref[...])).T.astype(BF16)


def _odd_proj(h, wts, tm):
    b, n, d = h.shape
    row = lambda j, i: (j, i, 0)
    col = lambda j, i: (j, 0, i)
    const = lambda j, i: (0, 0)
    full = lambda a: pl.BlockSpec(a.shape, const)
    wide = N_HEADS * LANES
    narrow = N_HEADS * HEAD_DIM
    outs = [(narrow, BF16, False), (SWA_KV_HEADS * HEAD_DIM, BF16, True), (narrow, BF16, False),
            (wide, BF16, False), (wide, BF16, True), (wide, BF16, False), (d, F32, False)]
    return pl.pallas_call(
        functools.partial(_odd_proj_kernel, tm=tm),
        grid=(b, n // tm),
        in_specs=[pl.BlockSpec((1, tm, d), row)] + [full(a) for a in wts],
        out_specs=[pl.BlockSpec((1, c, tm), col) if tr else pl.BlockSpec((1, tm, c), row)
                   for c, _, tr in outs],
        out_shape=[jax.ShapeDtypeStruct((b, c, n) if tr else (b, n, c), t) for c, t, tr in outs],
        scratch_shapes=[pltpu.VMEM((8, LANES), F32)],
        compiler_params=pltpu.CompilerParams(
            dimension_semantics=("parallel", "arbitrary"), vmem_limit_bytes=VMEM_LIMIT),
        name="odd_proj",
    )(h, *wts)


def _out_kernel(ma_ref, mb_ref, h_ref, w_ref, g_ref, b_ref, o_ref):
    half = ma_ref.shape[-1]
    y = _dot(ma_ref[...], w_ref[:half, :]) + _dot(mb_ref[...], w_ref[half:, :])
    r = DN_ALPHA * h_ref[...] + y
    mu = jnp.mean(r, axis=-1, keepdims=True)
    cen = r - mu
    var = jnp.mean(cen * cen, axis=-1, keepdims=True)
    o_ref[...] = cen * lax.rsqrt(var + LN_EPS) * g_ref[...] + b_ref[...]


def _out_proj(ma, mb, h, w, g, bias, tm):
    rows, d = h.shape
    half = ma.shape[-1]
    row = lambda i: (i, 0)
    const = lambda i: (0, 0)
    return pl.pallas_call(
        _out_kernel,
        grid=(rows // tm,),
        in_specs=[pl.BlockSpec((tm, half), row), pl.BlockSpec((tm, half), row),
                  pl.BlockSpec((tm, d), row),
                  pl.BlockSpec(w.shape, const), pl.BlockSpec((1, d), const), pl.BlockSpec((1, d), const)],
        out_specs=pl.BlockSpec((tm, d), row),
        out_shape=jax.ShapeDtypeStruct((rows, d), F32),
        compiler_params=pltpu.CompilerParams(
            dimension_semantics=("parallel",), vmem_limit_bytes=VMEM_LIMIT),
        name="out_proj_ln",
    )(ma, mb, h, w, g, bias)


def _mla_head_blocks(nope, rot, odd):
    spare = jnp.zeros(rot.shape[:-1] + (HALF - MLA_ROPE,), rot.dtype)
    return jnp.concatenate([nope, rot, spare] if odd else [rot, spare, nope], axis=-1)


def _even_weights(w_in, g_cq, g_ckv, w_uq, w_ukv):
    a = N_HEADS * HEAD_DIM
    wq, wk, wv = w_in[:, :a], w_in[:, a:2 * a], w_in[:, 2 * a:3 * a]
    o = 3 * a
    w_c = w_in[:, o:o + 2 * MLA_RANK]
    o += 2 * MLA_RANK
    w_kr = w_in[:, o:o + MLA_ROPE]
    o += MLA_ROPE
    w_g = w_in[:, o:]
    w_sb = jnp.concatenate([wq * HEAD_DIM ** -0.5, wk, wv], axis=1)
    hr = MLA_ROPE // 2
    zpad = jnp.zeros((w_in.shape[0], LANES - MLA_ROPE), w_in.dtype)
    w_kr2 = jnp.concatenate([w_kr, zpad, w_kr[:, hr:], w_kr[:, :hr], zpad], axis=1)

    uq = w_uq.reshape(MLA_RANK, N_HEADS, MLA_NOPE + MLA_ROPE)
    ukv = w_ukv.reshape(MLA_RANK, N_HEADS, MLA_NOPE + HEAD_DIM)
    wa, wb, wkv = [], [], []
    for hd in range(N_HEADS):
        nope, r1, r2 = uq[:, hd, :MLA_NOPE], uq[:, hd, MLA_NOPE:MLA_NOPE + hr], uq[:, hd, MLA_NOPE + hr:]
        wa.append(_mla_head_blocks(nope, jnp.concatenate([r1, r2], axis=-1), hd % 2))
        wb.append(_mla_head_blocks(jnp.zeros_like(nope), jnp.concatenate([r2, r1], axis=-1), hd % 2))
        k_nope, v = ukv[:, hd, :MLA_NOPE], ukv[:, hd, MLA_NOPE:]
        wkv.append(jnp.concatenate([k_nope, v] if hd % 2 else [v, k_nope], axis=-1))
    p_k = np.zeros((LANES, N_HEADS * LANES), np.float32)
    for hd in range(N_HEADS):
        for t in range(MLA_ROPE):
            p_k[t, hd * LANES + (HALF if hd % 2 else 0) + t] = 1.0
    bf = lambda t: t.astype(BF16)
    return (bf(w_sb), bf(w_c), bf(w_kr2), bf(w_g), g_cq[None, :], g_ckv[None, :],
            bf(jnp.concatenate(wa, axis=1)), bf(jnp.concatenate(wb, axis=1)),
            bf(jnp.concatenate(wkv, axis=1)), jnp.asarray(p_k, BF16))


def _rope_tables(n):
    hr = MLA_ROPE // 2
    pos = jnp.arange(n, dtype=F32) - LEAD
    inv = ROPE_BASE ** (-jnp.arange(hr, dtype=F32) / hr)
    ang = pos[:, None] * inv[None, :]
    cos, sin = jnp.cos(ang), jnp.sin(ang)
    ones = jnp.ones((n, MLA_NOPE), F32)
    scale = (MLA_NOPE + MLA_ROPE) ** -0.5
    c_rot, s_rot = jnp.concatenate([cos, cos], axis=1), jnp.concatenate([-sin, sin], axis=1)
    c_q = jnp.concatenate([_mla_head_blocks(ones, c_rot, hd % 2) for hd in range(N_HEADS)], axis=1) * scale
    s_q = jnp.concatenate([_mla_head_blocks(0.0 * ones, s_rot, hd % 2) for hd in range(N_HEADS)],
                          axis=1) * scale
    z96 = jnp.zeros((n, LANES - MLA_ROPE), F32)
    c_k = jnp.concatenate([c_rot, z96], axis=1)
    s_k = jnp.concatenate([s_rot, z96], axis=1)
    return c_q, s_q, c_k, s_k


def _odd_weights(w_in, b_forget):
    a = N_HEADS * HEAD_DIM
    kvw = SWA_KV_HEADS * HEAD_DIM
    wqc = w_in[:, :a]
    o = a
    wkc = w_in[:, o:o + kvw]
    o += kvw
    wvc = w_in[:, o:o + kvw]
    o += kvw
    wqd, wkd, wvd = w_in[:, o:o + a], w_in[:, o + a:o + 2 * a], w_in[:, o + 2 * a:o + 3 * a]
    o += 3 * a
    wfz = w_in[:, o:o + N_HEADS]
    o += N_HEADS
    w_g = w_in[:, o:]
    scale = HEAD_DIM ** -0.5
    v0, v1 = wvc[:, :HEAD_DIM], wvc[:, HEAD_DIM:]
    z = jnp.zeros_like(v0)
    w_vc = jnp.concatenate([v0, z, z, v0, v1, z, z, v1], axis=1)
    w_fz = jnp.concatenate([wfz, jnp.zeros((w_in.shape[0], LANES - N_HEADS), w_in.dtype)], axis=1)
    b_f = jnp.concatenate([b_forget, jnp.zeros((LANES - N_HEADS,), b_forget.dtype)])[None, :]
    p_f = np.zeros((3 * LANES, N_HEADS * LANES), np.float32)
    for t in range(3):
        for h in range(N_HEADS):
            p_f[t * LANES + h, h * LANES + (0 if h % 2 else HALF) + t] = 1.0
    bf = lambda t: t.astype(BF16)
    return (bf(wqc * scale), bf(wkc), bf(w_vc), bf(wqd * scale), bf(wkd), bf(wvd), bf(w_fz), bf(w_g),
            b_f.astype(F32), jnp.asarray(p_f, BF16))


def _row_tile(nb, cap):
    g = max(t for t in range(1, cap + 1) if nb % t == 0)
    return g * BLK


def kernel(x, meta_tokens, w_in_even, g_cq, g_ckv, w_uq, w_ukv, w_out_even,
           w_in_odd, b_forget, sink_logits, w_out_odd, ln_gain, ln_bias):
    b, s, d = x.shape
    assert s % TQ == 0 and d == D_MODEL
    n = BLK + s
    nb = n // BLK
    h = jnp.concatenate([jnp.zeros((b, LEAD, d), x.dtype),
                         jnp.broadcast_to(meta_tokens[None].astype(x.dtype), (b, N_META, d)), x], axis=1)
    tabs = _rope_tables(n)
    tm_proj = _row_tile(nb, 3)
    tm_out = _row_tile(b * nb, 4)
    for layer in range(DEPTH):
        i = layer // 2
        if layer % 2 == 0:
            wts = _even_weights(w_in_even[i], g_cq[i], g_ckv[i], w_uq[i], w_ukv[i])
            qs, kst, vs, qm, kmt, vm, gate = _even_proj(h, wts, tabs, tm_proj)
            mix_a = _sb_attention(qs, kst, vs, gate)
            mix_b = _softmax_attention(qm, kmt, vm, gate, "chunk")
            w_out = w_out_even[i]
        else:
            wts = _odd_weights(w_in_odd[i], b_forget[i])
            qc, kct, vc, qf, kft, vf, gate = _odd_proj(h, wts, tm_proj)
            mix_a = _swa_attention(sink_logits[i].astype(F32), qc, kct, vc, gate)
            mix_b = _softmax_attention(qf, kft, vf, gate, "causal")
            w_out = w_out_odd[i]
        rows = b * n
        h = _out_proj(mix_a.reshape(rows, -1), mix_b.reshape(rows, -1),
                      h.reshape(rows, d), w_out.astype(BF16), ln_gain[layer][None, :],
                      ln_bias[layer][None, :], tm_out).reshape(b, n, d)
    return h[:, BLK:]
```

```python
import functools

import jax
import jax.numpy as jnp
import numpy as np
from jax import lax
from jax.experimental import pallas as pl
from jax.experimental.pallas import tpu as pltpu

D_MODEL = 1024
CHUNK = 64
N_META = 16
HEAD_DIM = 64
N_HEADS = 8
MLA_NOPE = 64
MLA_ROPE = 32
MLA_RANK = 256
SWA_KV_HEADS = 2
WINDOW_CHUNKS = 2
ROPE_BASE = 10000.0
DEPTH = 4
DN_ALPHA = (2 * DEPTH) ** 0.25
LN_EPS = 1e-5
RMS_EPS = 1e-6
NEG = -1e30
NO_KEY = 1 << 30
SIGN_BIT = -(1 << 31)

LANES = 128
BLK = 128
TB = 4
TQ = TB * BLK
SB_TB = 8
SB_TQ = SB_TB * BLK
SB_UNROLL = 4
LEAD = BLK - N_META
HALF = LANES // 2
VMEM_LIMIT = 56 * 1024 * 1024

F32 = jnp.float32
BF16 = jnp.bfloat16


def _dot(a, b):
    return jnp.dot(a, b, preferred_element_type=F32)


def _chunk_of(pos):
    return jnp.maximum((pos >> 6) - 1, 0)


def _log_sigmoid(x):
    return jnp.minimum(x, 0.0) - jnp.log(1.0 + jnp.exp(-jnp.abs(x)))


def _al(x):
    return x if isinstance(x, int) else pl.multiple_of(x, BLK)


def _gated(mixed, gate):
    return (mixed * (gate * jax.nn.sigmoid(gate))).astype(BF16)


def _iotas(rows):
    return (lax.broadcasted_iota(jnp.int32, (rows, BLK), 0),
            lax.broadcasted_iota(jnp.int32, (rows, BLK), 1))


def _sb_kernel(q_ref, kt_ref, v_ref, g_ref, o_ref, u_ref, carry_ref, acc_ref, *, nt):
    r2 = lax.broadcasted_iota(jnp.int32, (2 * BLK, 2 * BLK), 0)
    c2 = lax.broadcasted_iota(jnp.int32, (2 * BLK, 2 * BLK), 1)
    u_ref[...] = jnp.where(((r2 >= BLK) == (c2 >= BLK)) & ((r2 & (BLK - 1)) > (c2 & (BLK - 1))),
                           1.0, 0.0).astype(BF16)
    zero_kt = jnp.zeros((HALF, BLK), BF16)

    def step(row0, off, rows, jb, masked, lo=LEAD):
        r0 = _al(row0 + off)
        k0 = _al(jb * BLK)
        sl = slice(off, off + rows)
        ktb = kt_ref[0, :, pl.ds(k0, BLK)]
        kbd = jnp.concatenate([jnp.concatenate([ktb[:HALF], zero_kt], axis=1),
                               jnp.concatenate([zero_kt, ktb[HALF:]], axis=1)], axis=0)
        z2 = _dot(q_ref[0, pl.ds(r0, rows), :], kbd)
        vst = jnp.concatenate([v_ref[0, pl.ds(k0, BLK), :LANES],
                               v_ref[0, pl.ds(k0, BLK), LANES:]], axis=0)
        if masked:
            rowi, lane = _iotas(rows)
            kpos = k0 + lane
            msk = (kpos >= lo) & (kpos < r0 + rowi)
        stays, log_betas = [], []
        for h in range(2):
            z = z2[:, h * BLK:(h + 1) * BLK]
            neg_abs = lax.bitcast_convert_type(
                lax.bitcast_convert_type(z, jnp.int32) | SIGN_BIT, F32)
            stay = jnp.maximum(z, 0.0) + jnp.log(1.0 + jnp.exp(neg_abs))
            log_betas.append(z - stay)
            stays.append(jnp.where(msk, stay, 0.0) if masked else stay)
        cs = _dot(jnp.concatenate([s.astype(BF16) for s in stays], axis=1), u_ref[...])
        ws = []
        for h in range(2):
            carry = carry_ref[h, sl, :]
            w = jnp.exp(log_betas[h] - (cs[:, h * BLK:(h + 1) * BLK] + carry))
            carry_ref[h, sl, :] = carry + jnp.sum(stays[h], axis=1, keepdims=True)
            if masked:
                w = jnp.where(msk, w, 0.0)
            ws.append(w.astype(BF16))
        acc_ref[sl, :] += _dot(jnp.concatenate(ws, axis=1), vst)

    def reset(rows):
        carry_ref[:, :rows, :] = jnp.zeros((2, rows, BLK), F32)
        acc_ref[:rows, :] = jnp.zeros((rows, BLK), F32)

    reset(BLK)
    step(0, 0, BLK, 0, True)
    o_ref[0, :BLK, :] = _gated(acc_ref[:BLK, :], g_ref[0, :BLK, :])

    def tile(t, _):
        row0 = BLK + t * SB_TQ
        jb0 = 1 + t * SB_TB
        reset(SB_TQ)
        for d in reversed(range(SB_TB)):
            step(row0, d * BLK, SB_TQ - d * BLK, jb0 + d, True)

        def body(i, c):
            for u in range(SB_UNROLL):
                step(row0, 0, SB_TQ, jb0 - 1 - (i * SB_UNROLL + u), False)
            return c

        lax.fori_loop(0, t * (SB_TB // SB_UNROLL), body, 0)
        step(row0, 0, SB_TQ, 0, True)
        o_ref[0, pl.ds(_al(row0), SB_TQ), :] = _gated(acc_ref[...], g_ref[0, pl.ds(_al(row0), SB_TQ), :])
        return 0

    lax.fori_loop(0, nt, tile, 0)


def _sb_attention(q, kt, v, gate):
    b, n, w = q.shape
    nt = (n - BLK) // SB_TQ
    return pl.pallas_call(
        functools.partial(_sb_kernel, nt=nt),
        grid=(b, w // LANES),
        in_specs=[pl.BlockSpec((1, n, LANES), lambda i, j: (i, 0, j)),
                  pl.BlockSpec((1, LANES, n), lambda i, j: (i, j, 0)),
                  pl.BlockSpec((1, n, 2 * LANES), lambda i, j: (i, 0, j)),
                  pl.BlockSpec((1, n, LANES), lambda i, j: (i, 0, j))],
        out_specs=pl.BlockSpec((1, n, LANES), lambda i, j: (i, 0, j)),
        out_shape=jax.ShapeDtypeStruct((b, n, w), BF16),
        scratch_shapes=[pltpu.VMEM((2 * BLK, 2 * BLK), BF16),
                        pltpu.VMEM((2, SB_TQ, BLK), F32),
                        pltpu.VMEM((SB_TQ, BLK), F32)],
        compiler_params=pltpu.CompilerParams(
            dimension_semantics=("parallel", "parallel"), vmem_limit_bytes=VMEM_LIMIT),
        name="sb_attention",
    )(q, kt, v, gate)


def _softmax_kernel(q_ref, kt_ref, v_ref, g_ref, o_ref, s_ref, sd_ref, mx_ref, mb_ref, acc_ref, *, nt, mode):
    def visible(kpos, qpos, lo):
        if mode == "chunk":
            return (kpos >= lo) & (_chunk_of(kpos) <= _chunk_of(qpos))
        return (kpos >= lo) & (kpos <= qpos)

    def chunk_loop(n_chunks, chunk_fn):
        def pair(i, carry):
            chunk_fn(2 * i)
            chunk_fn(2 * i + 1)
            return carry

        lax.fori_loop(0, n_chunks >> 1, pair, 0)

        @pl.when((n_chunks & 1) == 1)
        def _():
            chunk_fn(n_chunks - 1)

    def emit(row0, rows, n_chunks, frames):
        frs = [slice(h * LANES, (h + 1) * LANES) for h in range(2)]
        r0 = _al(row0)
        rowi, lane = _iotas(rows)

        def chunk_scores(c, h):
            k0 = _al(BLK + c * TQ)
            return _dot(q_ref[0, pl.ds(r0, rows), frs[h]], kt_ref[0, frs[h], pl.ds(k0, TQ)])

        def fold_max(c, h, s):
            s_ref[h, :, pl.ds(_al(BLK + c * TQ), TQ)] = s
            m4 = jnp.maximum(jnp.maximum(s[:, :BLK], s[:, BLK:2 * BLK]),
                             jnp.maximum(s[:, 2 * BLK:3 * BLK], s[:, 3 * BLK:]))
            mx_ref[h] = jnp.maximum(mx_ref[h], m4)

        for h, fr in enumerate(frs):
            s0 = _dot(q_ref[0, pl.ds(r0, rows), fr], kt_ref[0, fr, :BLK])
            s0 = jnp.where(visible(lane, r0 + rowi, LEAD), s0, NEG)
            sd_ref[h, :rows, :] = s0
            mx_ref[h, :rows, :] = s0
        if frames:
            qpos = r0 + lax.broadcasted_iota(jnp.int32, (rows, TQ), 0)
            kpos = BLK + n_chunks * TQ + lax.broadcasted_iota(jnp.int32, (rows, TQ), 1)
            own = visible(kpos, qpos, 0)
            for h in range(2):
                fold_max(n_chunks, h, jnp.where(own, chunk_scores(n_chunks, h), NEG))

            def p1(c):
                for h in range(2):
                    fold_max(c, h, chunk_scores(c, h))

            chunk_loop(n_chunks, p1)

        for h, fr in enumerate(frs):
            m_row = jnp.max(mx_ref[h, :rows, :], axis=1, keepdims=True)
            mb_ref[h, :rows, :] = jnp.broadcast_to(m_row, (rows, BLK))
            p0 = jnp.exp(sd_ref[h, :rows, :] - mb_ref[h, :rows, :]).astype(BF16)
            acc_ref[h, :rows, :] = _dot(p0, v_ref[0, :BLK, fr])

        if frames:
            def p2(c):
                k0 = _al(BLK + c * TQ)
                for h, fr in enumerate(frs):
                    mb = mb_ref[h]
                    p = jnp.exp(s_ref[h, :, pl.ds(k0, TQ)]
                                - jnp.concatenate([mb] * TB, axis=1)).astype(BF16)
                    acc_ref[h] += _dot(p, v_ref[0, pl.ds(k0, TQ), fr])

            chunk_loop(n_chunks + 1, p2)

        outs = []
        for h in range(2):
            acc = acc_ref[h, :rows, :]
            outs.append(acc / pltpu.roll(acc, HALF, 1))
        o_ref[0, pl.ds(r0, rows), :] = _gated(jnp.where(lane < HALF, outs[0], outs[1]),
                                              g_ref[0, pl.ds(r0, rows), :])

    emit(0, BLK, 0, False)

    def tile(t, _):
        emit(BLK + t * TQ, TQ, t, True)
        return 0

    lax.fori_loop(0, nt, tile, 0)


def _softmax_attention(q, kt, v, gate, mode):
    b, n, w = q.shape
    nt = (n - BLK) // TQ
    pairs = w // (2 * LANES)
    return pl.pallas_call(
        functools.partial(_softmax_kernel, nt=nt, mode=mode),
        grid=(b, pairs),
        in_specs=[pl.BlockSpec((1, n, 2 * LANES), lambda i, j: (i, 0, j)),
                  pl.BlockSpec((1, 2 * LANES, n), lambda i, j: (i, j, 0)),
                  pl.BlockSpec((1, n, 2 * LANES), lambda i, j: (i, 0, j)),
                  pl.BlockSpec((1, n, LANES), lambda i, j: (i, 0, pairs + j))],
        out_specs=pl.BlockSpec((1, n, LANES), lambda i, j: (i, 0, j)),
        out_shape=jax.ShapeDtypeStruct((b, n, w // 2), BF16),
        scratch_shapes=[pltpu.VMEM((2, TQ, n), F32),
                        pltpu.VMEM((2, TQ, BLK), F32),
                        pltpu.VMEM((2, TQ, BLK), F32),
                        pltpu.VMEM((2, TQ, BLK), F32),
                        pltpu.VMEM((2, TQ, BLK), F32)],
        compiler_params=pltpu.CompilerParams(
            dimension_semantics=("parallel", "parallel"), vmem_limit_bytes=VMEM_LIMIT),
        name="softmax_attention_" + mode,
    )(q, kt, v, gate)


def _swa_kernel(sink_ref, q_ref, kt_ref, v_ref, g_ref, o_ref, *, nb):
    rowi, lane = _iotas(BLK)
    group = N_HEADS // SWA_KV_HEADS
    zero_kt = jnp.zeros((HALF, BLK), BF16)
    r2 = lax.broadcasted_iota(jnp.int32, (2 * BLK, BLK), 0)
    c2 = lax.broadcasted_iota(jnp.int32, (2 * BLK, BLK), 1)
    ones_st = jnp.where((r2 >= BLK) == (c2 >= HALF), 1.0, 0.0).astype(BF16)

    def q_block(m):
        r0 = _al(m * BLK)
        qpos = r0 + rowi
        cq = _chunk_of(qpos)
        slots = ((0, jnp.where(m >= 1, LEAD, NO_KEY)),
                 (jnp.maximum(m - 1, 0), jnp.where(m >= 2, 0, NO_KEY)),
                 (m, LEAD))
        biases = []
        for jb, lo in slots:
            kpos = jb * BLK + lane
            ck = _chunk_of(kpos)
            msk = (kpos >= lo) & (ck <= cq) & ((cq - ck <= WINDOW_CHUNKS) | (kpos < BLK))
            biases.append((msk, jnp.abs(qpos - kpos).astype(F32)))
        for g in range(SWA_KV_HEADS):
            kbds, wcats = [], []
            for jb, _ in slots:
                k0 = _al(jb * BLK)
                ktg =kt_ref[0, g * HALF:(g + 1) * HALF, pl.ds(k0, BLK)]
                kbds.append(jnp.concatenate([jnp.concatenate([ktg, zero_kt], axis=1),
                                             jnp.concatenate([zero_kt, ktg], axis=1)], axis=0))
                vst = jnp.concatenate([v_ref[0, pl.ds(k0, BLK), (2 * g) * LANES:(2 * g + 1) * LANES],
                                       v_ref[0, pl.ds(k0, BLK), (2 * g + 1) * LANES:(2 * g + 2) * LANES]],
                                      axis=0)
                wcats.append(jnp.concatenate([vst, ones_st], axis=1))
            for pr in range(group // 2):
                pair = g * (group // 2) + pr
                q2 = q_ref[0, pl.ds(r0, BLK), pair * LANES:(pair + 1) * LANES]
                zs = [_dot(q2, kbd) for kbd in kbds]
                ps, sink_terms = [[], []], []
                for h in range(2):
                    head = 2 * pair + h
                    slope = 2.0 ** (-8.0 * (head + 1) / N_HEADS)
                    ss = [jnp.where(msk, z[:, h * BLK:(h + 1) * BLK] - slope * dist, NEG)
                          for z, (msk, dist) in zip(zs, biases)]
                    sink = sink_ref[head]
                    mx = jnp.maximum(jnp.maximum(ss[0], ss[1]), ss[2])
                    m_row = jnp.maximum(jnp.max(mx, axis=1, keepdims=True), sink)
                    ps[h] = [jnp.exp(s - m_row).astype(BF16) for s in ss]
                    sink_terms.append(jnp.exp(sink - m_row))
                res = jnp.zeros((BLK, 2 * BLK), F32)
                for i in range(len(slots)):
                    res = res + _dot(jnp.concatenate([ps[0][i], ps[1][i]], axis=1), wcats[i])
                denom = res[:, BLK:] + jnp.where(lane < HALF, sink_terms[0], sink_terms[1])
                cols = slice(pair * LANES, (pair + 1) * LANES)
                o_ref[0, pl.ds(r0, BLK), cols] = _gated(res[:, :BLK] / denom, g_ref[0, pl.ds(r0, BLK), cols])

    def two_blocks(i, carry):
        q_block(2 * i)
        q_block(2 * i + 1)
        return carry

    lax.fori_loop(0, nb // 2, two_blocks, 0)
    if nb % 2:
        q_block(nb - 1)


def _swa_attention(sinks, q, kt, v, gate):
    b, n, w = q.shape
    nb = n // BLK
    return pl.pallas_call(
        functools.partial(_swa_kernel, nb=nb),
        grid=(b,),
        in_specs=[
            pl.BlockSpec(memory_space=pltpu.SMEM),
            pl.BlockSpec((1, n, w), lambda i: (i, 0, 0)),
            pl.BlockSpec((1, kt.shape[1], n), lambda i: (i, 0, 0)),
            pl.BlockSpec((1, n, v.shape[2]), lambda i: (i, 0, 0)),
            pl.BlockSpec((1, n, w), lambda i: (i, 0, 0)),
        ],
        out_specs=pl.BlockSpec((1, n, w), lambda i: (i, 0, 0)),
        out_shape=jax.ShapeDtypeStruct((b, n, w), BF16),
        compiler_params=pltpu.CompilerParams(
            dimension_semantics=("parallel",), vmem_limit_bytes=VMEM_LIMIT),
        name="swa_attention",
    )(sinks, q, kt, v, gate)


def _spread_heads(x, fill):
    lane = lax.broadcasted_iota(jnp.int32, (x.shape[0], LANES), 1)
    blocks = []
    for p in range(x.shape[1] // LANES):
        xp = x[:, p * LANES:(p + 1) * LANES]
        for odd in range(2):
            blk = 2 * p + odd
            if isinstance(fill, float):
                f = fill
            elif callable(fill):
                f = fill(lane)
            else:
                f = fill[:, blk * LANES:(blk + 1) * LANES]
            blocks.append(jnp.where((lane >= HALF) if odd else (lane < HALF), xp, f))
    return jnp.concatenate(blocks, axis=1)


def _even_proj_kernel(x_ref, wsb_ref, wc_ref, wkr_ref, wg_ref, gq_ref, gkv_ref,
                      wa_ref, wb_ref, wkv_ref, pk_ref, cq_ref, sq_ref, ck_ref, sk_ref,
                      qs_ref, kst_ref, vs_ref, qm_ref, kmt_ref, vm_ref, gate_ref):
    xb = x_ref[0].astype(BF16)
    w = N_HEADS * HEAD_DIM
    sb = _dot(xb, wsb_ref[...])
    qs_ref[0] = sb[:, :w].astype(BF16)
    kst_ref[0] = sb[:, w:2 * w].T.astype(BF16)
    vs_ref[0] = _spread_heads(sb[:, 2 * w:], 0.0).astype(BF16)
    gate_ref[0] = _dot(xb, wg_ref[...])

    c = _dot(xb, wc_ref[...])

    def rms(t, g):
        return (t * lax.rsqrt(jnp.mean(t * t, axis=-1, keepdims=True) + RMS_EPS) * g).astype(BF16)

    xq = rms(c[:, :MLA_RANK], gq_ref[...])
    xkv = rms(c[:, MLA_RANK:], gkv_ref[...])
    qm = _dot(xq, wa_ref[...]) * cq_ref[...] + _dot(xq, wb_ref[...]) * sq_ref[...]
    qm_ref[0] = qm.astype(BF16)
    kr = _dot(xb, wkr_ref[...])
    krr = (kr[:, :LANES] * ck_ref[...] + kr[:, LANES:] * sk_ref[...]).astype(BF16)
    kv = _dot(xkv, wkv_ref[...])
    kr_placed = _dot(krr, pk_ref[...])
    lane = lax.broadcasted_iota(jnp.int32, (kv.shape[0], LANES), 1)
    kblocks, vblocks = [], []
    for hd in range(N_HEADS):
        blk = slice(hd * LANES, (hd + 1) * LANES)
        v_half = (lane >= HALF) if hd % 2 else (lane < HALF)
        kblocks.append(jnp.where(v_half, kr_placed[:, blk], kv[:, blk]))
        vblocks.append(jnp.where(v_half, kv[:, blk], 1.0))
    kmt_ref[0] = jnp.concatenate(kblocks, axis=1).T.astype(BF16)
    vm_ref[0] = jnp.concatenate(vblocks, axis=1).astype(BF16)


def _even_proj(h, wts, tabs, tm):
    b, n, d = h.shape
    grid = (n // tm, b)
    row = lambda i, j: (j, i, 0)
    col = lambda i, j: (j, 0, i)
    const = lambda i, j: (0, 0)
    tab = lambda i, j: (i, 0)
    full = lambda a: pl.BlockSpec(a.shape, const)
    wide = N_HEADS * LANES
    narrow = N_HEADS * HEAD_DIM
    outs = [(narrow, BF16, False), (narrow, BF16, True), (wide, BF16, False),
            (wide, BF16, False), (wide, BF16, True), (wide, BF16, False), (d, F32, False)]
    return pl.pallas_call(
        _even_proj_kernel,
        grid=grid,
        in_specs=[pl.BlockSpec((1, tm, d), row)] + [full(a) for a in wts] + [
            pl.BlockSpec((tm, wide), tab), pl.BlockSpec((tm, wide), tab),
            pl.BlockSpec((tm, LANES), tab), pl.BlockSpec((tm, LANES), tab)],
        out_specs=[pl.BlockSpec((1, c, tm), col) if tr else pl.BlockSpec((1, tm, c), row)
                   for c, _, tr in outs],
        out_shape=[jax.ShapeDtypeStruct((b, c, n) if tr else (b, n, c), t) for c, t, tr in outs],
        compiler_params=pltpu.CompilerParams(
            dimension_semantics=("parallel", "parallel"), vmem_limit_bytes=VMEM_LIMIT),
        name="even_proj",
    )(h, *wts, *tabs)


def _odd_proj_kernel(x_ref, wqc_ref, wkc_ref, wvc_ref, wqf_ref, wkf_ref, wvf_ref, wfz_ref, wg_ref,
                     bf_ref, pf_ref,
                     qc_ref, kct_ref, vc_ref, qf_ref, kft_ref, vf_ref, gate_ref, carry_ref, *, tm):
    i = pl.program_id(1)
    xb = x_ref[0].astype(BF16)
    qc_ref[0] = _dot(xb, wqc_ref[...]).astype(BF16)
    kct_ref[0] = _dot(xb, wkc_ref[...]).T.astype(BF16)
    vc_ref[0] = _dot(xb, wvc_ref[...]).astype(BF16)
    gate_ref[0] = _dot(xb, wg_ref[...])

    vf_ref[0] = _spread_heads(_dot(xb, wvf_ref[...]), 1.0).astype(BF16)
    qf_ref[0] = _spread_heads(_dot(xb, wqf_ref[...]),
                              lambda lane: jnp.where((lane & (HALF - 1)) < 3, 1.0, 0.0)).astype(BF16)

    fz = _dot(xb, wfz_ref[...]) + bf_ref[...]
    r = lax.broadcasted_iota(jnp.int32, (tm, LANES), 0) + i * tm
    c = lax.broadcasted_iota(jnp.int32, (tm, LANES), 1)
    lf = jnp.where((c < N_HEADS) & (r >= LEAD), _log_sigmoid(fz), 0.0)

    def split3(t):
        a = t.astype(BF16)
        rem = t - a.astype(F32)
        bb = rem.astype(BF16)
        cc = (rem - bb.astype(F32)).astype(BF16)
        return jnp.concatenate([a, bb, cc], axis=1)

    rr = lax.broadcasted_iota(jnp.int32, (tm, tm), 0)
    cc = lax.broadcasted_iota(jnp.int32, (tm, tm), 1)
    tri = jnp.where(cc <= rr, 1.0, 0.0).astype(BF16)

    @pl.when(i == 0)
    def _():
        carry_ref[...] = jnp.zeros_like(carry_ref)

    c3 = _dot(tri, split3(lf))
    cum = c3[:, :LANES] + c3[:, LANES:2 * LANES] + c3[:, 2 * LANES:] + carry_ref[0:1, :]
    carry_ref[...] = jnp.broadcast_to(cum[tm - 1:tm, :], carry_ref.shape)
    kft_ref[0] = _spread_heads(_dot(xb, wkf_ref[...]), _dot(split3(-cum), pf_ref[...])).T.astype(BF16)


def _odd_proj(h, wts, tm):
    b, n, d = h.shape
    row = lambda j, i: (j, i, 0)
    col = lambda j, i: (j, 0, i)
    const = lambda j, i: (0, 0)
    full = lambda a: pl.BlockSpec(a.shape, const)
    wide = N_HEADS * LANES
    narrow = N_HEADS * HEAD_DIM
    outs = [(narrow, BF16, False), (SWA_KV_HEADS * HEAD_DIM, BF16, True), (narrow, BF16, False),
            (wide, BF16, False), (wide, BF16, True), (wide, BF16, False), (d, F32, False)]
    return pl.pallas_call(
        functools.partial(_odd_proj_kernel, tm=tm),
        grid=(b, n // tm),
        in_specs=[pl.BlockSpec((1, tm, d), row)] + [full(a) for a in wts],
        out_specs=[pl.BlockSpec((1, c, tm), col) if tr else pl.BlockSpec((1, tm, c), row)
                   for c, _, tr in outs],
        out_shape=[jax.ShapeDtypeStruct((b, c, n) if tr else (b, n, c), t) for c, t, tr in outs],
        scratch_shapes=[pltpu.VMEM((8, LANES), F32)],
        compiler_params=pltpu.CompilerParams(
            dimension_semantics=("parallel", "arbitrary"), vmem_limit_bytes=VMEM_LIMIT),
        name="odd_proj",
    )(h, *wts)


def _out_kernel(ma_ref, mb_ref, h_ref, w_ref, g_ref, b_ref, o_ref):
    half = ma_ref.shape[-1]
    y = _dot(ma_ref[...], w_ref[:half, :]) + _dot(mb_ref[...], w_ref[half:, :])
    r = DN_ALPHA * h_ref[...] + y
    mu = jnp.mean(r, axis=-1, keepdims=True)
    cen = r - mu
    var = jnp.mean(cen * cen, axis=-1, keepdims=True)
    o_ref[...] = cen * lax.rsqrt(var + LN_EPS) * g_ref[...] + b_ref[...]


def _out_proj(ma, mb, h, w, g, bias, tm):
    rows, d = h.shape
    half = ma.shape[-1]
    row = lambda i: (i, 0)
    const = lambda i: (0, 0)
    return pl.pallas_call(
        _out_kernel,
        grid=(rows // tm,),
        in_specs=[pl.BlockSpec((tm, half), row), pl.BlockSpec((tm, half), row),
                  pl.BlockSpec((tm, d), row),
                  pl.BlockSpec(w.shape, const), pl.BlockSpec((1, d), const), pl.BlockSpec((1, d), const)],
        out_specs=pl.BlockSpec((tm, d), row),
        out_shape=jax.ShapeDtypeStruct((rows, d), F32),
        compiler_params=pltpu.CompilerParams(
            dimension_semantics=("parallel",), vmem_limit_bytes=VMEM_LIMIT),
        name="out_proj_ln",
    )(ma, mb, h, w, g, bias)


def _by_parity(even, odd):
    is_odd = (jnp.arange(N_HEADS) % 2 == 1)[:, None]
    return jnp.where(is_odd, odd, even).reshape(even.shape[:-2] + (N_HEADS * LANES,))


def _mla_head_blocks(nope, rot):
    spare = jnp.zeros(rot.shape[:-1] + (HALF - MLA_ROPE,), rot.dtype)
    return _by_parity(jnp.concatenate([rot, spare, nope], axis=-1),
                      jnp.concatenate([nope, rot, spare], axis=-1))


def _even_weights(w_in, g_cq, g_ckv, w_uq, w_ukv):
    a = N_HEADS * HEAD_DIM
    wq, wk, wv = w_in[:, :a], w_in[:, a:2 * a], w_in[:, 2 * a:3 * a]
    o = 3 * a
    w_c = w_in[:, o:o + 2 * MLA_RANK]
    o += 2 * MLA_RANK
    w_kr = w_in[:, o:o + MLA_ROPE]
    o += MLA_ROPE
    w_g = w_in[:, o:]
    w_sb = jnp.concatenate([wq * HEAD_DIM ** -0.5, wk, wv], axis=1)
    hr = MLA_ROPE // 2
    zpad = jnp.zeros((w_in.shape[0], LANES - MLA_ROPE), w_in.dtype)
    w_kr2 = jnp.concatenate([w_kr, zpad, w_kr[:, hr:], w_kr[:, :hr], zpad], axis=1)

    uq = w_uq.reshape(MLA_RANK, N_HEADS, MLA_NOPE + MLA_ROPE)
    ukv = w_ukv.reshape(MLA_RANK, N_HEADS, MLA_NOPE + HEAD_DIM)
    nope, r1, r2 = uq[..., :MLA_NOPE], uq[..., MLA_NOPE:MLA_NOPE + hr], uq[..., MLA_NOPE + hr:]
    w_a = _mla_head_blocks(nope, jnp.concatenate([r1, r2], axis=-1))
    w_b = _mla_head_blocks(jnp.zeros_like(nope), jnp.concatenate([r2, r1], axis=-1))
    k_nope, v = ukv[..., :MLA_NOPE], ukv[..., MLA_NOPE:]
    w_kv = _by_parity(jnp.concatenate([v, k_nope], axis=-1), ukv)
    p_k = np.zeros((LANES, N_HEADS * LANES), np.float32)
    for hd in range(N_HEADS):
        for t in range(MLA_ROPE):
            p_k[t, hd * LANES + (HALF if hd % 2 else 0) + t] = 1.0
    bf = lambda t: t.astype(BF16)
    return (bf(w_sb), bf(w_c), bf(w_kr2), bf(w_g), g_cq[None, :], g_ckv[None, :],
            bf(w_a), bf(w_b), bf(w_kv), jnp.asarray(p_k, BF16))


def _rope_tables(n):
    hr = MLA_ROPE // 2
    pos = jnp.arange(n, dtype=F32) - LEAD
    inv = ROPE_BASE ** (-jnp.arange(hr, dtype=F32) / hr)
    ang = pos[:, None] * inv[None, :]
    cos, sin = jnp.cos(ang), jnp.sin(ang)
    ones = jnp.ones((n, MLA_NOPE), F32)
    scale = (MLA_NOPE + MLA_ROPE) ** -0.5
    c_rot, s_rot = jnp.concatenate([cos, cos], axis=1), jnp.concatenate([-sin, sin], axis=1)
    per_head = lambda t: jnp.broadcast_to(t[:, None, :], (n, N_HEADS, t.shape[-1]))
    c_q = _mla_head_blocks(per_head(ones), per_head(c_rot)) * scale
    s_q = _mla_head_blocks(per_head(0.0 * ones), per_head(s_rot)) * scale
    z96 = jnp.zeros((n, LANES - MLA_ROPE), F32)
    c_k = jnp.concatenate([c_rot, z96], axis=1)
    s_k = jnp.concatenate([s_rot, z96], axis=1)
    return c_q, s_q, c_k, s_k


def _odd_weights(w_in, b_forget):
    a = N_HEADS * HEAD_DIM
    kvw = SWA_KV_HEADS * HEAD_DIM
    wqc = w_in[:, :a]
    o = a
    wkc = w_in[:, o:o + kvw]
    o += kvw
    wvc = w_in[:, o:o + kvw]
    o += kvw
    wqd, wkd, wvd = w_in[:, o:o + a], w_in[:, o + a:o + 2 * a], w_in[:, o + 2 * a:o + 3 * a]
    o += 3 * a
    wfz = w_in[:, o:o + N_HEADS]
    o += N_HEADS
    w_g = w_in[:, o:]
    scale = HEAD_DIM ** -0.5
    v0, v1 = wvc[:, :HEAD_DIM], wvc[:, HEAD_DIM:]
    z = jnp.zeros_like(v0)
    w_vc = jnp.concatenate([v0, z, z, v0, v1, z, z, v1], axis=1)
    w_fz = jnp.concatenate([wfz, jnp.zeros((w_in.shape[0], LANES - N_HEADS), w_in.dtype)], axis=1)
    b_f = jnp.concatenate([b_forget, jnp.zeros((LANES - N_HEADS,), b_forget.dtype)])[None, :]
    p_f = np.zeros((3 * LANES, N_HEADS * LANES), np.float32)
    for t in range(3):
        for h in range(N_HEADS):
            p_f[t * LANES + h, h * LANES + (0 if h % 2 else HALF) + t] = 1.0
    bf = lambda t: t.astype(BF16)
    return (bf(wqc * scale), bf(wkc), bf(w_vc), bf(wqd * scale), bf(wkd), bf(wvd), bf(w_fz), bf(w_g),
            b_f.astype(F32), jnp.asarray(p_f, BF16))


def _row_tile(nb, cap):
    g = max(t for t in range(1, cap + 1) if nb % t == 0)
    return g * BLK


def kernel(x, meta_tokens, w_in_even, g_cq, g_ckv, w_uq, w_ukv, w_out_even,
           w_in_odd, b_forget, sink_logits, w_out_odd, ln_gain, ln_bias):
    b, s, d = x.shape
    assert s % SB_TQ == 0 and s % TQ == 0 and d == D_MODEL
    n = BLK + s
    nb = n // BLK
    h = jnp.concatenate([jnp.zeros((b, LEAD, d), x.dtype),
                         jnp.broadcast_to(meta_tokens[None].astype(x.dtype), (b, N_META, d)), x], axis=1)
    tabs = _rope_tables(n)
    tm_proj = _row_tile(nb, 3)
    tm_out = _row_tile(b * nb, 4)
    for layer in range(DEPTH):
        i = layer // 2
        if layer % 2 == 0:
            wts = _even_weights(w_in_even[i], g_cq[i], g_ckv[i], w_uq[i], w_ukv[i])
            qs, kst, vs, qm, kmt, vm, gate = _even_proj(h, wts, tabs, tm_proj)
            mix_a = _sb_attention(qs, kst, vs, gate)
            mix_b = _softmax_attention(qm, kmt, vm, gate, "chunk")
            w_out = w_out_even[i]
        else:
            wts = _odd_weights(w_in_odd[i], b_forget[i])
            qc, kct, vc, qf, kft, vf, gate = _odd_proj(h, wts, tm_proj)
            mix_a = _swa_attention(sink_logits[i].astype(F32), qc, kct, vc, gate)
            mix_b = _softmax_attention(qf, kft, vf, gate, "causal")
            w_out = w_out_odd[i]
        rows = b * n
        h = _out_proj(mix_a.reshape(rows, -1), mix_b.reshape(rows, -1),
                      h.reshape(rows, d), w_out.astype(BF16), ln_gain[layer][None, :],
                      ln_bias[layer][None, :], tm_out).reshape(b, n, d)
    return h[:, BLK:]
```

```python
import functools

import jax
import jax.numpy as jnp
import numpy as np
from jax import lax
from jax.experimental import pallas as pl
from jax.experimental.pallas import tpu as pltpu

D_MODEL = 1024
CHUNK = 64
N_META = 16
HEAD_DIM = 64
N_HEADS = 8
MLA_NOPE = 64
MLA_ROPE = 32
MLA_RANK = 256
SWA_KV_HEADS = 2
WINDOW_CHUNKS = 2
ROPE_BASE = 10000.0
DEPTH = 4
DN_ALPHA = (2 * DEPTH) ** 0.25
LN_EPS = 1e-5
RMS_EPS = 1e-6
NEG = -1e30
NO_KEY = 1 << 30
SIGN_BIT = -(1 << 31)

LANES = 128
BLK = 128
TB = 4
TQ = TB * BLK
SB_TB = 8
SB_TQ = SB_TB * BLK
SB_UNROLL = 4
LEAD = BLK - N_META
HALF = LANES // 2
VMEM_LIMIT = 56 * 1024 * 1024

F32 = jnp.float32
BF16 = jnp.bfloat16


def _dot(a, b):
    return jnp.dot(a, b, preferred_element_type=F32)


def _chunk_of(pos):
    return jnp.maximum((pos >> 6) - 1, 0)


def _log_sigmoid(x):
    return jnp.minimum(x, 0.0) - jnp.log(1.0 + jnp.exp(-jnp.abs(x)))


def _al(x):
    return x if isinstance(x, int) else pl.multiple_of(x, BLK)


def _gated(mixed, gate):
    return (mixed * (gate * jax.nn.sigmoid(gate))).astype(BF16)


def _iotas(rows):
    return (lax.broadcasted_iota(jnp.int32, (rows, BLK), 0),
            lax.broadcasted_iota(jnp.int32, (rows, BLK), 1))


def _sb_kernel(q_ref, kt_ref, v_ref, g_ref, o_ref, u_ref, carry_ref, acc_ref, *, nt):
    r2 = lax.broadcasted_iota(jnp.int32, (2 * BLK, 2 * BLK), 0)
    c2 = lax.broadcasted_iota(jnp.int32, (2 * BLK, 2 * BLK), 1)
    u_ref[...] = jnp.where(((r2 >= BLK) == (c2 >= BLK)) & ((r2 & (BLK - 1)) > (c2 & (BLK - 1))),
                           1.0, 0.0).astype(BF16)
    zero_kt = jnp.zeros((HALF, BLK), BF16)

    def step(row0, off, rows, jb, masked, lo=LEAD):
        r0 = _al(row0 + off)
        k0 = _al(jb * BLK)
        sl = slice(off, off + rows)
        ktb = kt_ref[0, :, pl.ds(k0, BLK)]
        kbd = jnp.concatenate([jnp.concatenate([ktb[:HALF], zero_kt], axis=1),
                               jnp.concatenate([zero_kt, ktb[HALF:]], axis=1)], axis=0)
        z2 = _dot(q_ref[0, pl.ds(r0, rows), :], kbd)
        vst = jnp.concatenate([v_ref[0, pl.ds(k0, BLK), :LANES],
                               v_ref[0, pl.ds(k0, BLK), LANES:]], axis=0)
        if masked:
            rowi, lane = _iotas(rows)
            kpos = k0 + lane
            msk = (kpos >= lo) & (kpos < r0 + rowi)
        stays, log_betas = [], []
        for h in range(2):
            z = z2[:, h * BLK:(h + 1) * BLK]
            neg_abs = lax.bitcast_convert_type(
                lax.bitcast_convert_type(z, jnp.int32) | SIGN_BIT, F32)
            stay = jnp.maximum(z, 0.0) + jnp.log(1.0 + jnp.exp(neg_abs))
            log_betas.append(z - stay)
            stays.append(jnp.where(msk, stay, 0.0) if masked else stay)
        cs = _dot(jnp.concatenate([s.astype(BF16) for s in stays], axis=1), u_ref[...])
        ws = []
        for h in range(2):
            carry = carry_ref[h, sl, :]
            w = jnp.exp(log_betas[h] - (cs[:, h * BLK:(h + 1) * BLK] + carry))
            carry_ref[h, sl, :] = carry + jnp.sum(stays[h], axis=1, keepdims=True)
            if masked:
                w = jnp.where(msk, w, 0.0)
            ws.append(w.astype(BF16))
        acc_ref[sl, :] += _dot(jnp.concatenate(ws, axis=1), vst)

    def reset(rows):
        carry_ref[:, :rows, :] = jnp.zeros((2, rows, BLK), F32)
        acc_ref[:rows, :] = jnp.zeros((rows, BLK), F32)

    reset(BLK)
    step(0, 0, BLK, 0, True)
    o_ref[0, :BLK, :] = _gated(acc_ref[:BLK, :], g_ref[0, :BLK, :])

    def tile(t, _):
        row0 = BLK + t * SB_TQ
        jb0 = 1 + t * SB_TB
        reset(SB_TQ)
        for d in reversed(range(SB_TB)):
            step(row0, d * BLK, SB_TQ - d * BLK, jb0 + d, True)

        def body(i, c):
            for u in range(SB_UNROLL):
                step(row0, 0, SB_TQ, jb0 - 1 - (i * SB_UNROLL + u), False)
            return c

        lax.fori_loop(0, t * (SB_TB // SB_UNROLL), body, 0)
        step(row0, 0, SB_TQ, 0, True)
        o_ref[0, pl.ds(_al(row0), SB_TQ), :] = _gated(acc_ref[...], g_ref[0, pl.ds(_al(row0), SB_TQ), :])
        return 0

    lax.fori_loop(0, nt, tile, 0)


def _sb_attention(q, kt, v, gate):
    b, n, w = q.shape
    nt = (n - BLK) // SB_TQ
    return pl.pallas_call(
        functools.partial(_sb_kernel, nt=nt),
        grid=(b, w // LANES),
        in_specs=[pl.BlockSpec((1, n, LANES), lambda i, j: (i, 0, j)),
                  pl.BlockSpec((1, LANES, n), lambda i, j: (i, j, 0)),
                  pl.BlockSpec((1, n, 2 * LANES), lambda i, j: (i, 0, j)),
                  pl.BlockSpec((1, n, LANES), lambda i, j: (i, 0, j))],
        out_specs=pl.BlockSpec((1, n, LANES), lambda i, j: (i, 0, j)),
        out_shape=jax.ShapeDtypeStruct((b, n, w), BF16),
        scratch_shapes=[pltpu.VMEM((2 * BLK, 2 * BLK), BF16),
                        pltpu.VMEM((2, SB_TQ, BLK), F32),
                        pltpu.VMEM((SB_TQ, BLK), F32)],
        compiler_params=pltpu.CompilerParams(
            dimension_semantics=("parallel", "parallel"), vmem_limit_bytes=VMEM_LIMIT),
        name="sb_attention",
    )(q, kt, v, gate)


def _softmax_kernel(q_ref, kt_ref, v_ref, g_ref, o_ref, s_ref, sd_ref, mx_ref, mb_ref, acc_ref, bias_ref,
                    *, nt, mode):
    def visible(kpos, qpos, lo):
        if mode == "chunk":
            return (kpos >= lo) & (_chunk_of(kpos) <= _chunk_of(qpos))
        return (kpos >= lo) & (kpos <= qpos)

    ql = BLK + lax.broadcasted_iota(jnp.int32, (TQ, TQ), 0)
    kl = BLK + lax.broadcasted_iota(jnp.int32, (TQ, TQ), 1)
    bias_ref[...] = jnp.where(visible(kl, ql, 0), 0.0, NEG)

    def chunk_loop(n_chunks, chunk_fn):
        def pair(i, carry):
            chunk_fn(2 * i)
            chunk_fn(2 * i + 1)
            return carry

        lax.fori_loop(0, n_chunks >> 1, pair, 0)

        @pl.when((n_chunks & 1) == 1)
        def _():
            chunk_fn(n_chunks - 1)

    def emit(row0, rows, n_chunks, frames):
        frs = [slice(h * LANES, (h + 1) * LANES) for h in range(2)]
        r0 = _al(row0)
        rowi, lane = _iotas(rows)

        def chunk_scores(c, h):
            k0 = _al(BLK + c * TQ)
            return _dot(q_ref[0, pl.ds(r0, rows), frs[h]], kt_ref[0, frs[h], pl.ds(k0, TQ)])

        def fold_max(c, h, s):
            s_ref[h, :, pl.ds(_al(BLK + c * TQ), TQ)] = s
            m4 = jnp.maximum(jnp.maximum(s[:, :BLK], s[:, BLK:2 * BLK]),
                             jnp.maximum(s[:, 2 * BLK:3 * BLK], s[:, 3 * BLK:]))
            mx_ref[h] = jnp.maximum(mx_ref[h], m4)

        for h, fr in enumerate(frs):
            s0 = _dot(q_ref[0, pl.ds(r0, rows), fr], kt_ref[0, fr, :BLK])
            s0 = jnp.where(visible(lane, r0 + rowi, LEAD), s0, NEG)
            sd_ref[h, :rows, :] = s0
            mx_ref[h, :rows, :] = s0
        if frames:
            for h in range(2):
                fold_max(n_chunks, h, chunk_scores(n_chunks, h) + bias_ref[...])

            def p1(c):
                for h in range(2):
                    fold_max(c, h, chunk_scores(c, h))

            chunk_loop(n_chunks, p1)

        for h, fr in enumerate(frs):
            m_row = jnp.max(mx_ref[h, :rows, :], axis=1, keepdims=True)
            mb_ref[h, :rows, :] = jnp.broadcast_to(m_row, (rows, BLK))
            p0 = jnp.exp(sd_ref[h, :rows, :] - mb_ref[h, :rows, :]).astype(BF16)
            acc_ref[h, :rows, :] = _dot(p0, v_ref[0, :BLK, fr])

        if frames:
            def p2(c):
                k0 = _al(BLK + c * TQ)
                for h, fr in enumerate(frs):
                    mb = mb_ref[h]
                    p = jnp.exp(s_ref[h, :, pl.ds(k0, TQ)]
                                - jnp.concatenate([mb] * TB, axis=1)).astype(BF16)
                    acc_ref[h] += _dot(p, v_ref[0, pl.ds(k0, TQ), fr])

            chunk_loop(n_chunks + 1, p2)

        outs = []
        for h in range(2):
            acc = acc_ref[h, :rows, :]
            outs.append(acc / pltpu.roll(acc, HALF, 1))
        o_ref[0, pl.ds(r0, rows), :] = _gated(jnp.where(lane < HALF, outs[0], outs[1]),
                                              g_ref[0, pl.ds(r0, rows), :])

    emit(0, BLK, 0, False)

    def tile(t, _):
        emit(BLK + t * TQ, TQ, t, True)
        return 0

    lax.fori_loop(0, nt, tile, 0)


def _softmax_attention(q, kt, v, gate, mode):
    b, n, w = q.shape
    nt = (n - BLK) // TQ
    pairs = w // (2 * LANES)
    return pl.pallas_call(
        functools.partial(_softmax_kernel, nt=nt, mode=mode),
        grid=(b, pairs),
        in_specs=[pl.BlockSpec((1, n, 2 * LANES), lambda i, j: (i, 0, j)),
                  pl.BlockSpec((1, 2 * LANES, n), lambda i, j: (i, j, 0)),
                  pl.BlockSpec((1, n, 2 * LANES), lambda i, j: (i, 0, j)),
                  pl.BlockSpec((1, n, LANES), lambda i, j: (i, 0, pairs + j))],
        out_specs=pl.BlockSpec((1, n, LANES), lambda i, j: (i, 0, j)),
        out_shape=jax.ShapeDtypeStruct((b, n, w // 2), BF16),
        scratch_shapes=[pltpu.VMEM((2, TQ, n), F32),
                        pltpu.VMEM((2, TQ, BLK), F32),
                        pltpu.VMEM((2, TQ, BLK), F32),
                        pltpu.VMEM((2, TQ, BLK), F32),
                        pltpu.VMEM((2, TQ, BLK), F32),
                        pltpu.VMEM((TQ, TQ), F32)],
        compiler_params=pltpu.CompilerParams(
            dimension_semantics=("parallel", "parallel"), vmem_limit_bytes=VMEM_LIMIT),
        name="softmax_attention_" + mode,
    )(q, kt, v, gate)


def _swa_kernel(sink_ref, q_ref, kt_ref, v_ref, g_ref, o_ref, *, nb):
    rowi, lane = _iotas(BLK)
    group = N_HEADS // SWA_KV_HEADS
    zero_kt = jnp.zeros((HALF, BLK), BF16)
    r2 = lax.broadcasted_iota(jnp.int32, (2 * BLK, BLK), 0)
    c2 = lax.broadcasted_iota(jnp.int32, (2 * BLK, BLK), 1)
    ones_st = jnp.where((r2 >= BLK) == (c2 >= HALF), 1.0, 0.0).astype(BF16)

    def q_block(m):
        r0 = _al(m * BLK)
        qpos = r0 + rowi
        cq = _chunk_of(qpos)
        slots = ((0, jnp.where(m >= 1, LEAD, NO_KEY)),
                 (jnp.maximum(m - 1, 0), jnp.where(m >= 2, 0, NO_KEY)),
                 (m, LEAD))
        biases = []
        for jb, lo in slots:
            kpos = jb * BLK + lane
            ck = _chunk_of(kpos)
            msk = (kpos >= lo) & (ck <= cq) & ((cq - ck <= WINDOW_CHUNKS) | (kpos < BLK))
            biases.append((msk, jnp.abs(qpos - kpos).astype(F32)))
        for g in range(SWA_KV_HEADS):
            kbds, wcats = [], []
            for jb, _ in slots:
                k0 = _al(jb * BLK)
                ktg =kt_ref[0, g * HALF:(g + 1) * HALF, pl.ds(k0, BLK)]
                kbds.append(jnp.concatenate([jnp.concatenate([ktg, zero_kt], axis=1),
                                             jnp.concatenate([zero_kt, ktg], axis=1)], axis=0))
                vst = jnp.concatenate([v_ref[0, pl.ds(k0, BLK), (2 * g) * LANES:(2 * g + 1) * LANES],
                                       v_ref[0, pl.ds(k0, BLK), (2 * g + 1) * LANES:(2 * g + 2) * LANES]],
                                      axis=0)
                wcats.append(jnp.concatenate([vst, ones_st], axis=1))
            for pr in range(group // 2):
                pair = g * (group // 2) + pr
                q2 = q_ref[0, pl.ds(r0, BLK), pair * LANES:(pair + 1) * LANES]
                zs = [_dot(q2, kbd) for kbd in kbds]
                ps, sink_terms = [[], []], []
                for h in range(2):
                    head = 2 * pair + h
                    slope = 2.0 ** (-8.0 * (head + 1) / N_HEADS)
                    ss = [jnp.where(msk, z[:, h * BLK:(h + 1) * BLK] - slope * dist, NEG)
                          for z, (msk, dist) in zip(zs, biases)]
                    sink = sink_ref[head]
                    mx = jnp.maximum(jnp.maximum(ss[0], ss[1]), ss[2])
                    m_row = jnp.maximum(jnp.max(mx, axis=1, keepdims=True), sink)
                    ps[h] = [jnp.exp(s - m_row).astype(BF16) for s in ss]
                    sink_terms.append(jnp.exp(sink - m_row))
                res = jnp.zeros((BLK, 2 * BLK), F32)
                for i in range(len(slots)):
                    res = res + _dot(jnp.concatenate([ps[0][i], ps[1][i]], axis=1), wcats[i])
                denom = res[:, BLK:] + jnp.where(lane < HALF, sink_terms[0], sink_terms[1])
                cols = slice(pair * LANES, (pair + 1) * LANES)
                o_ref[0, pl.ds(r0, BLK), cols] = _gated(res[:, :BLK] / denom, g_ref[0, pl.ds(r0, BLK), cols])

    def two_blocks(i, carry):
        q_block(2 * i)
        q_block(2 * i + 1)
        return carry

    lax.fori_loop(0, nb // 2, two_blocks, 0)
    if nb % 2:
        q_block(nb - 1)


def _swa_attention(sinks, q, kt, v, gate):
    b, n, w = q.shape
    nb = n // BLK
    return pl.pallas_call(
        functools.partial(_swa_kernel, nb=nb),
        grid=(b,),
        in_specs=[
            pl.BlockSpec(memory_space=pltpu.SMEM),
            pl.BlockSpec((1, n, w), lambda i: (i, 0, 0)),
            pl.BlockSpec((1, kt.shape[1], n), lambda i: (i, 0, 0)),
            pl.BlockSpec((1, n, v.shape[2]), lambda i: (i, 0, 0)),
            pl.BlockSpec((1, n, w), lambda i: (i, 0, 0)),
        ],
        out_specs=pl.BlockSpec((1, n, w), lambda i: (i, 0, 0)),
        out_shape=jax.ShapeDtypeStruct((b, n, w), BF16),
        compiler_params=pltpu.CompilerParams(
            dimension_semantics=("parallel",), vmem_limit_bytes=VMEM_LIMIT),
        name="swa_attention",
    )(sinks, q, kt, v, gate)


def _spread_heads(x, fill):
    lane = lax.broadcasted_iota(jnp.int32, (x.shape[0], LANES), 1)
    blocks = []
    for p in range(x.shape[1] // LANES):
        xp = x[:, p * LANES:(p + 1) * LANES]
        for odd in range(2):
            blk = 2 * p + odd
            if isinstance(fill, float):
                f = fill
            elif callable(fill):
                f = fill(lane)
            else:
                f = fill[:, blk * LANES:(blk + 1) * LANES]
            blocks.append(jnp.where((lane >= HALF) if odd else (lane < HALF), xp, f))
    return jnp.concatenate(blocks, axis=1)


def _even_proj_kernel(x_ref, wsb_ref, wc_ref, wkr_ref, wg_ref, gq_ref, gkv_ref,
                      wa_ref, wb_ref, wkv_ref, cq_ref, sq_ref, ck_ref, sk_ref,
                      qs_ref, kst_ref, vs_ref, qm_ref, kmt_ref, vm_ref, gate_ref):
    xb = x_ref[0].astype(BF16)
    w = N_HEADS * HEAD_DIM
    sb = _dot(xb, wsb_ref[...])
    qs_ref[0] = sb[:, :w].astype(BF16)
    kst_ref[0] = sb[:, w:2 * w].T.astype(BF16)
    vs_ref[0] = _spread_heads(sb[:, 2 * w:], 0.0).astype(BF16)
    gate_ref[0] = _dot(xb, wg_ref[...])

    c = _dot(xb, wc_ref[...])

    def rms(t, g):
        return (t * lax.rsqrt(jnp.mean(t * t, axis=-1, keepdims=True) + RMS_EPS) * g).astype(BF16)

    xq = rms(c[:, :MLA_RANK], gq_ref[...])
    xkv = rms(c[:, MLA_RANK:], gkv_ref[...])
    pairs = N_HEADS // 2
    cq = jnp.concatenate([cq_ref[...]] * pairs, axis=1)
    sq = jnp.concatenate([sq_ref[...]] * pairs, axis=1)
    qm_ref[0] = (_dot(xq, wa_ref[...]) * cq + _dot(xq, wb_ref[...]) * sq).astype(BF16)
    kr = _dot(xb, wkr_ref[...])
    krr = kr[:, :LANES] * ck_ref[...] + kr[:, LANES:] * sk_ref[...]
    krr = (krr, pltpu.roll(krr, HALF, 1))
    kv = _dot(xkv, wkv_ref[...])
    lane = lax.broadcasted_iota(jnp.int32, (kv.shape[0], LANES), 1)
    kblocks, vblocks = [], []
    for hd in range(N_HEADS):
        blk = slice(hd * LANES, (hd + 1) * LANES)
        v_half = (lane >= HALF) if hd % 2 else (lane < HALF)
        kblocks.append(jnp.where(v_half, krr[hd % 2], kv[:, blk]))
        vblocks.append(jnp.where(v_half, kv[:, blk], 1.0))
    kmt_ref[0] = jnp.concatenate(kblocks, axis=1).T.astype(BF16)
    vm_ref[0] = jnp.concatenate(vblocks, axis=1).astype(BF16)


def _even_proj(h, wts, tabs, tm):
    b, n, d = h.shape
    grid = (n // tm, b)
    row = lambda i, j: (j, i, 0)
    col = lambda i, j: (j, 0, i)
    const = lambda i, j: (0, 0)
    tab = lambda i, j: (i, 0)
    full = lambda a: pl.BlockSpec(a.shape, const)
    wide = N_HEADS * LANES
    narrow = N_HEADS * HEAD_DIM
    outs = [(narrow, BF16, False), (narrow, BF16, True), (wide, BF16, False),
            (wide, BF16, False), (wide, BF16, True), (wide, BF16, False), (d, F32, False)]
    return pl.pallas_call(
        _even_proj_kernel,
        grid=grid,
        in_specs=[pl.BlockSpec((1, tm, d), row)] + [full(a) for a in wts] + [
            pl.BlockSpec((tm, 2 * LANES), tab), pl.BlockSpec((tm, 2 * LANES), tab),
            pl.BlockSpec((tm, LANES), tab), pl.BlockSpec((tm, LANES), tab)],
        out_specs=[pl.BlockSpec((1, c, tm), col) if tr else pl.BlockSpec((1, tm, c), row)
                   for c, _, tr in outs],
        out_shape=[jax.ShapeDtypeStruct((b, c, n) if tr else (b, n, c), t) for c, t, tr in outs],
        compiler_params=pltpu.CompilerParams(
            dimension_semantics=("parallel", "parallel"), vmem_limit_bytes=VMEM_LIMIT),
        name="even_proj",
    )(h, *wts, *tabs)


def _odd_proj_kernel(x_ref, wqc_ref, wkc_ref, wvc_ref, wqf_ref, wkf_ref, wvf_ref, wfz_ref, wg_ref,
                     bf_ref, pf_ref,
                     qc_ref, kct_ref, vc_ref, qf_ref, kft_ref, vf_ref, gate_ref, carry_ref, *, tm):
    i = pl.program_id(1)
    xb = x_ref[0].astype(BF16)
    qc_ref[0] = _dot(xb, wqc_ref[...]).astype(BF16)
    kct_ref[0] = _dot(xb, wkc_ref[...]).T.astype(BF16)
    vc_ref[0] = _dot(xb, wvc_ref[...]).astype(BF16)
    gate_ref[0] = _dot(xb, wg_ref[...])

    vf_ref[0] = _spread_heads(_dot(xb, wvf_ref[...]), 1.0).astype(BF16)
    qf_ref[0] = _spread_heads(_dot(xb, wqf_ref[...]),
                              lambda lane: jnp.where((lane & (HALF - 1)) < 3, 1.0, 0.0)).astype(BF16)

    fz = _dot(xb, wfz_ref[...]) + bf_ref[...]
    r = lax.broadcasted_iota(jnp.int32, (tm, LANES), 0) + i * tm
    c = lax.broadcasted_iota(jnp.int32, (tm, LANES), 1)
    lf = jnp.where((c < N_HEADS) & (r >= LEAD), _log_sigmoid(fz), 0.0)

    def split3(t):
        a = t.astype(BF16)
        rem = t - a.astype(F32)
        bb = rem.astype(BF16)
        cc = (rem - bb.astype(F32)).astype(BF16)
        return jnp.concatenate([a, bb, cc], axis=1)

    rr = lax.broadcasted_iota(jnp.int32, (tm, tm), 0)
    cc = lax.broadcasted_iota(jnp.int32, (tm, tm), 1)
    tri = jnp.where(cc <= rr, 1.0, 0.0).astype(BF16)

    @pl.when(i == 0)
    def _():
        carry_ref[...] = jnp.zeros_like(carry_ref)

    c3 = _dot(tri, split3(lf))
    cum = c3[:, :LANES] + c3[:, LANES:2 * LANES] + c3[:, 2 * LANES:] + carry_ref[0:1, :]
    carry_ref[...] = jnp.broadcast_to(cum[tm - 1:tm, :], carry_ref.shape)
    kft_ref[0] = _spread_heads(_dot(xb, wkf_ref[...]), _dot(split3(-cum), pf_ref[...])).T.astype(BF16)


def _odd_proj(h, wts, tm):
    b, n, d = h.shape
    row = lambda j, i: (j, i, 0)
    col = lambda j, i: (j, 0, i)
    const = lambda j, i: (0, 0)
    full = lambda a: pl.BlockSpec(a.shape, const)
    wide = N_HEADS * LANES
    narrow = N_HEADS * HEAD_DIM
    outs = [(narrow, BF16, False), (SWA_KV_HEADS * HEAD_DIM, BF16, True), (narrow, BF16, False),
            (wide, BF16, False), (wide, BF16, True), (wide, BF16, False), (d, F32, False)]
    return pl.pallas_call(
        functools.partial(_odd_proj_kernel, tm=tm),
        grid=(b, n // tm),
        in_specs=[pl.BlockSpec((1, tm, d), row)] + [full(a) for a in wts],
        out_specs=[pl.BlockSpec((1, c, tm), col) if tr else pl.BlockSpec((1, tm, c), row)
                   for c, _, tr in outs],
        out_shape=[jax.ShapeDtypeStruct((b, c, n) if tr else (b, n, c), t) for c, t, tr in outs],
        scratch_shapes=[pltpu.VMEM((8, LANES), F32)],
        compiler_params=pltpu.CompilerParams(
            dimension_semantics=("parallel", "arbitrary"), vmem_limit_bytes=VMEM_LIMIT),
        name="odd_proj",
    )(h, *wts)


def _out_math(ma, mb, h, w_ref, g, b):
    half = ma.shape[-1]
    y = _dot(ma, w_ref[:half, :]) + _dot(mb, w_ref[half:, :])
    r = DN_ALPHA * h + y
    mu = jnp.mean(r, axis=-1, keepdims=True)
    cen = r - mu
    var = jnp.mean(cen * cen, axis=-1, keepdims=True)
    return cen * lax.rsqrt(var + LN_EPS) * g + b


def _out_kernel(ma_ref, mb_ref, h_ref, w_ref, g_ref, b_ref, o_ref):
    o_ref[...] = _out_math(ma_ref[...], mb_ref[...], h_ref[...], w_ref, g_ref[...], b_ref[...])


def _out_frames_kernel(ma_ref, mb_ref, h_ref, w_ref, g_ref, b_ref, o_ref):
    o_ref[0] = _out_math(ma_ref[0], mb_ref[0], h_ref[0], w_ref, g_ref[...], b_ref[...])


def _out_proj_frames(ma, mb, h, w, g, bias):
    b, n, d = h.shape
    half = ma.shape[-1]
    src = lambda i, j: (i, j + 1, 0)
    const = lambda i, j: (0, 0)
    return pl.pallas_call(
        _out_frames_kernel,
        grid=(b, n // BLK - 1),
        in_specs=[pl.BlockSpec((1, BLK, half), src), pl.BlockSpec((1, BLK, half), src),
                  pl.BlockSpec((1, BLK, d), src),
                  pl.BlockSpec(w.shape, const), pl.BlockSpec((1, d), const), pl.BlockSpec((1, d), const)],
        out_specs=pl.BlockSpec((1, BLK, d), lambda i, j: (i, j, 0)),
        out_shape=jax.ShapeDtypeStruct((b, n - BLK, d), F32),
        compiler_params=pltpu.CompilerParams(
            dimension_semantics=("parallel", "parallel"), vmem_limit_bytes=VMEM_LIMIT),
        name="out_proj_ln_frames",
    )(ma, mb, h, w, g, bias)


def _out_proj(ma, mb, h, w, g, bias, tm):
    rows, d = h.shape
    half = ma.shape[-1]
    row = lambda i: (i, 0)
    const = lambda i: (0, 0)
    return pl.pallas_call(
        _out_kernel,
        grid=(rows // tm,),
        in_specs=[pl.BlockSpec((tm, half), row), pl.BlockSpec((tm, half), row),
                  pl.BlockSpec((tm, d), row),
                  pl.BlockSpec(w.shape, const), pl.BlockSpec((1, d), const), pl.BlockSpec((1, d), const)],
        out_specs=pl.BlockSpec((tm, d), row),
        out_shape=jax.ShapeDtypeStruct((rows, d), F32),
        compiler_params=pltpu.CompilerParams(
            dimension_semantics=("parallel",), vmem_limit_bytes=VMEM_LIMIT),
        name="out_proj_ln",
    )(ma, mb, h, w, g, bias)


def _by_parity(even, odd):
    is_odd = (jnp.arange(N_HEADS) % 2 == 1)[:, None]
    return jnp.where(is_odd, odd, even).reshape(even.shape[:-2] + (N_HEADS * LANES,))


def _mla_head_blocks(nope, rot):
    spare = jnp.zeros(rot.shape[:-1] + (HALF - MLA_ROPE,), rot.dtype)
    return _by_parity(jnp.concatenate([rot, spare, nope], axis=-1),
                      jnp.concatenate([nope, rot, spare], axis=-1))


def _even_weights(w_in, g_cq, g_ckv, w_uq, w_ukv):
    a = N_HEADS * HEAD_DIM
    wq, wk, wv = w_in[:, :a], w_in[:, a:2 * a], w_in[:, 2 * a:3 * a]
    o = 3 * a
    w_c = w_in[:, o:o + 2 * MLA_RANK]
    o += 2 * MLA_RANK
    w_kr = w_in[:, o:o + MLA_ROPE]
    o += MLA_ROPE
    w_g = w_in[:, o:]
    w_sb = jnp.concatenate([wq * HEAD_DIM ** -0.5, wk, wv], axis=1)
    hr = MLA_ROPE // 2
    zpad = jnp.zeros((w_in.shape[0], LANES - MLA_ROPE), w_in.dtype)
    w_kr2 = jnp.concatenate([w_kr, zpad, w_kr[:, hr:], w_kr[:, :hr], zpad], axis=1)

    uq = w_uq.reshape(MLA_RANK, N_HEADS, MLA_NOPE + MLA_ROPE)
    ukv = w_ukv.reshape(MLA_RANK, N_HEADS, MLA_NOPE + HEAD_DIM)
    nope, r1, r2 = uq[..., :MLA_NOPE], uq[..., MLA_NOPE:MLA_NOPE + hr], uq[..., MLA_NOPE + hr:]
    w_a = _mla_head_blocks(nope, jnp.concatenate([r1, r2], axis=-1))
    w_b = _mla_head_blocks(jnp.zeros_like(nope), jnp.concatenate([r2, r1], axis=-1))
    k_nope, v = ukv[..., :MLA_NOPE], ukv[..., MLA_NOPE:]
    w_kv = _by_parity(jnp.concatenate([v, k_nope], axis=-1), ukv)
    bf = lambda t: t.astype(BF16)
    return (bf(w_sb), bf(w_c), bf(w_kr2), bf(w_g), g_cq[None, :], g_ckv[None, :],
            bf(w_a), bf(w_b), bf(w_kv))


def _rope_tables(n):
    hr = MLA_ROPE // 2
    pos = jnp.arange(n, dtype=F32) - LEAD
    inv = ROPE_BASE ** (-jnp.arange(hr, dtype=F32) / hr)
    ang = pos[:, None] * inv[None, :]
    cos, sin = jnp.cos(ang), jnp.sin(ang)
    ones = jnp.ones((n, MLA_NOPE), F32)
    scale = (MLA_NOPE + MLA_ROPE) ** -0.5
    c_rot, s_rot = jnp.concatenate([cos, cos], axis=1), jnp.concatenate([-sin, sin], axis=1)
    spare = jnp.zeros((n, HALF - MLA_ROPE), F32)
    c_q = jnp.concatenate([c_rot, spare, ones, ones, c_rot, spare], axis=1) * scale
    s_q = jnp.concatenate([s_rot, spare, 0.0 * ones, 0.0 * ones, s_rot, spare], axis=1) * scale
    z96 = jnp.zeros((n, LANES - MLA_ROPE), F32)
    c_k = jnp.concatenate([c_rot, z96], axis=1)
    s_k = jnp.concatenate([s_rot, z96], axis=1)
    return c_q, s_q, c_k, s_k


def _odd_weights(w_in, b_forget):
    a = N_HEADS * HEAD_DIM
    kvw = SWA_KV_HEADS * HEAD_DIM
    wqc = w_in[:, :a]
    o = a
    wkc = w_in[:, o:o + kvw]
    o += kvw
    wvc = w_in[:, o:o + kvw]
    o += kvw
    wqd, wkd, wvd = w_in[:, o:o + a], w_in[:, o + a:o + 2 * a], w_in[:, o + 2 * a:o + 3 * a]
    o += 3 * a
    wfz = w_in[:, o:o + N_HEADS]
    o += N_HEADS
    w_g = w_in[:, o:]
    scale = HEAD_DIM ** -0.5
    v0, v1 = wvc[:, :HEAD_DIM], wvc[:, HEAD_DIM:]
    z = jnp.zeros_like(v0)
    w_vc = jnp.concatenate([v0, z, z, v0, v1, z, z, v1], axis=1)
    w_fz = jnp.concatenate([wfz, jnp.zeros((w_in.shape[0], LANES - N_HEADS), w_in.dtype)], axis=1)
    b_f = jnp.concatenate([b_forget, jnp.zeros((LANES - N_HEADS,), b_forget.dtype)])[None, :]
    p_f = np.zeros((3 * LANES, N_HEADS * LANES), np.float32)
    for t in range(3):
        for h in range(N_HEADS):
            p_f[t * LANES + h, h * LANES + (0 if h % 2 else HALF) + t] = 1.0
    bf = lambda t: t.astype(BF16)
    return (bf(wqc * scale), bf(wkc), bf(w_vc), bf(wqd * scale), bf(wkd), bf(wvd), bf(w_fz), bf(w_g),
            b_f.astype(F32), jnp.asarray(p_f, BF16))


def _row_tile(nb, cap):
    g = max(t for t in range(1, cap + 1) if nb % t == 0)
    return g * BLK


def kernel(x, meta_tokens, w_in_even, g_cq, g_ckv, w_uq, w_ukv, w_out_even,
           w_in_odd, b_forget, sink_logits, w_out_odd, ln_gain, ln_bias):
    b, s, d = x.shape
    assert s % SB_TQ == 0 and s % TQ == 0 and d == D_MODEL
    n = BLK + s
    nb = n // BLK
    h = jnp.concatenate([jnp.zeros((b, LEAD, d), x.dtype),
                         jnp.broadcast_to(meta_tokens[None].astype(x.dtype), (b, N_META, d)), x], axis=1)
    tabs = _rope_tables(n)
    tm_proj = _row_tile(nb, 3)
    tm_out = _row_tile(b * nb, 4)
    for layer in range(DEPTH):
        i = layer // 2
        if layer % 2 == 0:
            wts = _even_weights(w_in_even[i], g_cq[i], g_ckv[i], w_uq[i], w_ukv[i])
            qs, kst, vs, qm, kmt, vm, gate = _even_proj(h, wts, tabs, tm_proj)
            mix_a = _sb_attention(qs, kst, vs, gate)
            mix_b = _softmax_attention(qm, kmt, vm, gate, "chunk")
            w_out = w_out_even[i]
        else:
            wts = _odd_weights(w_in_odd[i], b_forget[i])
            qc, kct, vc, qf, kft, vf, gate = _odd_proj(h, wts, tm_proj)
            mix_a = _swa_attention(sink_logits[i].astype(F32), qc, kct, vc, gate)
            mix_b = _softmax_attention(qf, kft, vf, gate, "causal")
            w_out = w_out_odd[i]
        ln = (w_out.astype(BF16), ln_gain[layer][None, :], ln_bias[layer][None, :])
        if layer == DEPTH - 1:
            return _out_proj_frames(mix_a, mix_b, h, *ln)
        rows = b * n
        h = _out_proj(mix_a.reshape(rows, -1), mix_b.reshape(rows, -1), h.reshape(rows, d),
                      *ln, tm_out).reshape(b, n, d)
```

```python
import functools

import jax
import jax.numpy as jnp
import numpy as np
from jax import lax
from jax.experimental import pallas as pl
from jax.experimental.pallas import tpu as pltpu

D_MODEL = 1024
CHUNK = 64
N_META = 16
HEAD_DIM = 64
N_HEADS = 8
MLA_NOPE = 64
MLA_ROPE = 32
MLA_RANK = 256
SWA_KV_HEADS = 2
WINDOW_CHUNKS = 2
ROPE_BASE = 10000.0
DEPTH = 4
DN_ALPHA = (2 * DEPTH) ** 0.25
LN_EPS = 1e-5
RMS_EPS = 1e-6
NEG = -1e30
NO_KEY = 1 << 30
SIGN_BIT = -(1 << 31)

LANES = 128
BLK = 128
TB = 4
TQ = TB * BLK
SB_TB = 8
SB_TQ = SB_TB * BLK
SB_UNROLL = 4
LEAD = BLK - N_META
HALF = LANES // 2
VMEM_LIMIT = 56 * 1024 * 1024

F32 = jnp.float32
BF16 = jnp.bfloat16


def _dot(a, b):
    return jnp.dot(a, b, preferred_element_type=F32)


def _chunk_of(pos):
    return jnp.maximum((pos >> 6) - 1, 0)


def _log_sigmoid(x):
    return jnp.minimum(x, 0.0) - jnp.log(1.0 + jnp.exp(-jnp.abs(x)))


def _al(x):
    return x if isinstance(x, int) else pl.multiple_of(x, BLK)


def _gated(mixed, gate):
    return (mixed * (gate * jax.nn.sigmoid(gate))).astype(BF16)


def _iotas(rows):
    return (lax.broadcasted_iota(jnp.int32, (rows, BLK), 0),
            lax.broadcasted_iota(jnp.int32, (rows, BLK), 1))


def _sb_kernel(q_ref, kt_ref, v_ref, g_ref, o_ref, u_ref, carry_ref, acc_ref, *, nt):
    r2 = lax.broadcasted_iota(jnp.int32, (2 * BLK, 2 * BLK), 0)
    c2 = lax.broadcasted_iota(jnp.int32, (2 * BLK, 2 * BLK), 1)
    u_ref[...] = jnp.where(((r2 >= BLK) == (c2 >= BLK)) & ((r2 & (BLK - 1)) > (c2 & (BLK - 1))),
                           1.0, 0.0).astype(BF16)
    zero_kt = jnp.zeros((HALF, BLK), BF16)

    def step(row0, off, rows, jb, mask=None):
        r0 = _al(row0 + off)
        k0 = _al(jb * BLK)
        sl = slice(off, off + rows)
        ktb = kt_ref[0, :, pl.ds(k0, BLK)]
        kbd = jnp.concatenate([jnp.concatenate([ktb[:HALF], zero_kt], axis=1),
                               jnp.concatenate([zero_kt, ktb[HALF:]], axis=1)], axis=0)
        z2 = _dot(q_ref[0, pl.ds(r0, rows), :], kbd)
        vst = jnp.concatenate([v_ref[0, pl.ds(k0, BLK), :LANES],
                               v_ref[0, pl.ds(k0, BLK), LANES:]], axis=0)
        if mask == "meta_keys":
            parts = [(0, rows, _iotas(rows)[1] >= LEAD)]
        elif mask is not None:
            rowi, lane = _iotas(BLK)
            local = lane < rowi
            parts = [(0, BLK, local if mask == "diagonal" else local & (lane >= LEAD))]
            if rows > BLK:
                parts.append((BLK, rows, None))

        def masked(x):
            if mask is None:
                return x
            return jnp.concatenate([x[a:b] if m is None else jnp.where(m, x[a:b], 0.0)
                                    for a, b, m in parts], axis=0)

        stays, log_betas = [], []
        for h in range(2):
            z = z2[:, h * BLK:(h + 1) * BLK]
            neg_abs = lax.bitcast_convert_type(
                lax.bitcast_convert_type(z, jnp.int32) | SIGN_BIT, F32)
            stay = jnp.maximum(z, 0.0) + jnp.log(1.0 + jnp.exp(neg_abs))
            log_betas.append(z - stay)
            stays.append(masked(stay))
        cs = _dot(jnp.concatenate([s.astype(BF16) for s in stays], axis=1), u_ref[...])
        ws = []
        for h in range(2):
            carry = carry_ref[h, sl, :]
            w = masked(jnp.exp(log_betas[h] - (cs[:, h * BLK:(h + 1) * BLK] + carry)))
            carry_ref[h, sl, :] = carry + jnp.sum(stays[h], axis=1, keepdims=True)
            ws.append(w.astype(BF16))
        acc_ref[sl, :] += _dot(jnp.concatenate(ws, axis=1), vst)

    def reset(rows):
        carry_ref[:, :rows, :] = jnp.zeros((2, rows, BLK), F32)
        acc_ref[:rows, :] = jnp.zeros((rows, BLK), F32)

    reset(BLK)
    step(0, 0, BLK, 0, "meta_tile")
    o_ref[0, :BLK, :] = _gated(acc_ref[:BLK, :], g_ref[0, :BLK, :])

    def tile(t, _):
        row0 = BLK + t * SB_TQ
        jb0 = 1 + t * SB_TB
        reset(SB_TQ)
        for d in reversed(range(SB_TB)):
            step(row0, d * BLK, SB_TQ - d * BLK, jb0 + d, "diagonal")

        def body(i, c):
            for u in range(SB_UNROLL):
                step(row0, 0, SB_TQ, jb0 - 1 - (i * SB_UNROLL + u))
            return c

        lax.fori_loop(0, t * (SB_TB // SB_UNROLL), body, 0)
        step(row0, 0, SB_TQ, 0, "meta_keys")
        o_ref[0, pl.ds(_al(row0), SB_TQ), :] = _gated(acc_ref[...], g_ref[0, pl.ds(_al(row0), SB_TQ), :])
        return 0

    lax.fori_loop(0, nt, tile, 0)


def _sb_attention(q, kt, v, gate):
    b, n, w = q.shape
    nt = (n - BLK) // SB_TQ
    return pl.pallas_call(
        functools.partial(_sb_kernel, nt=nt),
        grid=(b, w // LANES),
        in_specs=[pl.BlockSpec((1, n, LANES), lambda i, j: (i, 0, j)),
                  pl.BlockSpec((1, LANES, n), lambda i, j: (i, j, 0)),
                  pl.BlockSpec((1, n, 2 * LANES), lambda i, j: (i, 0, j)),
                  pl.BlockSpec((1, n, LANES), lambda i, j: (i, 0, j))],
        out_specs=pl.BlockSpec((1, n, LANES), lambda i, j: (i, 0, j)),
        out_shape=jax.ShapeDtypeStruct((b, n, w), BF16),
        scratch_shapes=[pltpu.VMEM((2 * BLK, 2 * BLK), BF16),
                        pltpu.VMEM((2, SB_TQ, BLK), F32),
                        pltpu.VMEM((SB_TQ, BLK), F32)],
        compiler_params=pltpu.CompilerParams(
            dimension_semantics=("parallel", "parallel"), vmem_limit_bytes=VMEM_LIMIT),
        name="sb_attention",
    )(q, kt, v, gate)


def _softmax_kernel(q_ref, kt_ref, v_ref, g_ref, o_ref, s_ref, sd_ref, mx_ref, mb_ref, acc_ref, bias_ref,
                    *, nt, mode):
    def visible(kpos, qpos, lo):
        if mode == "chunk":
            return (kpos >= lo) & (_chunk_of(kpos) <= _chunk_of(qpos))
        return (kpos >= lo) & (kpos <= qpos)

    ql = BLK + lax.broadcasted_iota(jnp.int32, (TQ, TQ), 0)
    kl = BLK + lax.broadcasted_iota(jnp.int32, (TQ, TQ), 1)
    bias_ref[...] = jnp.where(visible(kl, ql, 0), 0.0, NEG)

    def chunk_loop(n_chunks, chunk_fn):
        def pair(i, carry):
            chunk_fn(2 * i)
            chunk_fn(2 * i + 1)
            return carry

        lax.fori_loop(0, n_chunks >> 1, pair, 0)

        @pl.when((n_chunks & 1) == 1)
        def _():
            chunk_fn(n_chunks - 1)

    def emit(row0, rows, n_chunks, frames):
        frs = [slice(h * LANES, (h + 1) * LANES) for h in range(2)]
        r0 = _al(row0)
        rowi, lane = _iotas(rows)

        def chunk_scores(c, h):
            k0 = _al(BLK + c * TQ)
            return _dot(q_ref[0, pl.ds(r0, rows), frs[h]], kt_ref[0, frs[h], pl.ds(k0, TQ)])

        def fold_max(c, h, s):
            s_ref[h, :, pl.ds(_al(BLK + c * TQ), TQ)] = s
            m4 = jnp.maximum(jnp.maximum(s[:, :BLK], s[:, BLK:2 * BLK]),
                             jnp.maximum(s[:, 2 * BLK:3 * BLK], s[:, 3 * BLK:]))
            mx_ref[h] = jnp.maximum(mx_ref[h], m4)

        for h, fr in enumerate(frs):
            s0 = _dot(q_ref[0, pl.ds(r0, rows), fr], kt_ref[0, fr, :BLK])
            s0 = jnp.where(visible(lane, r0 + rowi, LEAD), s0, NEG)
            sd_ref[h, :rows, :] = s0
            mx_ref[h, :rows, :] = s0
        if frames:
            for h in range(2):
                fold_max(n_chunks, h, chunk_scores(n_chunks, h) + bias_ref[...])

            def p1(c):
                for h in range(2):
                    fold_max(c, h, chunk_scores(c, h))

            chunk_loop(n_chunks, p1)

        for h, fr in enumerate(frs):
            m_row = jnp.max(mx_ref[h, :rows, :], axis=1, keepdims=True)
            mb_ref[h, :rows, :] = jnp.broadcast_to(m_row, (rows, BLK))
            p0 = jnp.exp(sd_ref[h, :rows, :] - mb_ref[h, :rows, :]).astype(BF16)
            acc_ref[h, :rows, :] = _dot(p0, v_ref[0, :BLK, fr])

        if frames:
            def p2(c):
                k0 = _al(BLK + c * TQ)
                for h, fr in enumerate(frs):
                    mb = mb_ref[h]
                    p = jnp.exp(s_ref[h, :, pl.ds(k0, TQ)]
                                - jnp.concatenate([mb] * TB, axis=1)).astype(BF16)
                    acc_ref[h] += _dot(p, v_ref[0, pl.ds(k0, TQ), fr])

            chunk_loop(n_chunks + 1, p2)

        outs = []
        for h in range(2):
            acc = acc_ref[h, :rows, :]
            outs.append(acc / pltpu.roll(acc, HALF, 1))
        o_ref[0, pl.ds(r0, rows), :] = _gated(jnp.where(lane < HALF, outs[0], outs[1]),
                                              g_ref[0, pl.ds(r0, rows), :])

    emit(0, BLK, 0, False)

    def tile(t, _):
        emit(BLK + t * TQ, TQ, t, True)
        return 0

    lax.fori_loop(0, nt, tile, 0)


def _softmax_attention(q, kt, v, gate, mode):
    b, n, w = q.shape
    nt = (n - BLK) // TQ
    pairs = w // (2 * LANES)
    return pl.pallas_call(
        functools.partial(_softmax_kernel, nt=nt, mode=mode),
        grid=(b, pairs),
        in_specs=[pl.BlockSpec((1, n, 2 * LANES), lambda i, j: (i, 0, j)),
                  pl.BlockSpec((1, 2 * LANES, n), lambda i, j: (i, j, 0)),
                  pl.BlockSpec((1, n, 2 * LANES), lambda i, j: (i, 0, j)),
                  pl.BlockSpec((1, n, LANES), lambda i, j: (i, 0, pairs + j))],
        out_specs=pl.BlockSpec((1, n, LANES), lambda i, j: (i, 0, j)),
        out_shape=jax.ShapeDtypeStruct((b, n, w // 2), BF16),
        scratch_shapes=[pltpu.VMEM((2, TQ, n), F32),
                        pltpu.VMEM((2, TQ, BLK), F32),
                        pltpu.VMEM((2, TQ, BLK), F32),
                        pltpu.VMEM((2, TQ, BLK), F32),
                        pltpu.VMEM((2, TQ, BLK), F32),
                        pltpu.VMEM((TQ, TQ), F32)],
        compiler_params=pltpu.CompilerParams(
            dimension_semantics=("parallel", "parallel"), vmem_limit_bytes=VMEM_LIMIT),
        name="softmax_attention_" + mode,
    )(q, kt, v, gate)


def _swa_kernel(sink_ref, q_ref, kt_ref, v_ref, g_ref, o_ref, *, nb):
    rowi, lane = _iotas(BLK)
    group = N_HEADS // SWA_KV_HEADS
    zero_kt = jnp.zeros((HALF, BLK), BF16)
    r2 = lax.broadcasted_iota(jnp.int32, (2 * BLK, BLK), 0)
    c2 = lax.broadcasted_iota(jnp.int32, (2 * BLK, BLK), 1)
    ones_st = jnp.where((r2 >= BLK) == (c2 >= HALF), 1.0, 0.0).astype(BF16)

    def q_block(m):
        r0 = _al(m * BLK)
        qpos = r0 + rowi
        cq = _chunk_of(qpos)
        slots = ((0, jnp.where(m >= 1, LEAD, NO_KEY)),
                 (jnp.maximum(m - 1, 0), jnp.where(m >= 2, 0, NO_KEY)),
                 (m, LEAD))
        biases = []
        for jb, lo in slots:
            kpos = jb * BLK + lane
            ck = _chunk_of(kpos)
            msk = (kpos >= lo) & (ck <= cq) & ((cq - ck <= WINDOW_CHUNKS) | (kpos < BLK))
            biases.append((msk, jnp.abs(qpos - kpos).astype(F32)))
        for g in range(SWA_KV_HEADS):
            kbds, wcats = [], []
            for jb, _ in slots:
                k0 = _al(jb * BLK)
                ktg =kt_ref[0, g * HALF:(g + 1) * HALF, pl.ds(k0, BLK)]
                kbds.append(jnp.concatenate([jnp.concatenate([ktg, zero_kt], axis=1),
                                             jnp.concatenate([zero_kt, ktg], axis=1)], axis=0))
                vst = jnp.concatenate([v_ref[0, pl.ds(k0, BLK), (2 * g) * LANES:(2 * g + 1) * LANES],
                                       v_ref[0, pl.ds(k0, BLK), (2 * g + 1) * LANES:(2 * g + 2) * LANES]],
                                      axis=0)
                wcats.append(jnp.concatenate([vst, ones_st], axis=1))
            for pr in range(group // 2):
                pair = g * (group // 2) + pr
                q2 = q_ref[0, pl.ds(r0, BLK), pair * LANES:(pair + 1) * LANES]
                zs = [_dot(q2, kbd) for kbd in kbds]
                ps, sink_terms = [[], []], []
                for h in range(2):
                    head = 2 * pair + h
                    slope = 2.0 ** (-8.0 * (head + 1) / N_HEADS)
                    ss = [jnp.where(msk, z[:, h * BLK:(h + 1) * BLK] - slope * dist, NEG)
                          for z, (msk, dist) in zip(zs, biases)]
                    sink = sink_ref[head]
                    mx = jnp.maximum(jnp.maximum(ss[0], ss[1]), ss[2])
                    m_row = jnp.maximum(jnp.max(mx, axis=1, keepdims=True), sink)
                    ps[h] = [jnp.exp(s - m_row).astype(BF16) for s in ss]
                    sink_terms.append(jnp.exp(sink - m_row))
                res = jnp.zeros((BLK, 2 * BLK), F32)
                for i in range(len(slots)):
                    res = res + _dot(jnp.concatenate([ps[0][i], ps[1][i]], axis=1), wcats[i])
                denom = res[:, BLK:] + jnp.where(lane < HALF, sink_terms[0], sink_terms[1])
                cols = slice(pair * LANES, (pair + 1) * LANES)
                o_ref[0, pl.ds(r0, BLK), cols] = _gated(res[:, :BLK] / denom, g_ref[0, pl.ds(r0, BLK), cols])

    def two_blocks(i, carry):
        q_block(2 * i)
        q_block(2 * i + 1)
        return carry

    lax.fori_loop(0, nb // 2, two_blocks, 0)
    if nb % 2:
        q_block(nb - 1)


def _swa_attention(sinks, q, kt, v, gate):
    b, n, w = q.shape
    nb = n // BLK
    return pl.pallas_call(
        functools.partial(_swa_kernel, nb=nb),
        grid=(b,),
        in_specs=[
            pl.BlockSpec(memory_space=pltpu.SMEM),
            pl.BlockSpec((1, n, w), lambda i: (i, 0, 0)),
            pl.BlockSpec((1, kt.shape[1], n), lambda i: (i, 0, 0)),
            pl.BlockSpec((1, n, v.shape[2]), lambda i: (i, 0, 0)),
            pl.BlockSpec((1, n, w), lambda i: (i, 0, 0)),
        ],
        out_specs=pl.BlockSpec((1, n, w), lambda i: (i, 0, 0)),
        out_shape=jax.ShapeDtypeStruct((b, n, w), BF16),
        compiler_params=pltpu.CompilerParams(
            dimension_semantics=("parallel",), vmem_limit_bytes=VMEM_LIMIT),
        name="swa_attention",
    )(sinks, q, kt, v, gate)


def _spread_heads(x, fill):
    lane = lax.broadcasted_iota(jnp.int32, (x.shape[0], LANES), 1)
    blocks = []
    for p in range(x.shape[1] // LANES):
        xp = x[:, p * LANES:(p + 1) * LANES]
        for odd in range(2):
            blk = 2 * p + odd
            if isinstance(fill, float):
                f = fill
            elif callable(fill):
                f = fill(lane)
            else:
                f = fill[:, blk * LANES:(blk + 1) * LANES]
            blocks.append(jnp.where((lane >= HALF) if odd else (lane < HALF), xp, f))
    return jnp.concatenate(blocks, axis=1)


def _even_proj_kernel(x_ref, wsb_ref, wc_ref, wkr_ref, wg_ref, gq_ref, gkv_ref,
                      wa_ref, wb_ref, wkv_ref, cq_ref, sq_ref, ck_ref, sk_ref,
                      qs_ref, kst_ref, vs_ref, qm_ref, kmt_ref, vm_ref, gate_ref):
    xb = x_ref[0].astype(BF16)
    w = N_HEADS * HEAD_DIM
    sb = _dot(xb, wsb_ref[...])
    qs_ref[0] = sb[:, :w].astype(BF16)
    kst_ref[0] = sb[:, w:2 * w].T.astype(BF16)
    vs_ref[0] = _spread_heads(sb[:, 2 * w:], 0.0).astype(BF16)
    gate_ref[0] = _dot(xb, wg_ref[...])

    c = _dot(xb, wc_ref[...])

    def rms(t, g):
        return (t * lax.rsqrt(jnp.mean(t * t, axis=-1, keepdims=True) + RMS_EPS) * g).astype(BF16)

    xq = rms(c[:, :MLA_RANK], gq_ref[...])
    xkv = rms(c[:, MLA_RANK:], gkv_ref[...])
    pairs = N_HEADS // 2
    cq = jnp.concatenate([cq_ref[...]] * pairs, axis=1)
    sq = jnp.concatenate([sq_ref[...]] * pairs, axis=1)
    qm_ref[0] = (_dot(xq, wa_ref[...]) * cq + _dot(xq, wb_ref[...]) * sq).astype(BF16)
    kr = _dot(xb, wkr_ref[...])
    krr = kr[:, :LANES] * ck_ref[...] + kr[:, LANES:] * sk_ref[...]
    krr = (krr, pltpu.roll(krr, HALF, 1))
    kv = _dot(xkv, wkv_ref[...])
    lane = lax.broadcasted_iota(jnp.int32, (kv.shape[0], LANES), 1)
    kblocks, vblocks = [], []
    for hd in range(N_HEADS):
        blk = slice(hd * LANES, (hd + 1) * LANES)
        v_half = (lane >= HALF) if hd % 2 else (lane < HALF)
        kblocks.append(jnp.where(v_half, krr[hd % 2], kv[:, blk]))
        vblocks.append(jnp.where(v_half, kv[:, blk], 1.0))
    kmt_ref[0] = jnp.concatenate(kblocks, axis=1).T.astype(BF16)
    vm_ref[0] = jnp.concatenate(vblocks, axis=1).astype(BF16)


def _even_proj(h, wts, tabs, tm):
    b, n, d = h.shape
    grid = (n // tm, b)
    row = lambda i, j: (j, i, 0)
    col = lambda i, j: (j, 0, i)
    const = lambda i, j: (0, 0)
    tab = lambda i, j: (i, 0)
    full = lambda a: pl.BlockSpec(a.shape, const)
    wide = N_HEADS * LANES
    narrow = N_HEADS * HEAD_DIM
    outs = [(narrow, BF16, False), (narrow, BF16, True), (wide, BF16, False),
            (wide, BF16, False), (wide, BF16, True), (wide, BF16, False), (d, F32, False)]
    return pl.pallas_call(
        _even_proj_kernel,
        grid=grid,
        in_specs=[pl.BlockSpec((1, tm, d), row)] + [full(a) for a in wts] + [
            pl.BlockSpec((tm, 2 * LANES), tab), pl.BlockSpec((tm, 2 * LANES), tab),
            pl.BlockSpec((tm, LANES), tab), pl.BlockSpec((tm, LANES), tab)],
        out_specs=[pl.BlockSpec((1, c, tm), col) if tr else pl.BlockSpec((1, tm, c), row)
                   for c, _, tr in outs],
        out_shape=[jax.ShapeDtypeStruct((b, c, n) if tr else (b, n, c), t) for c, t, tr in outs],
        compiler_params=pltpu.CompilerParams(
            dimension_semantics=("parallel", "parallel"), vmem_limit_bytes=VMEM_LIMIT),
        name="even_proj",
    )(h, *wts, *tabs)


def _odd_proj_kernel(x_ref, wqc_ref, wkc_ref, wvc_ref, wqf_ref, wkf_ref, wvf_ref, wfz_ref, wg_ref,
                     bf_ref, pf_ref,
                     qc_ref, kct_ref, vc_ref, qf_ref, kft_ref, vf_ref, gate_ref, carry_ref, *, tm):
    i = pl.program_id(1)
    xb = x_ref[0].astype(BF16)
    qc_ref[0] = _dot(xb, wqc_ref[...]).astype(BF16)
    kct_ref[0] = _dot(xb, wkc_ref[...]).T.astype(BF16)
    vc_ref[0] = _dot(xb, wvc_ref[...]).astype(BF16)
    gate_ref[0] = _dot(xb, wg_ref[...])

    vf_ref[0] = _spread_heads(_dot(xb, wvf_ref[...]), 1.0).astype(BF16)
    qf_ref[0] = _spread_heads(_dot(xb, wqf_ref[...]),
                              lambda lane: jnp.where((lane & (HALF - 1)) < 3, 1.0, 0.0)).astype(BF16)

    fz = _dot(xb, wfz_ref[...]) + bf_ref[...]
    r = lax.broadcasted_iota(jnp.int32, (tm, LANES), 0) + i * tm
    c = lax.broadcasted_iota(jnp.int32, (tm, LANES), 1)
    lf = jnp.where((c < N_HEADS) & (r >= LEAD), _log_sigmoid(fz), 0.0)

    def split3(t):
        a = t.astype(BF16)
        rem = t - a.astype(F32)
        bb = rem.astype(BF16)
        cc = (rem - bb.astype(F32)).astype(BF16)
        return jnp.concatenate([a, bb, cc], axis=1)

    rr = lax.broadcasted_iota(jnp.int32, (tm, tm), 0)
    cc = lax.broadcasted_iota(jnp.int32, (tm, tm), 1)
    tri = jnp.where(cc <= rr, 1.0, 0.0).astype(BF16)

    @pl.when(i == 0)
    def _():
        carry_ref[...] = jnp.zeros_like(carry_ref)

    c3 = _dot(tri, split3(lf))
    cum = c3[:, :LANES] + c3[:, LANES:2 * LANES] + c3[:, 2 * LANES:] + carry_ref[0:1, :]
    carry_ref[...] = jnp.broadcast_to(cum[tm - 1:tm, :], carry_ref.shape)
    kft_ref[0] = _spread_heads(_dot(xb, wkf_ref[...]), _dot(split3(-cum), pf_ref[...])).T.astype(BF16)


def _odd_proj(h, wts, tm):
    b, n, d = h.shape
    row = lambda j, i: (j, i, 0)
    col = lambda j, i: (j, 0, i)
    const = lambda j, i: (0, 0)
    full = lambda a: pl.BlockSpec(a.shape, const)
    wide = N_HEADS * LANES
    narrow = N_HEADS * HEAD_DIM
    outs = [(narrow, BF16, False), (SWA_KV_HEADS * HEAD_DIM, BF16, True), (narrow, BF16, False),
            (wide, BF16, False), (wide, BF16, True), (wide, BF16, False), (d, F32, False)]
    return pl.pallas_call(
        functools.partial(_odd_proj_kernel, tm=tm),
        grid=(b, n // tm),
        in_specs=[pl.BlockSpec((1, tm, d), row)] + [full(a) for a in wts],
        out_specs=[pl.BlockSpec((1, c, tm), col) if tr else pl.BlockSpec((1, tm, c), row)
                   for c, _, tr in outs],
        out_shape=[jax.ShapeDtypeStruct((b, c, n) if tr else (b, n, c), t) for c, t, tr in outs],
        scratch_shapes=[pltpu.VMEM((8, LANES), F32)],
        compiler_params=pltpu.CompilerParams(
            dimension_semantics=("parallel", "arbitrary"), vmem_limit_bytes=VMEM_LIMIT),
        name="odd_proj",
    )(h, *wts)


def _out_math(ma, mb, h, w_ref, g, b):
    half = ma.shape[-1]
    y = _dot(ma, w_ref[:half, :]) + _dot(mb, w_ref[half:, :])
    r = DN_ALPHA * h + y
    mu = jnp.mean(r, axis=-1, keepdims=True)
    cen = r - mu
    var = jnp.mean(cen * cen, axis=-1, keepdims=True)
    return cen * lax.rsqrt(var + LN_EPS) * g + b


def _out_kernel(ma_ref, mb_ref, h_ref, w_ref, g_ref, b_ref, o_ref):
    o_ref[...] = _out_math(ma_ref[...], mb_ref[...], h_ref[...], w_ref, g_ref[...], b_ref[...])


def _out_frames_kernel(ma_ref, mb_ref, h_ref, w_ref, g_ref, b_ref, o_ref):
    o_ref[0] = _out_math(ma_ref[...], mb_ref[...], h_ref[...], w_ref, g_ref[...], b_ref[...])


def _out_proj_frames(ma, mb, h, w, g, bias):
    b, n, d = h.shape
    half = ma.shape[-1]
    flat = lambda a: a.reshape(b * n, a.shape[-1])
    src = lambda i, j: (pl.multiple_of(i * n + BLK + j * TQ, BLK), 0)
    window = lambda width: pl.BlockSpec((pl.Element(TQ), pl.Element(width)), src)
    const = lambda i, j: (0, 0)
    return pl.pallas_call(
        _out_frames_kernel,
        grid=(b, (n - BLK) // TQ),
        in_specs=[window(half), window(half), window(d),
                  pl.BlockSpec(w.shape, const), pl.BlockSpec((1, d), const), pl.BlockSpec((1, d), const)],
        out_specs=pl.BlockSpec((1, TQ, d), lambda i, j: (i, j, 0)),
        out_shape=jax.ShapeDtypeStruct((b, n - BLK, d), F32),
        compiler_params=pltpu.CompilerParams(
            dimension_semantics=("parallel", "parallel"), vmem_limit_bytes=VMEM_LIMIT),
        name="out_proj_ln_frames",
    )(flat(ma), flat(mb), flat(h), w, g, bias)


def _out_proj(ma, mb, h, w, g, bias, tm):
    rows, d = h.shape
    half = ma.shape[-1]
    row = lambda i: (i, 0)
    const = lambda i: (0, 0)
    return pl.pallas_call(
        _out_kernel,
        grid=(rows // tm,),
        in_specs=[pl.BlockSpec((tm, half), row), pl.BlockSpec((tm, half), row),
                  pl.BlockSpec((tm, d), row),
                  pl.BlockSpec(w.shape, const), pl.BlockSpec((1, d), const), pl.BlockSpec((1, d), const)],
        out_specs=pl.BlockSpec((tm, d), row),
        out_shape=jax.ShapeDtypeStruct((rows, d), F32),
        compiler_params=pltpu.CompilerParams(
            dimension_semantics=("parallel",), vmem_limit_bytes=VMEM_LIMIT),
        name="out_proj_ln",
    )(ma, mb, h, w, g, bias)


def _by_parity(even, odd):
    is_odd = (jnp.arange(N_HEADS) % 2 == 1)[:, None]
    return jnp.where(is_odd, odd, even).reshape(even.shape[:-2] + (N_HEADS * LANES,))


def _mla_head_blocks(nope, rot):
    spare = jnp.zeros(rot.shape[:-1] + (HALF - MLA_ROPE,), rot.dtype)
    return _by_parity(jnp.concatenate([rot, spare, nope], axis=-1),
                      jnp.concatenate([nope, rot, spare], axis=-1))


def _even_weights(w_in, g_cq, g_ckv, w_uq, w_ukv):
    a = N_HEADS * HEAD_DIM
    wq, wk, wv = w_in[:, :a], w_in[:, a:2 * a], w_in[:, 2 * a:3 * a]
    o = 3 * a
    w_c = w_in[:, o:o + 2 * MLA_RANK]
    o += 2 * MLA_RANK
    w_kr = w_in[:, o:o + MLA_ROPE]
    o += MLA_ROPE
    w_g = w_in[:, o:]
    w_sb = jnp.concatenate([wq * HEAD_DIM ** -0.5, wk, wv], axis=1)
    hr = MLA_ROPE // 2
    zpad = jnp.zeros((w_in.shape[0], LANES - MLA_ROPE), w_in.dtype)
    w_kr2 = jnp.concatenate([w_kr, zpad, w_kr[:, hr:], w_kr[:, :hr], zpad], axis=1)

    uq = w_uq.reshape(MLA_RANK, N_HEADS, MLA_NOPE + MLA_ROPE)
    ukv = w_ukv.reshape(MLA_RANK, N_HEADS, MLA_NOPE + HEAD_DIM)
    nope, r1, r2 = uq[..., :MLA_NOPE], uq[..., MLA_NOPE:MLA_NOPE + hr], uq[..., MLA_NOPE + hr:]
    w_a = _mla_head_blocks(nope, jnp.concatenate([r1, r2], axis=-1))
    w_b = _mla_head_blocks(jnp.zeros_like(nope), jnp.concatenate([r2, r1], axis=-1))
    k_nope, v = ukv[..., :MLA_NOPE], ukv[..., MLA_NOPE:]
    w_kv = _by_parity(jnp.concatenate([v, k_nope], axis=-1), ukv)
    bf = lambda t: t.astype(BF16)
    return (bf(w_sb), bf(w_c), bf(w_kr2), bf(w_g), g_cq[None, :], g_ckv[None, :],
            bf(w_a), bf(w_b), bf(w_kv))


def _rope_tables(n):
    hr = MLA_ROPE // 2
    pos = jnp.arange(n, dtype=F32) - LEAD
    inv = ROPE_BASE ** (-jnp.arange(hr, dtype=F32) / hr)
    ang = pos[:, None] * inv[None, :]
    cos, sin = jnp.cos(ang), jnp.sin(ang)
    ones = jnp.ones((n, MLA_NOPE), F32)
    scale = (MLA_NOPE + MLA_ROPE) ** -0.5
    c_rot, s_rot = jnp.concatenate([cos, cos], axis=1), jnp.concatenate([-sin, sin], axis=1)
    spare = jnp.zeros((n, HALF - MLA_ROPE), F32)
    c_q = jnp.concatenate([c_rot, spare, ones, ones, c_rot, spare], axis=1) * scale
    s_q = jnp.concatenate([s_rot, spare, 0.0 * ones, 0.0 * ones, s_rot, spare], axis=1) * scale
    z96 = jnp.zeros((n, LANES - MLA_ROPE), F32)
    c_k = jnp.concatenate([c_rot, z96], axis=1)
    s_k = jnp.concatenate([s_rot, z96], axis=1)
    return c_q, s_q, c_k, s_k


def _odd_weights(w_in, b_forget):
    a = N_HEADS * HEAD_DIM
    kvw = SWA_KV_HEADS * HEAD_DIM
    wqc = w_in[:, :a]
    o = a
    wkc = w_in[:, o:o + kvw]
    o += kvw
    wvc = w_in[:, o:o + kvw]
    o += kvw
    wqd, wkd, wvd = w_in[:, o:o + a], w_in[:, o + a:o + 2 * a], w_in[:, o + 2 * a:o + 3 * a]
    o += 3 * a
    wfz = w_in[:, o:o + N_HEADS]
    o += N_HEADS
    w_g = w_in[:, o:]
    scale = HEAD_DIM ** -0.5
    v0, v1 = wvc[:, :HEAD_DIM], wvc[:, HEAD_DIM:]
    z = jnp.zeros_like(v0)
    w_vc = jnp.concatenate([v0, z, z, v0, v1, z, z, v1], axis=1)
    w_fz = jnp.concatenate([wfz, jnp.zeros((w_in.shape[0], LANES - N_HEADS), w_in.dtype)], axis=1)
    b_f = jnp.concatenate([b_forget, jnp.zeros((LANES - N_HEADS,), b_forget.dtype)])[None, :]
    p_f = np.zeros((3 * LANES, N_HEADS * LANES), np.float32)
    for t in range(3):
        for h in range(N_HEADS):
            p_f[t * LANES + h, h * LANES + (0 if h % 2 else HALF) + t] = 1.0
    bf = lambda t: t.astype(BF16)
    return (bf(wqc * scale), bf(wkc), bf(w_vc), bf(wqd * scale), bf(wkd), bf(wvd), bf(w_fz), bf(w_g),
            b_f.astype(F32), jnp.asarray(p_f, BF16))


def _row_tile(nb, cap):
    g = max(t for t in range(1, cap + 1) if nb % t == 0)
    return g * BLK


def kernel(x, meta_tokens, w_in_even, g_cq, g_ckv, w_uq, w_ukv, w_out_even,
           w_in_odd, b_forget, sink_logits, w_out_odd, ln_gain, ln_bias):
    b, s, d = x.shape
    assert s % SB_TQ == 0 and s % TQ == 0 and d == D_MODEL
    n = BLK + s
    nb = n // BLK
    h = jnp.concatenate([jnp.zeros((b, LEAD, d), x.dtype),
                         jnp.broadcast_to(meta_tokens[None].astype(x.dtype), (b, N_META, d)), x], axis=1)
    tabs = _rope_tables(n)
    tm_proj = _row_tile(nb, 3)
    tm_out = _row_tile(b * nb, 4)
    for layer in range(DEPTH):
        i = layer // 2
        if layer % 2 == 0:
            wts = _even_weights(w_in_even[i], g_cq[i], g_ckv[i], w_uq[i], w_ukv[i])
            qs, kst, vs, qm, kmt, vm, gate = _even_proj(h, wts, tabs, tm_proj)
            mix_a = _sb_attention(qs, kst, vs, gate)
            mix_b = _softmax_attention(qm, kmt, vm, gate, "chunk")
            w_out = w_out_even[i]
        else:
            wts = _odd_weights(w_in_odd[i], b_forget[i])
            qc, kct, vc, qf, kft, vf, gate = _odd_proj(h, wts, tm_proj)
            mix_a = _swa_attention(sink_logits[i].astype(F32), qc, kct, vc, gate)
            mix_b = _softmax_attention(qf, kft, vf, gate, "causal")
            w_out = w_out_odd[i]
        ln = (w_out.astype(BF16), ln_gain[layer][None, :], ln_bias[layer][None, :])
        if layer == DEPTH - 1:
            return _out_proj_frames(mix_a, mix_b, h, *ln)
        rows = b * n
        h = _out_proj(mix_a.reshape(rows, -1), mix_b.reshape(rows, -1), h.reshape(rows, d),
                      *ln, tm_out).reshape(b, n, d)
```

```python
import functools

import jax
import jax.numpy as jnp
import numpy as np
from jax import lax
from jax.experimental import pallas as pl
from jax.experimental.pallas import tpu as pltpu

D_MODEL = 1024
CHUNK = 64
N_META = 16
HEAD_DIM = 64
N_HEADS = 8
MLA_NOPE = 64
MLA_ROPE = 32
MLA_RANK = 256
SWA_KV_HEADS = 2
WINDOW_CHUNKS = 2
ROPE_BASE = 10000.0
DEPTH = 4
DN_ALPHA = (2 * DEPTH) ** 0.25
LN_EPS = 1e-5
RMS_EPS = 1e-6
NEG = -1e30
NO_KEY = 1 << 30
SIGN_BIT = -(1 << 31)

LANES = 128
BLK = 128
TB = 4
TQ = TB * BLK
SB_TB = 8
SB_TQ = SB_TB * BLK
SB_UNROLL = 4
SWA_UNROLL = 3
LEAD = BLK - N_META
HALF = LANES // 2
VMEM_LIMIT = 56 * 1024 * 1024

F32 = jnp.float32
BF16 = jnp.bfloat16


def _dot(a, b):
    return jnp.dot(a, b, preferred_element_type=F32)


def _chunk_of(pos):
    return jnp.maximum((pos >> 6) - 1, 0)


def _log_sigmoid(x):
    return jnp.minimum(x, 0.0) - jnp.log(1.0 + jnp.exp(-jnp.abs(x)))


def _al(x):
    return x if isinstance(x, int) else pl.multiple_of(x, BLK)


def _gated(mixed, gate):
    return (mixed * (gate * jax.nn.sigmoid(gate))).astype(BF16)


def _iotas(rows):
    return (lax.broadcasted_iota(jnp.int32, (rows, BLK), 0),
            lax.broadcasted_iota(jnp.int32, (rows, BLK), 1))


def _sb_kernel(q_ref, kt_ref, v_ref, g_ref, o_ref, u_ref, carry_ref, acc_ref, *, nt):
    r2 = lax.broadcasted_iota(jnp.int32, (2 * BLK, 2 * BLK), 0)
    c2 = lax.broadcasted_iota(jnp.int32, (2 * BLK, 2 * BLK), 1)
    u_ref[...] = jnp.where(((r2 >= BLK) == (c2 >= BLK)) & ((r2 & (BLK - 1)) > (c2 & (BLK - 1))),
                           1.0, 0.0).astype(BF16)
    zero_kt = jnp.zeros((HALF, BLK), BF16)

    def step(row0, off, rows, jb, mask=None):
        r0 = _al(row0 + off)
        k0 = _al(jb * BLK)
        sl = slice(off, off + rows)
        ktb = kt_ref[0, :, pl.ds(k0, BLK)]
        kbd = jnp.concatenate([jnp.concatenate([ktb[:HALF], zero_kt], axis=1),
                               jnp.concatenate([zero_kt, ktb[HALF:]], axis=1)], axis=0)
        z2 = _dot(q_ref[0, pl.ds(r0, rows), :], kbd)
        vst = jnp.concatenate([v_ref[0, pl.ds(k0, BLK), :LANES],
                               v_ref[0, pl.ds(k0, BLK), LANES:]], axis=0)
        if mask == "meta_keys":
            parts = [(0, rows, _iotas(rows)[1] >= LEAD)]
        elif mask is not None:
            rowi, lane = _iotas(BLK)
            local = lane < rowi
            parts = [(0, BLK, local if mask == "diagonal" else local & (lane >= LEAD))]
            if rows > BLK:
                parts.append((BLK, rows, None))

        def masked(x):
            if mask is None:
                return x
            return jnp.concatenate([x[a:b] if m is None else jnp.where(m, x[a:b], 0.0)
                                    for a, b, m in parts], axis=0)

        stays, log_betas = [], []
        for h in range(2):
            z = z2[:, h * BLK:(h + 1) * BLK]
            neg_abs = lax.bitcast_convert_type(
                lax.bitcast_convert_type(z, jnp.int32) | SIGN_BIT, F32)
            stay = jnp.maximum(z, 0.0) + jnp.log(1.0 + jnp.exp(neg_abs))
            log_betas.append(z - stay)
            stays.append(masked(stay))
        cs = _dot(jnp.concatenate([s.astype(BF16) for s in stays], axis=1), u_ref[...])
        ws = []
        for h in range(2):
            carry = carry_ref[h, sl, :]
            w = masked(jnp.exp(log_betas[h] - (cs[:, h * BLK:(h + 1) * BLK] + carry)))
            carry_ref[h, sl, :] = carry + jnp.sum(stays[h], axis=1, keepdims=True)
            ws.append(w.astype(BF16))
        acc_ref[sl, :] += _dot(jnp.concatenate(ws, axis=1), vst)

    def reset(rows):
        carry_ref[:, :rows, :] = jnp.zeros((2, rows, BLK), F32)
        acc_ref[:rows, :] = jnp.zeros((rows, BLK), F32)

    reset(BLK)
    step(0, 0, BLK, 0, "meta_tile")
    o_ref[0, :BLK, :] = _gated(acc_ref[:BLK, :], g_ref[0, :BLK, :])

    def tile(t, _):
        row0 = BLK + t * SB_TQ
        jb0 = 1 + t * SB_TB
        reset(SB_TQ)
        for d in reversed(range(SB_TB)):
            step(row0, d * BLK, SB_TQ - d * BLK, jb0 + d, "diagonal")

        def body(i, c):
            for u in range(SB_UNROLL):
                step(row0, 0, SB_TQ, jb0 - 1 - (i * SB_UNROLL + u))
            return c

        lax.fori_loop(0, t * (SB_TB // SB_UNROLL), body, 0)
        step(row0, 0, SB_TQ, 0, "meta_keys")
        o_ref[0, pl.ds(_al(row0), SB_TQ), :] = _gated(acc_ref[...], g_ref[0, pl.ds(_al(row0), SB_TQ), :])
        return 0

    lax.fori_loop(0, nt, tile, 0)


def _sb_attention(q, kt, v, gate):
    b, n, w = q.shape
    nt = (n - BLK) // SB_TQ
    return pl.pallas_call(
        functools.partial(_sb_kernel, nt=nt),
        grid=(b, w // LANES),
        in_specs=[pl.BlockSpec((1, n, LANES), lambda i, j: (i, 0, j)),
                  pl.BlockSpec((1, LANES, n), lambda i, j: (i, j, 0)),
                  pl.BlockSpec((1, n, 2 * LANES), lambda i, j: (i, 0, j)),
                  pl.BlockSpec((1, n, LANES), lambda i, j: (i, 0, j))],
        out_specs=pl.BlockSpec((1, n, LANES), lambda i, j: (i, 0, j)),
        out_shape=jax.ShapeDtypeStruct((b, n, w), BF16),
        scratch_shapes=[pltpu.VMEM((2 * BLK, 2 * BLK), BF16),
                        pltpu.VMEM((2, SB_TQ, BLK), F32),
                        pltpu.VMEM((SB_TQ, BLK), F32)],
        compiler_params=pltpu.CompilerParams(
            dimension_semantics=("parallel", "parallel"), vmem_limit_bytes=VMEM_LIMIT),
        name="sb_attention",
    )(q, kt, v, gate)


def _softmax_kernel(q_ref, kt_ref, v_ref, g_ref, o_ref, s_ref, sd_ref, mx_ref, mb_ref, acc_ref, bias_ref,
                    *, nt, mode):
    def visible(kpos, qpos, lo):
        if mode == "chunk":
            return (kpos >= lo) & (_chunk_of(kpos) <= _chunk_of(qpos))
        return (kpos >= lo) & (kpos <= qpos)

    ql = BLK + lax.broadcasted_iota(jnp.int32, (TQ, TQ), 0)
    kl = BLK + lax.broadcasted_iota(jnp.int32, (TQ, TQ), 1)
    bias_ref[...] = jnp.where(visible(kl, ql, 0), 0.0, NEG)

    def chunk_loop(n_chunks, chunk_fn):
        def quad(i, carry):
            for u in range(4):
                chunk_fn(4 * i + u)
            return carry

        lax.fori_loop(0, n_chunks >> 2, quad, 0)
        done = (n_chunks >> 2) << 2

        @pl.when((n_chunks & 2) != 0)
        def _():
            chunk_fn(done)
            chunk_fn(done + 1)

        @pl.when((n_chunks & 1) != 0)
        def _():
            chunk_fn(n_chunks - 1)

    def emit(row0, rows, n_chunks, frames):
        frs = [slice(h * LANES, (h + 1) * LANES) for h in range(2)]
        r0 = _al(row0)
        rowi, lane = _iotas(rows)

        def chunk_scores(c, h):
            k0 = _al(BLK + c * TQ)
            return _dot(q_ref[0, pl.ds(r0, rows), frs[h]], kt_ref[0, frs[h], pl.ds(k0, TQ)])

        def fold_max(c, h, s):
            s_ref[h, :, pl.ds(_al(BLK + c * TQ), TQ)] = s
            m4 = jnp.maximum(jnp.maximum(s[:, :BLK], s[:, BLK:2 * BLK]),
                             jnp.maximum(s[:, 2 * BLK:3 * BLK], s[:, 3 * BLK:]))
            mx_ref[h] = jnp.maximum(mx_ref[h], m4)

        for h, fr in enumerate(frs):
            s0 = _dot(q_ref[0, pl.ds(r0, rows), fr], kt_ref[0, fr, :BLK])
            s0 = jnp.where(visible(lane, r0 + rowi, LEAD), s0, NEG)
            sd_ref[h, :rows, :] = s0
            mx_ref[h, :rows, :] = s0
        if frames:
            for h in range(2):
                fold_max(n_chunks, h, chunk_scores(n_chunks, h) + bias_ref[...])

            def p1(c):
                for h in range(2):
                    fold_max(c, h, chunk_scores(c, h))

            chunk_loop(n_chunks, p1)

        for h, fr in enumerate(frs):
            m_row = jnp.max(mx_ref[h, :rows, :], axis=1, keepdims=True)
            mb_ref[h, :rows, :] = jnp.broadcast_to(m_row, (rows, BLK))
            p0 = jnp.exp(sd_ref[h, :rows, :] - mb_ref[h, :rows, :]).astype(BF16)
            acc_ref[h, :rows, :] = _dot(p0, v_ref[0, :BLK, fr])

        if frames:
            def p2(c):
                k0 = _al(BLK + c * TQ)
                for h, fr in enumerate(frs):
                    mb = mb_ref[h]
                    p = jnp.exp(s_ref[h, :, pl.ds(k0, TQ)]
                                - jnp.concatenate([mb] * TB, axis=1)).astype(BF16)
                    acc_ref[h] += _dot(p, v_ref[0, pl.ds(k0, TQ), fr])

            chunk_loop(n_chunks + 1, p2)

        outs = []
        for h in range(2):
            acc = acc_ref[h, :rows, :]
            outs.append(acc / pltpu.roll(acc, HALF, 1))
        o_ref[0, pl.ds(r0, rows), :] = _gated(jnp.where(lane < HALF, outs[0], outs[1]),
                                              g_ref[0, pl.ds(r0, rows), :])

    emit(0, BLK, 0, False)

    def tile(t, _):
        emit(BLK + t * TQ, TQ, t, True)
        return 0

    lax.fori_loop(0, nt, tile, 0)


def _softmax_attention(q, kt, v, gate, mode):
    b, n, w = q.shape
    nt = (n - BLK) // TQ
    pairs = w // (2 * LANES)
    return pl.pallas_call(
        functools.partial(_softmax_kernel, nt=nt, mode=mode),
        grid=(b, pairs),
        in_specs=[pl.BlockSpec((1, n, 2 * LANES), lambda i, j: (i, 0, j)),
                  pl.BlockSpec((1, 2 * LANES, n), lambda i, j: (i, j, 0)),
                  pl.BlockSpec((1, n, 2 * LANES), lambda i, j: (i, 0, j)),
                  pl.BlockSpec((1, n, LANES), lambda i, j: (i, 0, pairs + j))],
        out_specs=pl.BlockSpec((1, n, LANES), lambda i, j: (i, 0, j)),
        out_shape=jax.ShapeDtypeStruct((b, n, w // 2), BF16),
        scratch_shapes=[pltpu.VMEM((2, TQ, n), F32),
                        pltpu.VMEM((2, TQ, BLK), F32),
                        pltpu.VMEM((2, TQ, BLK), F32),
                        pltpu.VMEM((2, TQ, BLK), F32),
                        pltpu.VMEM((2, TQ, BLK), F32),
                        pltpu.VMEM((TQ, TQ), F32)],
        compiler_params=pltpu.CompilerParams(
            dimension_semantics=("parallel", "parallel"), vmem_limit_bytes=VMEM_LIMIT),
        name="softmax_attention_" + mode,
    )(q, kt, v, gate)


def _swa_kernel(sink_ref, q_ref, kt_ref, v_ref, g_ref, o_ref, *, nb):
    rowi, lane = _iotas(BLK)
    group = N_HEADS // SWA_KV_HEADS
    zero_kt = jnp.zeros((HALF, BLK), BF16)
    r2 = lax.broadcasted_iota(jnp.int32, (2 * BLK, BLK), 0)
    c2 = lax.broadcasted_iota(jnp.int32, (2 * BLK, BLK), 1)
    ones_st = jnp.where((r2 >= BLK) == (c2 >= HALF), 1.0, 0.0).astype(BF16)

    def q_block(m):
        r0 = _al(m * BLK)
        qpos = r0 + rowi
        cq = _chunk_of(qpos)
        slots = ((0, jnp.where(m >= 1, LEAD, NO_KEY)),
                 (jnp.maximum(m - 1, 0), jnp.where(m >= 2, 0, NO_KEY)),
                 (m, LEAD))
        biases = []
        for jb, lo in slots:
            kpos = jb * BLK + lane
            ck = _chunk_of(kpos)
            msk = (kpos >= lo) & (ck <= cq) & ((cq - ck <= WINDOW_CHUNKS) | (kpos < BLK))
            biases.append((msk, jnp.abs(qpos - kpos).astype(F32)))
        for g in range(SWA_KV_HEADS):
            kbds, wcats = [], []
            for jb, _ in slots:
                k0 = _al(jb * BLK)
                ktg =kt_ref[0, g * HALF:(g + 1) * HALF, pl.ds(k0, BLK)]
                kbds.append(jnp.concatenate([jnp.concatenate([ktg, zero_kt], axis=1),
                                             jnp.concatenate([zero_kt, ktg], axis=1)], axis=0))
                vst = jnp.concatenate([v_ref[0, pl.ds(k0, BLK), (2 * g) * LANES:(2 * g + 1) * LANES],
                                       v_ref[0, pl.ds(k0, BLK), (2 * g + 1) * LANES:(2 * g + 2) * LANES]],
                                      axis=0)
                wcats.append(jnp.concatenate([vst, ones_st], axis=1))
            for pr in range(group // 2):
                pair = g * (group // 2) + pr
                q2 = q_ref[0, pl.ds(r0, BLK), pair * LANES:(pair + 1) * LANES]
                zs = [_dot(q2, kbd) for kbd in kbds]
                ps, sink_terms = [[], []], []
                for h in range(2):
                    head = 2 * pair + h
                    slope = 2.0 ** (-8.0 * (head + 1) / N_HEADS)
                    ss = [jnp.where(msk, z[:, h * BLK:(h + 1) * BLK] - slope * dist, NEG)
                          for z, (msk, dist) in zip(zs, biases)]
                    sink = sink_ref[head]
                    mx = jnp.maximum(jnp.maximum(ss[0], ss[1]), ss[2])
                    m_row = jnp.maximum(jnp.max(mx, axis=1, keepdims=True), sink)
                    ps[h] = [jnp.exp(s - m_row).astype(BF16) for s in ss]
                    sink_terms.append(jnp.exp(sink - m_row))
                res = jnp.zeros((BLK, 2 * BLK), F32)
                for i in range(len(slots)):
                    res = res + _dot(jnp.concatenate([ps[0][i], ps[1][i]], axis=1), wcats[i])
                denom = res[:, BLK:] + jnp.where(lane < HALF, sink_terms[0], sink_terms[1])
                cols = slice(pair * LANES, (pair + 1) * LANES)
                o_ref[0, pl.ds(r0, BLK), cols] = _gated(res[:, :BLK] / denom, g_ref[0, pl.ds(r0, BLK), cols])

    def q_blocks(i, carry):
        for u in range(SWA_UNROLL):
            q_block(SWA_UNROLL * i + u)
        return carry

    lax.fori_loop(0, nb // SWA_UNROLL, q_blocks, 0)
    for m in range(nb - nb % SWA_UNROLL, nb):
        q_block(m)


def _swa_attention(sinks, q, kt, v, gate):
    b, n, w = q.shape
    nb = n // BLK
    return pl.pallas_call(
        functools.partial(_swa_kernel, nb=nb),
        grid=(b,),
        in_specs=[
            pl.BlockSpec(memory_space=pltpu.SMEM),
            pl.BlockSpec((1, n, w), lambda i: (i, 0, 0)),
            pl.BlockSpec((1, kt.shape[1], n), lambda i: (i, 0, 0)),
            pl.BlockSpec((1, n, v.shape[2]), lambda i: (i, 0, 0)),
            pl.BlockSpec((1, n, w), lambda i: (i, 0, 0)),
        ],
        out_specs=pl.BlockSpec((1, n, w), lambda i: (i, 0, 0)),
        out_shape=jax.ShapeDtypeStruct((b, n, w), BF16),
        compiler_params=pltpu.CompilerParams(
            dimension_semantics=("parallel",), vmem_limit_bytes=VMEM_LIMIT),
        name="swa_attention",
    )(sinks, q, kt, v, gate)


def _spread_heads(x, fill):
    lane = lax.broadcasted_iota(jnp.int32, (x.shape[0], LANES), 1)
    blocks = []
    for p in range(x.shape[1] // LANES):
        xp = x[:, p * LANES:(p + 1) * LANES]
        for odd in range(2):
            blk = 2 * p + odd
            if isinstance(fill, float):
                f = fill
            elif callable(fill):
                f = fill(lane)
            else:
                f = fill[:, blk * LANES:(blk + 1) * LANES]
            blocks.append(jnp.where((lane >= HALF) if odd else (lane < HALF), xp, f))
    return jnp.concatenate(blocks, axis=1)


def _even_proj_kernel(x_ref, wsb_ref, wc_ref, wkr_ref, wg_ref, gq_ref, gkv_ref,
                      wa_ref, wb_ref, wkv_ref, cq_ref, sq_ref, ck_ref, sk_ref,
                      qs_ref, kst_ref, vs_ref, qm_ref, kmt_ref, vm_ref, gate_ref):
    xb = x_ref[0].astype(BF16)
    w = N_HEADS * HEAD_DIM
    sb = _dot(xb, wsb_ref[...])
    qs_ref[0] = sb[:, :w].astype(BF16)
    kst_ref[0] = sb[:, w:2 * w].T.astype(BF16)
    vs_ref[0] = _spread_heads(sb[:, 2 * w:], 0.0).astype(BF16)
    gate_ref[0] = _dot(xb, wg_ref[...])

    c = _dot(xb, wc_ref[...])

    def rms(t, g):
        return (t * lax.rsqrt(jnp.mean(t * t, axis=-1, keepdims=True) + RMS_EPS) * g).astype(BF16)

    xq = rms(c[:, :MLA_RANK], gq_ref[...])
    xkv = rms(c[:, MLA_RANK:], gkv_ref[...])
    pairs = N_HEADS // 2
    cq = jnp.concatenate([cq_ref[...]] * pairs, axis=1)
    sq = jnp.concatenate([sq_ref[...]] * pairs, axis=1)
    qm_ref[0] = (_dot(xq, wa_ref[...]) * cq + _dot(xq, wb_ref[...]) * sq).astype(BF16)
    kr = _dot(xb, wkr_ref[...])
    krr = kr[:, :LANES] * ck_ref[...] + kr[:, LANES:] * sk_ref[...]
    krr = (krr, pltpu.roll(krr, HALF, 1))
    kv = _dot(xkv, wkv_ref[...])
    lane = lax.broadcasted_iota(jnp.int32, (kv.shape[0], LANES), 1)
    kblocks, vblocks = [], []
    for hd in range(N_HEADS):
        blk = slice(hd * LANES, (hd + 1) * LANES)
        v_half = (lane >= HALF) if hd % 2 else (lane < HALF)
        kblocks.append(jnp.where(v_half, krr[hd % 2], kv[:, blk]))
        vblocks.append(jnp.where(v_half, kv[:, blk], 1.0))
    kmt_ref[0] = jnp.concatenate(kblocks, axis=1).T.astype(BF16)
    vm_ref[0] = jnp.concatenate(vblocks, axis=1).astype(BF16)


def _even_proj(h, wts, tabs, tm):
    b, n, d = h.shape
    grid = (n // tm, b)
    row = lambda i, j: (j, i, 0)
    col = lambda i, j: (j, 0, i)
    const = lambda i, j: (0, 0)
    tab = lambda i, j: (i, 0)
    full = lambda a: pl.BlockSpec(a.shape, const)
    wide = N_HEADS * LANES
    narrow = N_HEADS * HEAD_DIM
    outs = [(narrow, BF16, False), (narrow, BF16, True), (wide, BF16, False),
            (wide, BF16, False), (wide, BF16, True), (wide, BF16, False), (d, F32, False)]
    return pl.pallas_call(
        _even_proj_kernel,
        grid=grid,
        in_specs=[pl.BlockSpec((1, tm, d), row)] + [full(a) for a in wts] + [
            pl.BlockSpec((tm, 2 * LANES), tab), pl.BlockSpec((tm, 2 * LANES), tab),
            pl.BlockSpec((tm, LANES), tab), pl.BlockSpec((tm, LANES), tab)],
        out_specs=[pl.BlockSpec((1, c, tm), col) if tr else pl.BlockSpec((1, tm, c), row)
                   for c, _, tr in outs],
        out_shape=[jax.ShapeDtypeStruct((b, c, n) if tr else (b, n, c), t) for c, t, tr in outs],
        compiler_params=pltpu.CompilerParams(
            dimension_semantics=("parallel", "parallel"), vmem_limit_bytes=VMEM_LIMIT),
        name="even_proj",
    )(h, *wts, *tabs)


def _odd_proj_kernel(x_ref, wqc_ref, wkc_ref, wvc_ref, wqf_ref, wkf_ref, wvf_ref, wfz_ref, wg_ref,
                     bf_ref, pf_ref,
                     qc_ref, kct_ref, vc_ref, qf_ref, kft_ref, vf_ref, gate_ref, carry_ref, *, tm):
    i = pl.program_id(1)
    xb = x_ref[0].astype(BF16)
    qc_ref[0] = _dot(xb, wqc_ref[...]).astype(BF16)
    kct_ref[0] = _dot(xb, wkc_ref[...]).T.astype(BF16)
    vc_ref[0] = _dot(xb, wvc_ref[...]).astype(BF16)
    gate_ref[0] = _dot(xb, wg_ref[...])

    vf_ref[0] = _spread_heads(_dot(xb, wvf_ref[...]), 1.0).astype(BF16)
    qf_ref[0] = _spread_heads(_dot(xb, wqf_ref[...]),
                              lambda lane: jnp.where((lane & (HALF - 1)) < 3, 1.0, 0.0)).astype(BF16)

    fz = _dot(xb, wfz_ref[...]) + bf_ref[...]
    r = lax.broadcasted_iota(jnp.int32, (tm, LANES), 0) + i * tm
    c = lax.broadcasted_iota(jnp.int32, (tm, LANES), 1)
    lf = jnp.where((c < N_HEADS) & (r >= LEAD), _log_sigmoid(fz), 0.0)

    def split3(t):
        a = t.astype(BF16)
        rem = t - a.astype(F32)
        bb = rem.astype(BF16)
        cc = (rem - bb.astype(F32)).astype(BF16)
        return jnp.concatenate([a, bb, cc], axis=1)

    rr = lax.broadcasted_iota(jnp.int32, (tm, tm), 0)
    cc = lax.broadcasted_iota(jnp.int32, (tm, tm), 1)
    tri = jnp.where(cc <= rr, 1.0, 0.0).astype(BF16)

    @pl.when(i == 0)
    def _():
        carry_ref[...] = jnp.zeros_like(carry_ref)

    c3 = _dot(tri, split3(lf))
    cum = c3[:, :LANES] + c3[:, LANES:2 * LANES] + c3[:, 2 * LANES:] + carry_ref[0:1, :]
    carry_ref[...] = jnp.broadcast_to(cum[tm - 1:tm, :], carry_ref.shape)
    kft_ref[0] = _spread_heads(_dot(xb, wkf_ref[...]), _dot(split3(-cum), pf_ref[...])).T.astype(BF16)


def _odd_proj(h, wts, tm):
    b, n, d = h.shape
    row = lambda j, i: (j, i, 0)
    col = lambda j, i: (j, 0, i)
    const = lambda j, i: (0, 0)
    full = lambda a: pl.BlockSpec(a.shape, const)
    wide = N_HEADS * LANES
    narrow = N_HEADS * HEAD_DIM
    outs = [(narrow, BF16, False), (SWA_KV_HEADS * HEAD_DIM, BF16, True), (narrow, BF16, False),
            (wide, BF16, False), (wide, BF16, True), (wide, BF16, False), (d, F32, False)]
    return pl.pallas_call(
        functools.partial(_odd_proj_kernel, tm=tm),
        grid=(b, n // tm),
        in_specs=[pl.BlockSpec((1, tm, d), row)] + [full(a) for a in wts],
        out_specs=[pl.BlockSpec((1, c, tm), col) if tr else pl.BlockSpec((1, tm, c), row)
                   for c, _, tr in outs],
        out_shape=[jax.ShapeDtypeStruct((b, c, n) if tr else (b, n, c), t) for c, t, tr in outs],
        scratch_shapes=[pltpu.VMEM((8, LANES), F32)],
        compiler_params=pltpu.CompilerParams(
            dimension_semantics=("parallel", "arbitrary"), vmem_limit_bytes=VMEM_LIMIT),
        name="odd_proj",
    )(h, *wts)


def _out_math(ma, mb, h, w_ref, g, b):
    half = ma.shape[-1]
    y = _dot(ma, w_ref[:half, :]) + _dot(mb, w_ref[half:, :])
    r = DN_ALPHA * h + y
    mu = jnp.mean(r, axis=-1, keepdims=True)
    cen = r - mu
    var = jnp.mean(cen * cen, axis=-1, keepdims=True)
    return cen * lax.rsqrt(var + LN_EPS) * g + b


def _out_kernel(ma_ref, mb_ref, h_ref, w_ref, g_ref, b_ref, o_ref):
    o_ref[...] = _out_math(ma_ref[...], mb_ref[...], h_ref[...], w_ref, g_ref[...], b_ref[...])


def _out_frames_kernel(ma_ref, mb_ref, h_ref, w_ref, g_ref, b_ref, o_ref):
    o_ref[0] = _out_math(ma_ref[...], mb_ref[...], h_ref[...], w_ref, g_ref[...], b_ref[...])


def _out_proj_frames(ma, mb, h, w, g, bias):
    b, n, d = h.shape
    half = ma.shape[-1]
    flat = lambda a: a.reshape(b * n, a.shape[-1])
    src = lambda i, j: (pl.multiple_of(i * n + BLK + j * TQ, BLK), 0)
    window = lambda width: pl.BlockSpec((pl.Element(TQ), pl.Element(width)), src)
    const = lambda i, j: (0, 0)
    return pl.pallas_call(
        _out_frames_kernel,
        grid=(b, (n - BLK) // TQ),
        in_specs=[window(half), window(half), window(d),
                  pl.BlockSpec(w.shape, const), pl.BlockSpec((1, d), const), pl.BlockSpec((1, d), const)],
        out_specs=pl.BlockSpec((1, TQ, d), lambda i, j: (i, j, 0)),
        out_shape=jax.ShapeDtypeStruct((b, n - BLK, d), F32),
        compiler_params=pltpu.CompilerParams(
            dimension_semantics=("parallel", "parallel"), vmem_limit_bytes=VMEM_LIMIT),
        name="out_proj_ln_frames",
    )(flat(ma), flat(mb), flat(h), w, g, bias)


def _out_proj(ma, mb, h, w, g, bias, tm):
    rows, d = h.shape
    half = ma.shape[-1]
    row = lambda i: (i, 0)
    const = lambda i: (0, 0)
    return pl.pallas_call(
        _out_kernel,
        grid=(rows // tm,),
        in_specs=[pl.BlockSpec((tm, half), row), pl.BlockSpec((tm, half), row),
                  pl.BlockSpec((tm, d), row),
                  pl.BlockSpec(w.shape, const), pl.BlockSpec((1, d), const), pl.BlockSpec((1, d), const)],
        out_specs=pl.BlockSpec((tm, d), row),
        out_shape=jax.ShapeDtypeStruct((rows, d), F32),
        compiler_params=pltpu.CompilerParams(
            dimension_semantics=("parallel",), vmem_limit_bytes=VMEM_LIMIT),
        name="out_proj_ln",
    )(ma, mb, h, w, g, bias)


def _by_parity(even, odd):
    is_odd = (jnp.arange(N_HEADS) % 2 == 1)[:, None]
    return jnp.where(is_odd, odd, even).reshape(even.shape[:-2] + (N_HEADS * LANES,))


def _mla_head_blocks(nope, rot):
    spare = jnp.zeros(rot.shape[:-1] + (HALF - MLA_ROPE,), rot.dtype)
    return _by_parity(jnp.concatenate([rot, spare, nope], axis=-1),
                      jnp.concatenate([nope, rot, spare], axis=-1))


def _even_weights(w_in, g_cq, g_ckv, w_uq, w_ukv):
    a = N_HEADS * HEAD_DIM
    wq, wk, wv = w_in[:, :a], w_in[:, a:2 * a], w_in[:, 2 * a:3 * a]
    o = 3 * a
    w_c = w_in[:, o:o + 2 * MLA_RANK]
    o += 2 * MLA_RANK
    w_kr = w_in[:, o:o + MLA_ROPE]
    o += MLA_ROPE
    w_g = w_in[:, o:]
    w_sb = jnp.concatenate([wq * HEAD_DIM ** -0.5, wk, wv], axis=1)
    hr = MLA_ROPE // 2
    zpad = jnp.zeros((w_in.shape[0], LANES - MLA_ROPE), w_in.dtype)
    w_kr2 = jnp.concatenate([w_kr, zpad, w_kr[:, hr:], w_kr[:, :hr], zpad], axis=1)

    uq = w_uq.reshape(MLA_RANK, N_HEADS, MLA_NOPE + MLA_ROPE)
    ukv = w_ukv.reshape(MLA_RANK, N_HEADS, MLA_NOPE + HEAD_DIM)
    nope, r1, r2 = uq[..., :MLA_NOPE], uq[..., MLA_NOPE:MLA_NOPE + hr], uq[..., MLA_NOPE + hr:]
    w_a = _mla_head_blocks(nope, jnp.concatenate([r1, r2], axis=-1))
    w_b = _mla_head_blocks(jnp.zeros_like(nope), jnp.concatenate([r2, r1], axis=-1))
    k_nope, v = ukv[..., :MLA_NOPE], ukv[..., MLA_NOPE:]
    w_kv = _by_parity(jnp.concatenate([v, k_nope], axis=-1), ukv)
    bf = lambda t: t.astype(BF16)
    return (bf(w_sb), bf(w_c), bf(w_kr2), bf(w_g), g_cq[None, :], g_ckv[None, :],
            bf(w_a), bf(w_b), bf(w_kv))


def _rope_tables(n):
    hr = MLA_ROPE // 2
    pos = jnp.arange(n, dtype=F32) - LEAD
    inv = ROPE_BASE ** (-jnp.arange(hr, dtype=F32) / hr)
    ang = pos[:, None] * inv[None, :]
    cos, sin = jnp.cos(ang), jnp.sin(ang)
    ones = jnp.ones((n, MLA_NOPE), F32)
    scale = (MLA_NOPE + MLA_ROPE) ** -0.5
    c_rot, s_rot = jnp.concatenate([cos, cos], axis=1), jnp.concatenate([-sin, sin], axis=1)
    spare = jnp.zeros((n, HALF - MLA_ROPE), F32)
    c_q = jnp.concatenate([c_rot, spare, ones, ones, c_rot, spare], axis=1) * scale
    s_q = jnp.concatenate([s_rot, spare, 0.0 * ones, 0.0 * ones, s_rot, spare], axis=1) * scale
    z96 = jnp.zeros((n, LANES - MLA_ROPE), F32)
    c_k = jnp.concatenate([c_rot, z96], axis=1)
    s_k = jnp.concatenate([s_rot, z96], axis=1)
    return c_q, s_q, c_k, s_k


def _odd_weights(w_in, b_forget):
    a = N_HEADS * HEAD_DIM
    kvw = SWA_KV_HEADS * HEAD_DIM
    wqc = w_in[:, :a]
    o = a
    wkc = w_in[:, o:o + kvw]
    o += kvw
    wvc = w_in[:, o:o + kvw]
    o += kvw
    wqd, wkd, wvd = w_in[:, o:o + a], w_in[:, o + a:o + 2 * a], w_in[:, o + 2 * a:o + 3 * a]
    o += 3 * a
    wfz = w_in[:, o:o + N_HEADS]
    o += N_HEADS
    w_g = w_in[:, o:]
    scale = HEAD_DIM ** -0.5
    v0, v1 = wvc[:, :HEAD_DIM], wvc[:, HEAD_DIM:]
    z = jnp.zeros_like(v0)
    w_vc = jnp.concatenate([v0, z, z, v0, v1, z, z, v1], axis=1)
    w_fz = jnp.concatenate([wfz, jnp.zeros((w_in.shape[0], LANES - N_HEADS), w_in.dtype)], axis=1)
    b_f = jnp.concatenate([b_forget, jnp.zeros((LANES - N_HEADS,), b_forget.dtype)])[None, :]
    bf = lambda t: t.astype(BF16)
    return (bf(wqc * scale), bf(wkc), bf(w_vc), bf(wqd * scale), bf(wkd), bf(wvd), bf(w_fz), bf(w_g),
            b_f.astype(F32))


def _piece_placement():
    p_f = np.zeros((3 * LANES, N_HEADS * LANES), np.float32)
    for t in range(3):
        for h in range(N_HEADS):
            p_f[t * LANES + h, h * LANES + (0 if h % 2 else HALF) + t] = 1.0
    return jnp.asarray(p_f, BF16)


def _row_tile(nb, cap):
    g = max(t for t in range(1, cap + 1) if nb % t == 0)
    return g * BLK


def kernel(x, meta_tokens, w_in_even, g_cq, g_ckv, w_uq, w_ukv, w_out_even,
           w_in_odd, b_forget, sink_logits, w_out_odd, ln_gain, ln_bias):
    b, s, d = x.shape
    assert s % SB_TQ == 0 and s % TQ == 0 and d == D_MODEL
    n = BLK + s
    nb = n // BLK
    h = jnp.concatenate([jnp.zeros((b, LEAD, d), x.dtype),
                         jnp.broadcast_to(meta_tokens[None].astype(x.dtype), (b, N_META, d)), x], axis=1)
    tabs = _rope_tables(n)
    tm_proj = _row_tile(nb, 3)
    tm_out = _row_tile(b * nb, 4)
    even_wts = jax.vmap(_even_weights)(w_in_even, g_cq, g_ckv, w_uq, w_ukv)
    odd_wts = jax.vmap(_odd_weights)(w_in_odd, b_forget)
    p_f = _piece_placement()
    for layer in range(DEPTH):
        i = layer // 2
        if layer % 2 == 0:
            wts = tuple(t[i] for t in even_wts)
            qs, kst, vs, qm, kmt, vm, gate = _even_proj(h, wts, tabs, tm_proj)
            mix_a = _sb_attention(qs, kst, vs, gate)
            mix_b = _softmax_attention(qm, kmt, vm, gate, "chunk")
            w_out = w_out_even[i]
        else:
            wts = tuple(t[i] for t in odd_wts) + (p_f,)
            qc, kct, vc, qf, kft, vf, gate = _odd_proj(h, wts, tm_proj)
            mix_a = _swa_attention(sink_logits[i].astype(F32), qc, kct, vc, gate)
            mix_b = _softmax_attention(qf, kft, vf, gate, "causal")
            w_out = w_out_odd[i]
        ln = (w_out.astype(BF16), ln_gain[layer][None, :], ln_bias[layer][None, :])
        if layer == DEPTH - 1:
            return _out_proj_frames(mix_a, mix_b, h, *ln)
        rows = b * n
        h = _out_proj(mix_a.reshape(rows, -1), mix_b.reshape(rows, -1), h.reshape(rows, d),
                      *ln, tm_out).reshape(b, n, d)
```

```python
import functools

import jax
import jax.numpy as jnp
import numpy as np
from jax import lax
from jax.experimental import pallas as pl
from jax.experimental.pallas import tpu as pltpu

D_MODEL = 1024
CHUNK = 64
N_META = 16
HEAD_DIM = 64
N_HEADS = 8
MLA_NOPE = 64
MLA_ROPE = 32
MLA_RANK = 256
SWA_KV_HEADS = 2
WINDOW_CHUNKS = 2
ROPE_BASE = 10000.0
DEPTH = 4
DN_ALPHA = (2 * DEPTH) ** 0.25
LN_EPS = 1e-5
RMS_EPS = 1e-6
NEG = -1e30
NO_KEY = 1 << 30
SIGN_BIT = -(1 << 31)

LANES = 128
BLK = 128
TB = 4
TQ = TB * BLK
SB_TB = 8
SB_TQ = SB_TB * BLK
SB_UNROLL = 8
SWA_UNROLL = 3
LEAD = BLK - N_META
HALF = LANES // 2
VMEM_LIMIT = 56 * 1024 * 1024

F32 = jnp.float32
BF16 = jnp.bfloat16


def _dot(a, b):
    return jnp.dot(a, b, preferred_element_type=F32)


def _chunk_of(pos):
    return jnp.maximum((pos >> 6) - 1, 0)


def _log_sigmoid(x):
    return jnp.minimum(x, 0.0) - jnp.log(1.0 + jnp.exp(-jnp.abs(x)))


def _al(x):
    return x if isinstance(x, int) else pl.multiple_of(x, BLK)


def _gated(mixed, gate):
    return (mixed * (gate * jax.nn.sigmoid(gate))).astype(BF16)


def _iotas(rows):
    return (lax.broadcasted_iota(jnp.int32, (rows, BLK), 0),
            lax.broadcasted_iota(jnp.int32, (rows, BLK), 1))


def _sb_kernel(q_ref, kt_ref, v_ref, g_ref, o_ref, u_ref, carry_ref, acc_ref, *, nt):
    r2 = lax.broadcasted_iota(jnp.int32, (2 * BLK, 2 * BLK), 0)
    c2 = lax.broadcasted_iota(jnp.int32, (2 * BLK, 2 * BLK), 1)
    u_ref[...] = jnp.where(((r2 >= BLK) == (c2 >= BLK)) & ((r2 & (BLK - 1)) > (c2 & (BLK - 1))),
                           1.0, 0.0).astype(BF16)
    zero_kt = jnp.zeros((HALF, BLK), BF16)

    def step(row0, off, rows, jb, mask=None):
        r0 = _al(row0 + off)
        k0 = _al(jb * BLK)
        sl = slice(off, off + rows)
        ktb = kt_ref[0, :, pl.ds(k0, BLK)]
        kbd = jnp.concatenate([jnp.concatenate([ktb[:HALF], zero_kt], axis=1),
                               jnp.concatenate([zero_kt, ktb[HALF:]], axis=1)], axis=0)
        z2 = _dot(q_ref[0, pl.ds(r0, rows), :], kbd)
        vst = jnp.concatenate([v_ref[0, pl.ds(k0, BLK), :LANES],
                               v_ref[0, pl.ds(k0, BLK), LANES:]], axis=0)
        if mask == "meta_keys":
            parts = [(0, rows, _iotas(rows)[1] >= LEAD)]
        elif mask is not None:
            rowi, lane = _iotas(BLK)
            local = lane < rowi
            parts = [(0, BLK, local if mask == "diagonal" else local & (lane >= LEAD))]
            if rows > BLK:
                parts.append((BLK, rows, None))

        def masked(x):
            if mask is None:
                return x
            return jnp.concatenate([x[a:b] if m is None else jnp.where(m, x[a:b], 0.0)
                                    for a, b, m in parts], axis=0)

        stays, log_betas = [], []
        for h in range(2):
            z = z2[:, h * BLK:(h + 1) * BLK]
            neg_abs = lax.bitcast_convert_type(
                lax.bitcast_convert_type(z, jnp.int32) | SIGN_BIT, F32)
            stay = jnp.maximum(z, 0.0) + jnp.log(1.0 + jnp.exp(neg_abs))
            log_betas.append(z - stay)
            stays.append(masked(stay))
        cs = _dot(jnp.concatenate([s.astype(BF16) for s in stays], axis=1), u_ref[...])
        ws = []
        for h in range(2):
            carry = carry_ref[h, sl, :]
            w = masked(jnp.exp(log_betas[h] - (cs[:, h * BLK:(h + 1) * BLK] + carry)))
            carry_ref[h, sl, :] = carry + jnp.sum(stays[h], axis=1, keepdims=True)
            ws.append(w.astype(BF16))
        acc_ref[sl, :] += _dot(jnp.concatenate(ws, axis=1), vst)

    def reset(rows):
        carry_ref[:, :rows, :] = jnp.zeros((2, rows, BLK), F32)
        acc_ref[:rows, :] = jnp.zeros((rows, BLK), F32)

    reset(BLK)
    step(0, 0, BLK, 0, "meta_tile")
    o_ref[0, :BLK, :] = _gated(acc_ref[:BLK, :], g_ref[0, :BLK, :])

    def tile(t, _):
        row0 = BLK + t * SB_TQ
        jb0 = 1 + t * SB_TB
        reset(SB_TQ)
        for d in reversed(range(SB_TB)):
            step(row0, d * BLK, SB_TQ - d * BLK, jb0 + d, "diagonal")

        def body(i, c):
            for u in range(SB_UNROLL):
                step(row0, 0, SB_TQ, jb0 - 1 - (i * SB_UNROLL + u))
            return c

        lax.fori_loop(0, t * (SB_TB // SB_UNROLL), body, 0)
        step(row0, 0, SB_TQ, 0, "meta_keys")
        o_ref[0, pl.ds(_al(row0), SB_TQ), :] = _gated(acc_ref[...], g_ref[0, pl.ds(_al(row0), SB_TQ), :])
        return 0

    lax.fori_loop(0, nt, tile, 0)


def _sb_attention(q, kt, v, gate):
    b, n, w = q.shape
    nt = (n - BLK) // SB_TQ
    return pl.pallas_call(
        functools.partial(_sb_kernel, nt=nt),
        grid=(b, w // LANES),
        in_specs=[pl.BlockSpec((1, n, LANES), lambda i, j: (i, 0, j)),
                  pl.BlockSpec((1, LANES, n), lambda i, j: (i, j, 0)),
                  pl.BlockSpec((1, n, 2 * LANES), lambda i, j: (i, 0, j)),
                  pl.BlockSpec((1, n, LANES), lambda i, j: (i, 0, j))],
        out_specs=pl.BlockSpec((1, n, LANES), lambda i, j: (i, 0, j)),
        out_shape=jax.ShapeDtypeStruct((b, n, w), BF16),
        scratch_shapes=[pltpu.VMEM((2 * BLK, 2 * BLK), BF16),
                        pltpu.VMEM((2, SB_TQ, BLK), F32),
                        pltpu.VMEM((SB_TQ, BLK), F32)],
        compiler_params=pltpu.CompilerParams(
            dimension_semantics=("parallel", "parallel"), vmem_limit_bytes=VMEM_LIMIT),
        name="sb_attention",
    )(q, kt, v, gate)


def _softmax_kernel(q_ref, kt_ref, v_ref, g_ref, o_ref, s_ref, sd_ref, mx_ref, mb_ref, acc_ref, bias_ref,
                    *, nt, mode):
    def visible(kpos, qpos, lo):
        if mode == "chunk":
            return (kpos >= lo) & (_chunk_of(kpos) <= _chunk_of(qpos))
        return (kpos >= lo) & (kpos <= qpos)

    ql = BLK + lax.broadcasted_iota(jnp.int32, (TQ, TQ), 0)
    kl = BLK + lax.broadcasted_iota(jnp.int32, (TQ, TQ), 1)
    bias_ref[...] = jnp.where(visible(kl, ql, 0), 0.0, NEG)

    def chunk_loop(n_chunks, chunk_fn):
        def quad(i, carry):
            for u in range(4):
                chunk_fn(4 * i + u)
            return carry

        lax.fori_loop(0, n_chunks >> 2, quad, 0)
        done = (n_chunks >> 2) << 2

        @pl.when((n_chunks & 2) != 0)
        def _():
            chunk_fn(done)
            chunk_fn(done + 1)

        @pl.when((n_chunks & 1) != 0)
        def _():
            chunk_fn(n_chunks - 1)

    def emit(row0, rows, n_chunks, frames):
        frs = [slice(h * LANES, (h + 1) * LANES) for h in range(2)]
        r0 = _al(row0)
        rowi, lane = _iotas(rows)

        def chunk_scores(c, h):
            k0 = _al(BLK + c * TQ)
            return _dot(q_ref[0, pl.ds(r0, rows), frs[h]], kt_ref[0, frs[h], pl.ds(k0, TQ)])

        def fold_max(c, h, s):
            s_ref[h, :, pl.ds(_al(BLK + c * TQ), TQ)] = s
            m4 = jnp.maximum(jnp.maximum(s[:, :BLK], s[:, BLK:2 * BLK]),
                             jnp.maximum(s[:, 2 * BLK:3 * BLK], s[:, 3 * BLK:]))
            mx_ref[h] = jnp.maximum(mx_ref[h], m4)

        for h, fr in enumerate(frs):
            s0 = _dot(q_ref[0, pl.ds(r0, rows), fr], kt_ref[0, fr, :BLK])
            s0 = jnp.where(visible(lane, r0 + rowi, LEAD), s0, NEG)
            sd_ref[h, :rows, :] = s0
            mx_ref[h, :rows, :] = s0
        if frames:
            for h in range(2):
                fold_max(n_chunks, h, chunk_scores(n_chunks, h) + bias_ref[...])

            def p1(c):
                for h in range(2):
                    fold_max(c, h, chunk_scores(c, h))

            chunk_loop(n_chunks, p1)

        for h, fr in enumerate(frs):
            m_row = jnp.max(mx_ref[h, :rows, :], axis=1, keepdims=True)
            mb_ref[h, :rows, :] = jnp.broadcast_to(m_row, (rows, BLK))
            p0 = jnp.exp(sd_ref[h, :rows, :] - mb_ref[h, :rows, :]).astype(BF16)
            acc_ref[h, :rows, :] = _dot(p0, v_ref[0, :BLK, fr])

        if frames:
            def p2(c):
                k0 = _al(BLK + c * TQ)
                for h, fr in enumerate(frs):
                    mb = mb_ref[h]
                    p = jnp.exp(s_ref[h, :, pl.ds(k0, TQ)]
                                - jnp.concatenate([mb] * TB, axis=1)).astype(BF16)
                    acc_ref[h] += _dot(p, v_ref[0, pl.ds(k0, TQ), fr])

            chunk_loop(n_chunks + 1, p2)

        outs = []
        for h in range(2):
            acc = acc_ref[h, :rows, :]
            outs.append(acc / pltpu.roll(acc, HALF, 1))
        o_ref[0, pl.ds(r0, rows), :] = _gated(jnp.where(lane < HALF, outs[0], outs[1]),
                                              g_ref[0, pl.ds(r0, rows), :])

    emit(0, BLK, 0, False)

    def tile(t, _):
        emit(BLK + t * TQ, TQ, t, True)
        return 0

    lax.fori_loop(0, nt, tile, 0)


def _softmax_attention(q, kt, v, gate, mode):
    b, n, w = q.shape
    nt = (n - BLK) // TQ
    pairs = w // (2 * LANES)
    return pl.pallas_call(
        functools.partial(_softmax_kernel, nt=nt, mode=mode),
        grid=(b, pairs),
        in_specs=[pl.BlockSpec((1, n, 2 * LANES), lambda i, j: (i, 0, j)),
                  pl.BlockSpec((1, 2 * LANES, n), lambda i, j: (i, j, 0)),
                  pl.BlockSpec((1, n, 2 * LANES), lambda i, j: (i, 0, j)),
                  pl.BlockSpec((1, n, LANES), lambda i, j: (i, 0, pairs + j))],
        out_specs=pl.BlockSpec((1, n, LANES), lambda i, j: (i, 0, j)),
        out_shape=jax.ShapeDtypeStruct((b, n, w // 2), BF16),
        scratch_shapes=[pltpu.VMEM((2, TQ, n), F32),
                        pltpu.VMEM((2, TQ, BLK), F32),
                        pltpu.VMEM((2, TQ, BLK), F32),
                        pltpu.VMEM((2, TQ, BLK), F32),
                        pltpu.VMEM((2, TQ, BLK), F32),
                        pltpu.VMEM((TQ, TQ), F32)],
        compiler_params=pltpu.CompilerParams(
            dimension_semantics=("parallel", "parallel"), vmem_limit_bytes=VMEM_LIMIT),
        name="softmax_attention_" + mode,
    )(q, kt, v, gate)


def _swa_kernel(sink_ref, q_ref, kt_ref, v_ref, g_ref, o_ref, bias_ref, *, nb):
    rowi, lane = _iotas(BLK)
    group = N_HEADS // SWA_KV_HEADS
    zero_kt = jnp.zeros((HALF, BLK), BF16)
    r2 = lax.broadcasted_iota(jnp.int32, (2 * BLK, BLK), 0)
    c2 = lax.broadcasted_iota(jnp.int32, (2 * BLK, BLK), 1)
    ones_st = jnp.where((r2 >= BLK) == (c2 >= HALF), 1.0, 0.0).astype(BF16)
    slopes = [2.0 ** (-8.0 * (hd + 1) / N_HEADS) for hd in range(N_HEADS)]
    GENERIC = 2

    def key_slots(m):
        return ((0, jnp.where(m >= 1, LEAD, NO_KEY)),
                (jnp.maximum(m - 1, 0), jnp.where(m >= 2, 0, NO_KEY)),
                (m, LEAD))

    def slot_biases(m):
        qpos = m * BLK + rowi
        cq = _chunk_of(qpos)
        out = []
        for jb, lo in key_slots(m):
            kpos = jb * BLK + lane
            ck = _chunk_of(kpos)
            msk = (kpos >= lo) & (ck <= cq) & ((cq - ck <= WINDOW_CHUNKS) | (kpos < BLK))
            out.append((msk, jnp.abs(qpos - kpos).astype(F32)))
        return out

    for slot, (msk, dist) in enumerate(slot_biases(GENERIC)):
        for hd in range(N_HEADS):
            bias_ref[hd, slot] = jnp.where(msk, -slopes[hd] * dist, NEG)

    def q_block(m, generic):
        r0 = _al(m * BLK)
        slots = key_slots(m)
        if generic:
            extra = (m - GENERIC).astype(F32) * float(BLK)
            biases = lambda hd: [bias_ref[hd, 0] - slopes[hd] * extra, bias_ref[hd, 1], bias_ref[hd, 2]]
        else:
            special = slot_biases(m)
            biases = lambda hd: [jnp.where(msk, -slopes[hd] * dist, NEG) for msk, dist in special]
        for g in range(SWA_KV_HEADS):
            kbds, wcats = [], []
            for jb, _ in slots:
                k0 = _al(jb * BLK)
                ktg =kt_ref[0, g * HALF:(g + 1) * HALF, pl.ds(k0, BLK)]
                kbds.append(jnp.concatenate([jnp.concatenate([ktg, zero_kt], axis=1),
                                             jnp.concatenate([zero_kt, ktg], axis=1)], axis=0))
                vst = jnp.concatenate([v_ref[0, pl.ds(k0, BLK), (2 * g) * LANES:(2 * g + 1) * LANES],
                                       v_ref[0, pl.ds(k0, BLK), (2 * g + 1) * LANES:(2 * g + 2) * LANES]],
                                      axis=0)
                wcats.append(jnp.concatenate([vst, ones_st], axis=1))
            for pr in range(group // 2):
                pair = g * (group // 2) + pr
                q2 = q_ref[0, pl.ds(r0, BLK), pair * LANES:(pair + 1) * LANES]
                zs = [_dot(q2, kbd) for kbd in kbds]
                ps, sink_terms = [[], []], []
                for h in range(2):
                    head = 2 * pair + h
                    ss = [z[:, h * BLK:(h + 1) * BLK] + bias for z, bias in zip(zs, biases(head))]
                    sink = sink_ref[head]
                    mx = jnp.maximum(jnp.maximum(ss[0], ss[1]), ss[2])
                    m_row = jnp.maximum(jnp.max(mx, axis=1, keepdims=True), sink)
                    ps[h] = [jnp.exp(s - m_row).astype(BF16) for s in ss]
                    sink_terms.append(jnp.exp(sink - m_row))
                res = jnp.zeros((BLK, 2 * BLK), F32)
                for i in range(len(slots)):
                    res = res + _dot(jnp.concatenate([ps[0][i], ps[1][i]], axis=1), wcats[i])
                denom = res[:, BLK:] + jnp.where(lane < HALF, sink_terms[0], sink_terms[1])
                cols = slice(pair * LANES, (pair + 1) * LANES)
                o_ref[0, pl.ds(r0, BLK), cols] = _gated(res[:, :BLK] / denom, g_ref[0, pl.ds(r0, BLK), cols])

    for m in range(GENERIC):
        q_block(m, False)

    def q_blocks(i, carry):
        for u in range(SWA_UNROLL):
            q_block(GENERIC + SWA_UNROLL * i + u, True)
        return carry

    n_generic = nb - GENERIC
    lax.fori_loop(0, n_generic // SWA_UNROLL, q_blocks, 0)
    for m in range(nb - n_generic % SWA_UNROLL, nb):
        q_block(jnp.int32(m), True)


def _swa_attention(sinks, q, kt, v, gate):
    b, n, w = q.shape
    nb = n // BLK
    return pl.pallas_call(
        functools.partial(_swa_kernel, nb=nb),
        grid=(b,),
        in_specs=[
            pl.BlockSpec(memory_space=pltpu.SMEM),
            pl.BlockSpec((1, n, w), lambda i: (i, 0, 0)),
            pl.BlockSpec((1, kt.shape[1], n), lambda i: (i, 0, 0)),
            pl.BlockSpec((1, n, v.shape[2]), lambda i: (i, 0, 0)),
            pl.BlockSpec((1, n, w), lambda i: (i, 0, 0)),
        ],
        out_specs=pl.BlockSpec((1, n, w), lambda i: (i, 0, 0)),
        out_shape=jax.ShapeDtypeStruct((b, n, w), BF16),
        scratch_shapes=[pltpu.VMEM((N_HEADS, 3, BLK, BLK), F32)],
        compiler_params=pltpu.CompilerParams(
            dimension_semantics=("parallel",), vmem_limit_bytes=VMEM_LIMIT),
        name="swa_attention",
    )(sinks, q, kt, v, gate)


def _spread_heads(x, fill):
    lane = lax.broadcasted_iota(jnp.int32, (x.shape[0], LANES), 1)
    blocks = []
    for p in range(x.shape[1] // LANES):
        xp = x[:, p * LANES:(p + 1) * LANES]
        for odd in range(2):
            blk = 2 * p + odd
            if isinstance(fill, float):
                f = fill
            elif callable(fill):
                f = fill(lane)
            else:
                f = fill[:, blk * LANES:(blk + 1) * LANES]
            blocks.append(jnp.where((lane >= HALF) if odd else (lane < HALF), xp, f))
    return jnp.concatenate(blocks, axis=1)


def _even_proj_kernel(x_ref, wsb_ref, wc_ref, wkr_ref, wg_ref, gq_ref, gkv_ref,
                      wa_ref, wb_ref, wkv_ref, cq_ref, sq_ref, ck_ref, sk_ref,
                      qs_ref, kst_ref, vs_ref, qm_ref, kmt_ref, vm_ref, gate_ref):
    xb = x_ref[0].astype(BF16)
    w = N_HEADS * HEAD_DIM
    sb = _dot(xb, wsb_ref[...])
    qs_ref[0] = sb[:, :w].astype(BF16)
    kst_ref[0] = sb[:, w:2 * w].T.astype(BF16)
    vs_ref[0] = _spread_heads(sb[:, 2 * w:], 0.0).astype(BF16)
    gate_ref[0] = _dot(xb, wg_ref[...])

    c = _dot(xb, wc_ref[...])

    def rms(t, g):
        return (t * lax.rsqrt(jnp.mean(t * t, axis=-1, keepdims=True) + RMS_EPS) * g).astype(BF16)

    xq = rms(c[:, :MLA_RANK], gq_ref[...])
    xkv = rms(c[:, MLA_RANK:], gkv_ref[...])
    pairs = N_HEADS // 2
    cq = jnp.concatenate([cq_ref[...]] * pairs, axis=1)
    sq = jnp.concatenate([sq_ref[...]] * pairs, axis=1)
    qm_ref[0] = (_dot(xq, wa_ref[...]) * cq + _dot(xq, wb_ref[...]) * sq).astype(BF16)
    kr = _dot(xb, wkr_ref[...])
    krr = kr[:, :LANES] * ck_ref[...] + kr[:, LANES:] * sk_ref[...]
    krr = (krr, pltpu.roll(krr, HALF, 1))
    kv = _dot(xkv, wkv_ref[...])
    lane = lax.broadcasted_iota(jnp.int32, (kv.shape[0], LANES), 1)
    kblocks, vblocks = [], []
    for hd in range(N_HEADS):
        blk = slice(hd * LANES, (hd + 1) * LANES)
        v_half = (lane >= HALF) if hd % 2 else (lane < HALF)
        kblocks.append(jnp.where(v_half, krr[hd % 2], kv[:, blk]))
        vblocks.append(jnp.where(v_half, kv[:, blk], 1.0))
    kmt_ref[0] = jnp.concatenate(kblocks, axis=1).T.astype(BF16)
    vm_ref[0] = jnp.concatenate(vblocks, axis=1).astype(BF16)


def _even_proj(h, wts, tabs, tm):
    b, n, d = h.shape
    grid = (n // tm, b)
    row = lambda i, j: (j, i, 0)
    col = lambda i, j: (j, 0, i)
    const = lambda i, j: (0, 0)
    tab = lambda i, j: (i, 0)
    full = lambda a: pl.BlockSpec(a.shape, const)
    wide = N_HEADS * LANES
    narrow = N_HEADS * HEAD_DIM
    outs = [(narrow, BF16, False), (narrow, BF16, True), (wide, BF16, False),
            (wide, BF16, False), (wide, BF16, True), (wide, BF16, False), (d, F32, False)]
    return pl.pallas_call(
        _even_proj_kernel,
        grid=grid,
        in_specs=[pl.BlockSpec((1, tm, d), row)] + [full(a) for a in wts] + [
            pl.BlockSpec((tm, 2 * LANES), tab), pl.BlockSpec((tm, 2 * LANES), tab),
            pl.BlockSpec((tm, LANES), tab), pl.BlockSpec((tm, LANES), tab)],
        out_specs=[pl.BlockSpec((1, c, tm), col) if tr else pl.BlockSpec((1, tm, c), row)
                   for c, _, tr in outs],
        out_shape=[jax.ShapeDtypeStruct((b, c, n) if tr else (b, n, c), t) for c, t, tr in outs],
        compiler_params=pltpu.CompilerParams(
            dimension_semantics=("parallel", "parallel"), vmem_limit_bytes=VMEM_LIMIT),
        name="even_proj",
    )(h, *wts, *tabs)


def _odd_proj_kernel(x_ref, wqc_ref, wkc_ref, wvc_ref, wqf_ref, wkf_ref, wvf_ref, wfz_ref, wg_ref,
                     bf_ref, pf_ref,
                     qc_ref, kct_ref, vc_ref, qf_ref, kft_ref, vf_ref, gate_ref, carry_ref, *, tm):
    i = pl.program_id(1)
    xb = x_ref[0].astype(BF16)
    qc_ref[0] = _dot(xb, wqc_ref[...]).astype(BF16)
    kct_ref[0] = _dot(xb, wkc_ref[...]).T.astype(BF16)
    vc_ref[0] = _dot(xb, wvc_ref[...]).astype(BF16)
    gate_ref[0] = _dot(xb, wg_ref[...])

    vf_ref[0] = _spread_heads(_dot(xb, wvf_ref[...]), 1.0).astype(BF16)
    qf_ref[0] = _spread_heads(_dot(xb, wqf_ref[...]),
                              lambda lane: jnp.where((lane & (HALF - 1)) < 3, 1.0, 0.0)).astype(BF16)

    fz = _dot(xb, wfz_ref[...]) + bf_ref[...]
    r = lax.broadcasted_iota(jnp.int32, (tm, LANES), 0) + i * tm
    c = lax.broadcasted_iota(jnp.int32, (tm, LANES), 1)
    lf = jnp.where((c < N_HEADS) & (r >= LEAD), _log_sigmoid(fz), 0.0)

    def split3(t):
        a = t.astype(BF16)
        rem = t - a.astype(F32)
        bb = rem.astype(BF16)
        cc = (rem - bb.astype(F32)).astype(BF16)
        return jnp.concatenate([a, bb, cc], axis=1)

    rr = lax.broadcasted_iota(jnp.int32, (tm, tm), 0)
    cc = lax.broadcasted_iota(jnp.int32, (tm, tm), 1)
    tri = jnp.where(cc <= rr, 1.0, 0.0).astype(BF16)

    @pl.when(i == 0)
    def _():
        carry_ref[...] = jnp.zeros_like(carry_ref)

    c3 = _dot(tri, split3(lf))
    cum = c3[:, :LANES] + c3[:, LANES:2 * LANES] + c3[:, 2 * LANES:] + carry_ref[0:1, :]
    carry_ref[...] = jnp.broadcast_to(cum[tm - 1:tm, :], carry_ref.shape)
    kft_ref[0] = _spread_heads(_dot(xb, wkf_ref[...]), _dot(split3(-cum), pf_ref[...])).T.astype(BF16)


def _odd_proj(h, wts, tm):
    b, n, d = h.shape
    row = lambda j, i: (j, i, 0)
    col = lambda j, i: (j, 0, i)
    const = lambda j, i: (0, 0)
    full = lambda a: pl.BlockSpec(a.shape, const)
    wide = N_HEADS * LANES
    narrow = N_HEADS * HEAD_DIM
    outs = [(narrow, BF16, False), (SWA_KV_HEADS * HEAD_DIM, BF16, True), (narrow, BF16, False),
            (wide, BF16, False), (wide, BF16, True), (wide, BF16, False), (d, F32, False)]
    return pl.pallas_call(
        functools.partial(_odd_proj_kernel, tm=tm),
        grid=(b, n // tm),
        in_specs=[pl.BlockSpec((1, tm, d), row)] + [full(a) for a in wts],
        out_specs=[pl.BlockSpec((1, c, tm), col) if tr else pl.BlockSpec((1, tm, c), row)
                   for c, _, tr in outs],
        out_shape=[jax.ShapeDtypeStruct((b, c, n) if tr else (b, n, c), t) for c, t, tr in outs],
        scratch_shapes=[pltpu.VMEM((8, LANES), F32)],
        compiler_params=pltpu.CompilerParams(
            dimension_semantics=("parallel", "arbitrary"), vmem_limit_bytes=VMEM_LIMIT),
        name="odd_proj",
    )(h, *wts)


def _out_math(ma, mb, h, w_ref, g, b):
    half = ma.shape[-1]
    y = _dot(ma, w_ref[:half, :]) + _dot(mb, w_ref[half:, :])
    r = DN_ALPHA * h + y
    mu = jnp.mean(r, axis=-1, keepdims=True)
    cen = r - mu
    var = jnp.mean(cen * cen, axis=-1, keepdims=True)
    return cen * lax.rsqrt(var + LN_EPS) * g + b


def _out_kernel(ma_ref, mb_ref, h_ref, w_ref, g_ref, b_ref, o_ref):
    o_ref[...] = _out_math(ma_ref[...], mb_ref[...], h_ref[...], w_ref, g_ref[...], b_ref[...])


def _out_frames_kernel(ma_ref, mb_ref, h_ref, w_ref, g_ref, b_ref, o_ref):
    o_ref[0] = _out_math(ma_ref[...], mb_ref[...], h_ref[...], w_ref, g_ref[...], b_ref[...])


def _out_proj_frames(ma, mb, h, w, g, bias):
    b, n, d = h.shape
    half = ma.shape[-1]
    flat = lambda a: a.reshape(b * n, a.shape[-1])
    src = lambda i, j: (pl.multiple_of(i * n + BLK + j * TQ, BLK), 0)
    window = lambda width: pl.BlockSpec((pl.Element(TQ), pl.Element(width)), src)
    const = lambda i, j: (0, 0)
    return pl.pallas_call(
        _out_frames_kernel,
        grid=(b, (n - BLK) // TQ),
        in_specs=[window(half), window(half), window(d),
                  pl.BlockSpec(w.shape, const), pl.BlockSpec((1, d), const), pl.BlockSpec((1, d), const)],
        out_specs=pl.BlockSpec((1, TQ, d), lambda i, j: (i, j, 0)),
        out_shape=jax.ShapeDtypeStruct((b, n - BLK, d), F32),
        compiler_params=pltpu.CompilerParams(
            dimension_semantics=("parallel", "parallel"), vmem_limit_bytes=VMEM_LIMIT),
        name="out_proj_ln_frames",
    )(flat(ma), flat(mb), flat(h), w, g, bias)


def _out_proj(ma, mb, h, w, g, bias, tm):
    rows, d = h.shape
    half = ma.shape[-1]
    row = lambda i: (i, 0)
    const = lambda i: (0, 0)
    return pl.pallas_call(
        _out_kernel,
        grid=(rows // tm,),
        in_specs=[pl.BlockSpec((tm, half), row), pl.BlockSpec((tm, half), row),
                  pl.BlockSpec((tm, d), row),
                  pl.BlockSpec(w.shape, const), pl.BlockSpec((1, d), const), pl.BlockSpec((1, d), const)],
        out_specs=pl.BlockSpec((tm, d), row),
        out_shape=jax.ShapeDtypeStruct((rows, d), F32),
        compiler_params=pltpu.CompilerParams(
            dimension_semantics=("parallel",), vmem_limit_bytes=VMEM_LIMIT),
        name="out_proj_ln",
    )(ma, mb, h, w, g, bias)


def _by_parity(even, odd):
    is_odd = (jnp.arange(N_HEADS) % 2 == 1)[:, None]
    return jnp.where(is_odd, odd, even).reshape(even.shape[:-2] + (N_HEADS * LANES,))


def _mla_head_blocks(nope, rot):
    spare = jnp.zeros(rot.shape[:-1] + (HALF - MLA_ROPE,), rot.dtype)
    return _by_parity(jnp.concatenate([rot, spare, nope], axis=-1),
                      jnp.concatenate([nope, rot, spare], axis=-1))


def _even_weights(w_in, g_cq, g_ckv, w_uq, w_ukv):
    a = N_HEADS * HEAD_DIM
    wq, wk, wv = w_in[:, :a], w_in[:, a:2 * a], w_in[:, 2 * a:3 * a]
    o = 3 * a
    w_c = w_in[:, o:o + 2 * MLA_RANK]
    o += 2 * MLA_RANK
    w_kr = w_in[:, o:o + MLA_ROPE]
    o += MLA_ROPE
    w_g = w_in[:, o:]
    w_sb = jnp.concatenate([wq * HEAD_DIM ** -0.5, wk, wv], axis=1)
    hr = MLA_ROPE // 2
    zpad = jnp.zeros((w_in.shape[0], LANES - MLA_ROPE), w_in.dtype)
    w_kr2 = jnp.concatenate([w_kr, zpad, w_kr[:, hr:], w_kr[:, :hr], zpad], axis=1)

    uq = w_uq.reshape(MLA_RANK, N_HEADS, MLA_NOPE + MLA_ROPE)
    ukv = w_ukv.reshape(MLA_RANK, N_HEADS, MLA_NOPE + HEAD_DIM)
    nope, r1, r2 = uq[..., :MLA_NOPE], uq[..., MLA_NOPE:MLA_NOPE + hr], uq[..., MLA_NOPE + hr:]
    w_a = _mla_head_blocks(nope, jnp.concatenate([r1, r2], axis=-1))
    w_b = _mla_head_blocks(jnp.zeros_like(nope), jnp.concatenate([r2, r1], axis=-1))
    k_nope, v = ukv[..., :MLA_NOPE], ukv[..., MLA_NOPE:]
    w_kv = _by_parity(jnp.concatenate([v, k_nope], axis=-1), ukv)
    bf = lambda t: t.astype(BF16)
    return (bf(w_sb), bf(w_c), bf(w_kr2), bf(w_g), g_cq[None, :], g_ckv[None, :],
            bf(w_a), bf(w_b), bf(w_kv))


def _rope_tables(n):
    hr = MLA_ROPE // 2
    pos = jnp.arange(n, dtype=F32) - LEAD
    inv = ROPE_BASE ** (-jnp.arange(hr, dtype=F32) / hr)
    ang = pos[:, None] * inv[None, :]
    cos, sin = jnp.cos(ang), jnp.sin(ang)
    ones = jnp.ones((n, MLA_NOPE), F32)
    scale = (MLA_NOPE + MLA_ROPE) ** -0.5
    c_rot, s_rot = jnp.concatenate([cos, cos], axis=1), jnp.concatenate([-sin, sin], axis=1)
    spare = jnp.zeros((n, HALF - MLA_ROPE), F32)
    c_q = jnp.concatenate([c_rot, spare, ones, ones, c_rot, spare], axis=1) * scale
    s_q = jnp.concatenate([s_rot, spare, 0.0 * ones, 0.0 * ones, s_rot, spare], axis=1) * scale
    z96 = jnp.zeros((n, LANES - MLA_ROPE), F32)
    c_k = jnp.concatenate([c_rot, z96], axis=1)
    s_k = jnp.concatenate([s_rot, z96], axis=1)
    return c_q, s_q, c_k, s_k


def _odd_weights(w_in, b_forget):
    a = N_HEADS * HEAD_DIM
    kvw = SWA_KV_HEADS * HEAD_DIM
    wqc = w_in[:, :a]
    o = a
    wkc = w_in[:, o:o + kvw]
    o += kvw
    wvc = w_in[:, o:o + kvw]
    o += kvw
    wqd, wkd, wvd = w_in[:, o:o + a], w_in[:, o + a:o + 2 * a], w_in[:, o + 2 * a:o + 3 * a]
    o += 3 * a
    wfz = w_in[:, o:o + N_HEADS]
    o += N_HEADS
    w_g = w_in[:, o:]
    scale = HEAD_DIM ** -0.5
    v0, v1 = wvc[:, :HEAD_DIM], wvc[:, HEAD_DIM:]
    z = jnp.zeros_like(v0)
    w_vc = jnp.concatenate([v0, z, z, v0, v1, z, z, v1], axis=1)
    w_fz = jnp.concatenate([wfz, jnp.zeros((w_in.shape[0], LANES - N_HEADS), w_in.dtype)], axis=1)
    b_f = jnp.concatenate([b_forget, jnp.zeros((LANES - N_HEADS,), b_forget.dtype)])[None, :]
    bf = lambda t: t.astype(BF16)
    return (bf(wqc * scale), bf(wkc), bf(w_vc), bf(wqd * scale), bf(wkd), bf(wvd), bf(w_fz), bf(w_g),
            b_f.astype(F32))


def _piece_placement():
    p_f = np.zeros((3 * LANES, N_HEADS * LANES), np.float32)
    for t in range(3):
        for h in range(N_HEADS):
            p_f[t * LANES + h, h * LANES + (0 if h % 2 else HALF) + t] = 1.0
    return jnp.asarray(p_f, BF16)


def _row_tile(nb, cap):
    g = max(t for t in range(1, cap + 1) if nb % t == 0)
    return g * BLK


def kernel(x, meta_tokens, w_in_even, g_cq, g_ckv, w_uq, w_ukv, w_out_even,
           w_in_odd, b_forget, sink_logits, w_out_odd, ln_gain, ln_bias):
    b, s, d = x.shape
    assert s % SB_TQ == 0 and s % TQ == 0 and d == D_MODEL
    n = BLK + s
    nb = n // BLK
    h = jnp.concatenate([jnp.zeros((b, LEAD, d), x.dtype),
                         jnp.broadcast_to(meta_tokens[None].astype(x.dtype), (b, N_META, d)), x], axis=1)
    tabs = _rope_tables(n)
    tm_proj = _row_tile(nb, 3)
    tm_out = _row_tile(b * nb, 4)
    even_wts = jax.vmap(_even_weights)(w_in_even, g_cq, g_ckv, w_uq, w_ukv)
    odd_wts = jax.vmap(_odd_weights)(w_in_odd, b_forget)
    p_f = _piece_placement()
    for layer in range(DEPTH):
        i = layer // 2
        if layer % 2 == 0:
            wts = tuple(t[i] for t in even_wts)
            qs, kst, vs, qm, kmt, vm, gate = _even_proj(h, wts, tabs, tm_proj)
            mix_a = _sb_attention(qs, kst, vs, gate)
            mix_b = _softmax_attention(qm, kmt, vm, gate, "chunk")
            w_out = w_out_even[i]
        else:
            wts = tuple(t[i] for t in odd_wts) + (p_f,)
            qc, kct, vc, qf, kft, vf, gate = _odd_proj(h, wts, tm_proj)
            mix_a = _swa_attention(sink_logits[i].astype(F32), qc, kct, vc, gate)
            mix_b = _softmax_attention(qf, kft, vf, gate, "causal")
            w_out = w_out_odd[i]
        ln = (w_out.astype(BF16), ln_gain[layer][None, :], ln_bias[layer][None, :])
        if layer == DEPTH - 1:
            return _out_proj_frames(mix_a, mix_b, h, *ln)
        rows = b * n
        h = _out_proj(mix_a.reshape(rows, -1), mix_b.reshape(rows, -1), h.reshape(rows, d),
                      *ln, tm_out).reshape(b, n, d)
```

```python
import functools

import jax
import jax.numpy as jnp
import numpy as np
from jax import lax
from jax.experimental import pallas as pl
from jax.experimental.pallas import tpu as pltpu

D_MODEL = 1024
CHUNK = 64
N_META = 16
HEAD_DIM = 64
N_HEADS = 8
MLA_NOPE = 64
MLA_ROPE = 32
MLA_RANK = 256
SWA_KV_HEADS = 2
WINDOW_CHUNKS = 2
ROPE_BASE = 10000.0
DEPTH = 4
DN_ALPHA = (2 * DEPTH) ** 0.25
LN_EPS = 1e-5
RMS_EPS = 1e-6
NEG = -1e30
NO_KEY = 1 << 30
SIGN_BIT = -(1 << 31)

LANES = 128
BLK = 128
TB = 4
TQ = TB * BLK
SB_TB = 8
SB_TQ = SB_TB * BLK
SB_UNROLL = 8
SWA_UNROLL = 3
OUT_TM = 1024
OUT_PART = 256
LEAD = BLK - N_META
HALF = LANES // 2
VMEM_LIMIT = 56 * 1024 * 1024

F32 = jnp.float32
BF16 = jnp.bfloat16


def _dot(a, b):
    return jnp.dot(a, b, preferred_element_type=F32)


def _chunk_of(pos):
    return jnp.maximum((pos >> 6) - 1, 0)


def _log_sigmoid(x):
    return jnp.minimum(x, 0.0) - jnp.log(1.0 + jnp.exp(-jnp.abs(x)))


def _al(x):
    return x if isinstance(x, int) else pl.multiple_of(x, BLK)


def _gated(mixed, gate):
    return (mixed * (gate * jax.nn.sigmoid(gate))).astype(BF16)


def _iotas(rows):
    return (lax.broadcasted_iota(jnp.int32, (rows, BLK), 0),
            lax.broadcasted_iota(jnp.int32, (rows, BLK), 1))


def _sb_kernel(q_ref, kt_ref, v_ref, g_ref, o_ref, u_ref, carry_ref, acc_ref, *, nt):
    r2 = lax.broadcasted_iota(jnp.int32, (2 * BLK, 2 * BLK), 0)
    c2 = lax.broadcasted_iota(jnp.int32, (2 * BLK, 2 * BLK), 1)
    u_ref[...] = jnp.where(((r2 >= BLK) == (c2 >= BLK)) & ((r2 & (BLK - 1)) > (c2 & (BLK - 1))),
                           1.0, 0.0).astype(BF16)
    zero_kt = jnp.zeros((HALF, BLK), BF16)

    def step(row0, off, rows, jb, mask=None):
        r0 = _al(row0 + off)
        k0 = _al(jb * BLK)
        sl = slice(off, off + rows)
        ktb = kt_ref[0, :, pl.ds(k0, BLK)]
        kbd = jnp.concatenate([jnp.concatenate([ktb[:HALF], zero_kt], axis=1),
                               jnp.concatenate([zero_kt, ktb[HALF:]], axis=1)], axis=0)
        z2 = _dot(q_ref[0, pl.ds(r0, rows), :], kbd)
        vst = jnp.concatenate([v_ref[0, pl.ds(k0, BLK), :LANES],
                               v_ref[0, pl.ds(k0, BLK), LANES:]], axis=0)
        if mask == "meta_keys":
            parts = [(0, rows, _iotas(rows)[1] >= LEAD)]
        elif mask is not None:
            rowi, lane = _iotas(BLK)
            local = lane < rowi
            parts = [(0, BLK, local if mask == "diagonal" else local & (lane >= LEAD))]
            if rows > BLK:
                parts.append((BLK, rows, None))

        def masked(x):
            if mask is None:
                return x
            return jnp.concatenate([x[a:b] if m is None else jnp.where(m, x[a:b], 0.0)
                                    for a, b, m in parts], axis=0)

        stays, log_betas = [], []
        for h in range(2):
            z = z2[:, h * BLK:(h + 1) * BLK]
            neg_abs = lax.bitcast_convert_type(
                lax.bitcast_convert_type(z, jnp.int32) | SIGN_BIT, F32)
            stay = jnp.maximum(z, 0.0) + jnp.log(1.0 + jnp.exp(neg_abs))
            log_betas.append(z - stay)
            stays.append(masked(stay))
        cs = _dot(jnp.concatenate([s.astype(BF16) for s in stays], axis=1), u_ref[...])
        ws = []
        for h in range(2):
            carry = carry_ref[h, sl, :]
            w = masked(jnp.exp(log_betas[h] - (cs[:, h * BLK:(h + 1) * BLK] + carry)))
            carry_ref[h, sl, :] = carry + jnp.sum(stays[h], axis=1, keepdims=True)
            ws.append(w.astype(BF16))
        acc_ref[sl, :] += _dot(jnp.concatenate(ws, axis=1), vst)

    def reset(rows):
        carry_ref[:, :rows, :] = jnp.zeros((2, rows, BLK), F32)
        acc_ref[:rows, :] = jnp.zeros((rows, BLK), F32)

    reset(BLK)
    step(0, 0, BLK, 0, "meta_tile")
    o_ref[0, :BLK, :] = _gated(acc_ref[:BLK, :], g_ref[0, :BLK, :])

    def tile(t, _):
        row0 = BLK + t * SB_TQ
        jb0 = 1 + t * SB_TB
        reset(SB_TQ)
        for d in reversed(range(SB_TB)):
            step(row0, d * BLK, SB_TQ - d * BLK, jb0 + d, "diagonal")

        def body(i, c):
            for u in range(SB_UNROLL):
                step(row0, 0, SB_TQ, jb0 - 1 - (i * SB_UNROLL + u))
            return c

        lax.fori_loop(0, t * (SB_TB // SB_UNROLL), body, 0)
        step(row0, 0, SB_TQ, 0, "meta_keys")
        o_ref[0, pl.ds(_al(row0), SB_TQ), :] = _gated(acc_ref[...], g_ref[0, pl.ds(_al(row0), SB_TQ), :])
        return 0

    lax.fori_loop(0, nt, tile, 0)


def _sb_attention(q, kt, v, gate):
    b, n, w = q.shape
    nt = (n - BLK) // SB_TQ
    return pl.pallas_call(
        functools.partial(_sb_kernel, nt=nt),
        grid=(b, w // LANES),
        in_specs=[pl.BlockSpec((1, n, LANES), lambda i, j: (i, 0, j)),
                  pl.BlockSpec((1, LANES, n), lambda i, j: (i, j, 0)),
                  pl.BlockSpec((1, n, 2 * LANES), lambda i, j: (i, 0, j)),
                  pl.BlockSpec((1, n, LANES), lambda i, j: (i, 0, j))],
        out_specs=pl.BlockSpec((1, n, LANES), lambda i, j: (i, 0, j)),
        out_shape=jax.ShapeDtypeStruct((b, n, w), BF16),
        scratch_shapes=[pltpu.VMEM((2 * BLK, 2 * BLK), BF16),
                        pltpu.VMEM((2, SB_TQ, BLK), F32),
                        pltpu.VMEM((SB_TQ, BLK), F32)],
        compiler_params=pltpu.CompilerParams(
            dimension_semantics=("parallel", "parallel"), vmem_limit_bytes=VMEM_LIMIT),
        name="sb_attention",
    )(q, kt, v, gate)


def _softmax_kernel(q_ref, kt_ref, v_ref, g_ref, o_ref, s_ref, sd_ref, mx_ref, mb_ref, acc_ref, bias_ref,
                    *, nt, mode):
    def visible(kpos, qpos, lo):
        if mode == "chunk":
            return (kpos >= lo) & (_chunk_of(kpos) <= _chunk_of(qpos))
        return (kpos >= lo) & (kpos <= qpos)

    ql = BLK + lax.broadcasted_iota(jnp.int32, (TQ, TQ), 0)
    kl = BLK + lax.broadcasted_iota(jnp.int32, (TQ, TQ), 1)
    bias_ref[...] = jnp.where(visible(kl, ql, 0), 0.0, NEG)

    def chunk_loop(n_chunks, chunk_fn):
        def quad(i, carry):
            for u in range(4):
                chunk_fn(4 * i + u)
            return carry

        lax.fori_loop(0, n_chunks >> 2, quad, 0)
        done = (n_chunks >> 2) << 2

        @pl.when((n_chunks & 2) != 0)
        def _():
            chunk_fn(done)
            chunk_fn(done + 1)

        @pl.when((n_chunks & 1) != 0)
        def _():
            chunk_fn(n_chunks - 1)

    def emit(row0, rows, n_chunks, frames):
        frs = [slice(h * LANES, (h + 1) * LANES) for h in range(2)]
        r0 = _al(row0)
        rowi, lane = _iotas(rows)

        def chunk_scores(c, h):
            k0 = _al(BLK + c * TQ)
            return _dot(q_ref[0, pl.ds(r0, rows), frs[h]], kt_ref[0, frs[h], pl.ds(k0, TQ)])

        def fold_max(c, h, s):
            s_ref[h, :, pl.ds(_al(BLK + c * TQ), TQ)] = s
            m4 = jnp.maximum(jnp.maximum(s[:, :BLK], s[:, BLK:2 * BLK]),
                             jnp.maximum(s[:, 2 * BLK:3 * BLK], s[:, 3 * BLK:]))
            mx_ref[h] = jnp.maximum(mx_ref[h], m4)

        for h, fr in enumerate(frs):
            s0 = _dot(q_ref[0, pl.ds(r0, rows), fr], kt_ref[0, fr, :BLK])
            s0 = jnp.where(visible(lane, r0 + rowi, LEAD), s0, NEG)
            sd_ref[h, :rows, :] = s0
            mx_ref[h, :rows, :] = s0
        if frames:
            for h in range(2):
                fold_max(n_chunks, h, chunk_scores(n_chunks, h) + bias_ref[...])

            def p1(c):
                for h in range(2):
                    fold_max(c, h, chunk_scores(c, h))

            chunk_loop(n_chunks, p1)

        for h, fr in enumerate(frs):
            m_row = jnp.max(mx_ref[h, :rows, :], axis=1, keepdims=True)
            mb_ref[h, :rows, :] = jnp.broadcast_to(m_row, (rows, BLK))
            p0 = jnp.exp(sd_ref[h, :rows, :] - mb_ref[h, :rows, :]).astype(BF16)
            acc_ref[h, :rows, :] = _dot(p0, v_ref[0, :BLK, fr])

        if frames:
            def p2(c):
                k0 = _al(BLK + c * TQ)
                for h, fr in enumerate(frs):
                    mb = mb_ref[h]
                    p = jnp.exp(s_ref[h, :, pl.ds(k0, TQ)]
                                - jnp.concatenate([mb] * TB, axis=1)).astype(BF16)
                    acc_ref[h] += _dot(p, v_ref[0, pl.ds(k0, TQ), fr])

            chunk_loop(n_chunks + 1, p2)

        outs = []
        for h in range(2):
            acc = acc_ref[h, :rows, :]
            outs.append(acc / pltpu.roll(acc, HALF, 1))
        o_ref[0, pl.ds(r0, rows), :] = _gated(jnp.where(lane < HALF, outs[0], outs[1]),
                                              g_ref[0, pl.ds(r0, rows), :])

    emit(0, BLK, 0, False)

    def tile(t, _):
        emit(BLK + t * TQ, TQ, t, True)
        return 0

    lax.fori_loop(0, nt, tile, 0)


def _softmax_attention(q, kt, v, gate, mode):
    b, n, w = q.shape
    nt = (n - BLK) // TQ
    pairs = w // (2 * LANES)
    return pl.pallas_call(
        functools.partial(_softmax_kernel, nt=nt, mode=mode),
        grid=(b, pairs),
        in_specs=[pl.BlockSpec((1, n, 2 * LANES), lambda i, j: (i, 0, j)),
                  pl.BlockSpec((1, 2 * LANES, n), lambda i, j: (i, j, 0)),
                  pl.BlockSpec((1, n, 2 * LANES), lambda i, j: (i, 0, j)),
                  pl.BlockSpec((1, n, LANES), lambda i, j: (i, 0, pairs + j))],
        out_specs=pl.BlockSpec((1, n, LANES), lambda i, j: (i, 0, j)),
        out_shape=jax.ShapeDtypeStruct((b, n, w // 2), BF16),
        scratch_shapes=[pltpu.VMEM((2, TQ, n), F32),
                        pltpu.VMEM((2, TQ, BLK), F32),
                        pltpu.VMEM((2, TQ, BLK), F32),
                        pltpu.VMEM((2, TQ, BLK), F32),
                        pltpu.VMEM((2, TQ, BLK), F32),
                        pltpu.VMEM((TQ, TQ), F32)],
        compiler_params=pltpu.CompilerParams(
            dimension_semantics=("parallel", "parallel"), vmem_limit_bytes=VMEM_LIMIT),
        name="softmax_attention_" + mode,
    )(q, kt, v, gate)


def _swa_kernel(sink_ref, q_ref, kt_ref, v_ref, g_ref, o_ref, bias_ref, *, nb):
    rowi, lane = _iotas(BLK)
    group = N_HEADS // SWA_KV_HEADS
    zero_kt = jnp.zeros((HALF, BLK), BF16)
    r2 = lax.broadcasted_iota(jnp.int32, (2 * BLK, BLK), 0)
    c2 = lax.broadcasted_iota(jnp.int32, (2 * BLK, BLK), 1)
    ones_st = jnp.where((r2 >= BLK) == (c2 >= HALF), 1.0, 0.0).astype(BF16)
    slopes = [2.0 ** (-8.0 * (hd + 1) / N_HEADS) for hd in range(N_HEADS)]
    GENERIC = 2

    def key_slots(m):
        return ((0, jnp.where(m >= 1, LEAD, NO_KEY)),
                (jnp.maximum(m - 1, 0), jnp.where(m >= 2, 0, NO_KEY)),
                (m, LEAD))

    def slot_biases(m):
        qpos = m * BLK + rowi
        cq = _chunk_of(qpos)
        out = []
        for jb, lo in key_slots(m):
            kpos = jb * BLK + lane
            ck = _chunk_of(kpos)
            msk = (kpos >= lo) & (ck <= cq) & ((cq - ck <= WINDOW_CHUNKS) | (kpos < BLK))
            out.append((msk, jnp.abs(qpos - kpos).astype(F32)))
        return out

    for slot, (msk, dist) in enumerate(slot_biases(GENERIC)):
        for hd in range(N_HEADS):
            bias_ref[hd, slot] = jnp.where(msk, -slopes[hd] * dist, NEG)

    def q_block(m, generic):
        r0 = _al(m * BLK)
        slots = key_slots(m)
        if generic:
            extra = jnp.float32(BLK) * (m - GENERIC)
            biases = lambda hd: [bias_ref[hd, 0] - slopes[hd] * extra, bias_ref[hd, 1], bias_ref[hd, 2]]
        else:
            special = slot_biases(m)
            biases = lambda hd: [jnp.where(msk, -slopes[hd] * dist, NEG) for msk, dist in special]
        for g in range(SWA_KV_HEADS):
            kbds, wcats = [], []
            for jb, _ in slots:
                k0 = _al(jb * BLK)
                ktg =kt_ref[0, g * HALF:(g + 1) * HALF, pl.ds(k0, BLK)]
                kbds.append(jnp.concatenate([jnp.concatenate([ktg, zero_kt], axis=1),
                                             jnp.concatenate([zero_kt, ktg], axis=1)], axis=0))
                vst = jnp.concatenate([v_ref[0, pl.ds(k0, BLK), (2 * g) * LANES:(2 * g + 1) * LANES],
                                       v_ref[0, pl.ds(k0, BLK), (2 * g + 1) * LANES:(2 * g + 2) * LANES]],
                                      axis=0)
                wcats.append(jnp.concatenate([vst, ones_st], axis=1))
            for pr in range(group // 2):
                pair = g * (group // 2) + pr
                q2 = q_ref[0, pl.ds(r0, BLK), pair * LANES:(pair + 1) * LANES]
                zs = [_dot(q2, kbd) for kbd in kbds]
                ps, sink_terms = [[], []], []
                for h in range(2):
                    head = 2 * pair + h
                    ss = [z[:, h * BLK:(h + 1) * BLK] + bias for z, bias in zip(zs, biases(head))]
                    sink = sink_ref[head]
                    mx = jnp.maximum(jnp.maximum(ss[0], ss[1]), ss[2])
                    m_row = jnp.maximum(jnp.max(mx, axis=1, keepdims=True), sink)
                    ps[h] = [jnp.exp(s - m_row).astype(BF16) for s in ss]
                    sink_terms.append(jnp.exp(sink - m_row))
                res = jnp.zeros((BLK, 2 * BLK), F32)
                for i in range(len(slots)):
                    res = res + _dot(jnp.concatenate([ps[0][i], ps[1][i]], axis=1), wcats[i])
                denom = res[:, BLK:] + jnp.where(lane < HALF, sink_terms[0], sink_terms[1])
                cols = slice(pair * LANES, (pair + 1) * LANES)
                o_ref[0, pl.ds(r0, BLK), cols] = _gated(res[:, :BLK] / denom, g_ref[0, pl.ds(r0, BLK), cols])

    for m in range(GENERIC):
        q_block(m, False)

    def q_blocks(i, carry):
        for u in range(SWA_UNROLL):
            q_block(GENERIC + SWA_UNROLL * i + u, True)
        return carry

    n_generic = nb - GENERIC
    lax.fori_loop(0, n_generic // SWA_UNROLL, q_blocks, 0)
    for m in range(nb - n_generic % SWA_UNROLL, nb):
        q_block(jnp.int32(m), True)


def _swa_attention(sinks, q, kt, v, gate):
    b, n, w = q.shape
    nb = n // BLK
    return pl.pallas_call(
        functools.partial(_swa_kernel, nb=nb),
        grid=(b,),
        in_specs=[
            pl.BlockSpec(memory_space=pltpu.SMEM),
            pl.BlockSpec((1, n, w), lambda i: (i, 0, 0)),
            pl.BlockSpec((1, kt.shape[1], n), lambda i: (i, 0, 0)),
            pl.BlockSpec((1, n, v.shape[2]), lambda i: (i, 0, 0)),
            pl.BlockSpec((1, n, w), lambda i: (i, 0, 0)),
        ],
        out_specs=pl.BlockSpec((1, n, w), lambda i: (i, 0, 0)),
        out_shape=jax.ShapeDtypeStruct((b, n, w), BF16),
        scratch_shapes=[pltpu.VMEM((N_HEADS, 3, BLK, BLK), F32)],
        compiler_params=pltpu.CompilerParams(
            dimension_semantics=("parallel",), vmem_limit_bytes=VMEM_LIMIT),
        name="swa_attention",
    )(sinks, q, kt, v, gate)


def _spread_heads(x, fill):
    lane = lax.broadcasted_iota(jnp.int32, (x.shape[0], LANES), 1)
    blocks = []
    for p in range(x.shape[1] // LANES):
        xp = x[:, p * LANES:(p + 1) * LANES]
        for odd in range(2):
            blk = 2 * p + odd
            if isinstance(fill, float):
                f = fill
            elif callable(fill):
                f = fill(lane)
            else:
                f = fill[:, blk * LANES:(blk + 1) * LANES]
            blocks.append(jnp.where((lane >= HALF) if odd else (lane < HALF), xp, f))
    return jnp.concatenate(blocks, axis=1)


def _even_proj_kernel(x_ref, wsb_ref, wc_ref, wkr_ref, wg_ref, gq_ref, gkv_ref,
                      wa_ref, wb_ref, wkv_ref, cq_ref, sq_ref, ck_ref, sk_ref,
                      qs_ref, kst_ref, vs_ref, qm_ref, kmt_ref, vm_ref, gate_ref):
    xb = x_ref[0].astype(BF16)
    w = N_HEADS * HEAD_DIM
    sb = _dot(xb, wsb_ref[...])
    qs_ref[0] = sb[:, :w].astype(BF16)
    kst_ref[0] = sb[:, w:2 * w].T.astype(BF16)
    vs_ref[0] = _spread_heads(sb[:, 2 * w:], 0.0).astype(BF16)
    gate_ref[0] = _dot(xb, wg_ref[...])

    c = _dot(xb, wc_ref[...])

    def rms(t, g):
        return (t * lax.rsqrt(jnp.mean(t * t, axis=-1, keepdims=True) + RMS_EPS) * g).astype(BF16)

    xq = rms(c[:, :MLA_RANK], gq_ref[...])
    xkv = rms(c[:, MLA_RANK:], gkv_ref[...])
    pairs = N_HEADS // 2
    cq = jnp.concatenate([cq_ref[...]] * pairs, axis=1)
    sq = jnp.concatenate([sq_ref[...]] * pairs, axis=1)
    qm_ref[0] = (_dot(xq, wa_ref[...]) * cq + _dot(xq, wb_ref[...]) * sq).astype(BF16)
    kr = _dot(xb, wkr_ref[...])
    krr = kr[:, :LANES] * ck_ref[...] + kr[:, LANES:] * sk_ref[...]
    krr = (krr, pltpu.roll(krr, HALF, 1))
    kv = _dot(xkv, wkv_ref[...])
    lane = lax.broadcasted_iota(jnp.int32, (kv.shape[0], LANES), 1)
    kblocks, vblocks = [], []
    for hd in range(N_HEADS):
        blk = slice(hd * LANES, (hd + 1) * LANES)
        v_half = (lane >= HALF) if hd % 2 else (lane < HALF)
        kblocks.append(jnp.where(v_half, krr[hd % 2], kv[:, blk]))
        vblocks.append(jnp.where(v_half, kv[:, blk], 1.0))
    kmt_ref[0] = jnp.concatenate(kblocks, axis=1).T.astype(BF16)
    vm_ref[0] = jnp.concatenate(vblocks, axis=1).astype(BF16)


def _even_proj(h, wts, tabs, tm):
    b, n, d = h.shape
    grid = (n // tm, b)
    row = lambda i, j: (j, i, 0)
    col = lambda i, j: (j, 0, i)
    const = lambda i, j: (0, 0)
    tab = lambda i, j: (i, 0)
    full = lambda a: pl.BlockSpec(a.shape, const)
    wide = N_HEADS * LANES
    narrow = N_HEADS * HEAD_DIM
    outs = [(narrow, BF16, False), (narrow, BF16, True), (wide, BF16, False),
            (wide, BF16, False), (wide, BF16, True), (wide, BF16, False), (d, F32, False)]
    return pl.pallas_call(
        _even_proj_kernel,
        grid=grid,
        in_specs=[pl.BlockSpec((1, tm, d), row)] + [full(a) for a in wts] + [
            pl.BlockSpec((tm, 2 * LANES), tab), pl.BlockSpec((tm, 2 * LANES), tab),
            pl.BlockSpec((tm, LANES), tab), pl.BlockSpec((tm, LANES), tab)],
        out_specs=[pl.BlockSpec((1, c, tm), col) if tr else pl.BlockSpec((1, tm, c), row)
                   for c, _, tr in outs],
        out_shape=[jax.ShapeDtypeStruct((b, c, n) if tr else (b, n, c), t) for c, t, tr in outs],
        compiler_params=pltpu.CompilerParams(
            dimension_semantics=("parallel", "parallel"), vmem_limit_bytes=VMEM_LIMIT),
        name="even_proj",
    )(h, *wts, *tabs)


def _odd_proj_kernel(x_ref, wqc_ref, wkc_ref, wvc_ref, wqf_ref, wkf_ref, wvf_ref, wfz_ref, wg_ref,
                     bf_ref, pf_ref,
                     qc_ref, kct_ref, vc_ref, qf_ref, kft_ref, vf_ref, gate_ref, carry_ref, *, tm):
    i = pl.program_id(1)
    xb = x_ref[0].astype(BF16)
    qc_ref[0] = _dot(xb, wqc_ref[...]).astype(BF16)
    kct_ref[0] = _dot(xb, wkc_ref[...]).T.astype(BF16)
    vc_ref[0] = _dot(xb, wvc_ref[...]).astype(BF16)
    gate_ref[0] = _dot(xb, wg_ref[...])

    vf_ref[0] = _spread_heads(_dot(xb, wvf_ref[...]), 1.0).astype(BF16)
    qf_ref[0] = _spread_heads(_dot(xb, wqf_ref[...]),
                              lambda lane: jnp.where((lane & (HALF - 1)) < 3, 1.0, 0.0)).astype(BF16)

    fz = _dot(xb, wfz_ref[...]) + bf_ref[...]
    r = lax.broadcasted_iota(jnp.int32, (tm, LANES), 0) + i * tm
    c = lax.broadcasted_iota(jnp.int32, (tm, LANES), 1)
    lf = jnp.where((c < N_HEADS) & (r >= LEAD), _log_sigmoid(fz), 0.0)

    def split3(t):
        a = t.astype(BF16)
        rem = t - a.astype(F32)
        bb = rem.astype(BF16)
        cc = (rem - bb.astype(F32)).astype(BF16)
        return jnp.concatenate([a, bb, cc], axis=1)

    rr = lax.broadcasted_iota(jnp.int32, (tm, tm), 0)
    cc = lax.broadcasted_iota(jnp.int32, (tm, tm), 1)
    tri = jnp.where(cc <= rr, 1.0, 0.0).astype(BF16)

    @pl.when(i == 0)
    def _():
        carry_ref[...] = jnp.zeros_like(carry_ref)

    c3 = _dot(tri, split3(lf))
    cum = c3[:, :LANES] + c3[:, LANES:2 * LANES] + c3[:, 2 * LANES:] + carry_ref[0:1, :]
    carry_ref[...] = jnp.broadcast_to(cum[tm - 1:tm, :], carry_ref.shape)
    kft_ref[0] = _spread_heads(_dot(xb, wkf_ref[...]), _dot(split3(-cum), pf_ref[...])).T.astype(BF16)


def _odd_proj(h, wts, tm):
    b, n, d = h.shape
    row = lambda j, i: (j, i, 0)
    col = lambda j, i: (j, 0, i)
    const = lambda j, i: (0, 0)
    full = lambda a: pl.BlockSpec(a.shape, const)
    wide = N_HEADS * LANES
    narrow = N_HEADS * HEAD_DIM
    outs = [(narrow, BF16, False), (SWA_KV_HEADS * HEAD_DIM, BF16, True), (narrow, BF16, False),
            (wide, BF16, False), (wide, BF16, True), (wide, BF16, False), (d, F32, False)]
    return pl.pallas_call(
        functools.partial(_odd_proj_kernel, tm=tm),
        grid=(b, n // tm),
        in_specs=[pl.BlockSpec((1, tm, d), row)] + [full(a) for a in wts],
        out_specs=[pl.BlockSpec((1, c, tm), col) if tr else pl.BlockSpec((1, tm, c), row)
                   for c, _, tr in outs],
        out_shape=[jax.ShapeDtypeStruct((b, c, n) if tr else (b, n, c), t) for c, t, tr in outs],
        scratch_shapes=[pltpu.VMEM((8, LANES), F32)],
        compiler_params=pltpu.CompilerParams(
            dimension_semantics=("parallel", "arbitrary"), vmem_limit_bytes=VMEM_LIMIT),
        name="odd_proj",
    )(h, *wts)


def _out_math(ma, mb, h, w_ref, g, b):
    half = ma.shape[-1]
    y = _dot(ma, w_ref[:half, :]) + _dot(mb, w_ref[half:, :])
    r = DN_ALPHA * h + y
    mu = jnp.mean(r, axis=-1, keepdims=True)
    cen = r - mu
    var = jnp.mean(cen * cen, axis=-1, keepdims=True)
    return cen * lax.rsqrt(var + LN_EPS) * g + b


def _out_rows(ma_ref, mb_ref, h_ref, w_ref, g_ref, b_ref, store):
    rows = ma_ref.shape[0]
    part = OUT_PART if rows % OUT_PART == 0 else rows
    for r in range(rows // part):
        sl = slice(r * part, (r + 1) * part)
        store(sl, _out_math(ma_ref[sl, :], mb_ref[sl, :], h_ref[sl, :], w_ref, g_ref[...], b_ref[...]))


def _out_kernel(ma_ref, mb_ref, h_ref, w_ref, g_ref, b_ref, o_ref):
    def store(sl, val):
        o_ref[sl, :] = val

    _out_rows(ma_ref, mb_ref, h_ref, w_ref, g_ref, b_ref, store)


def _out_frames_kernel(ma_ref, mb_ref, h_ref, w_ref, g_ref, b_ref, o_ref):
    def store(sl, val):
        o_ref[0, sl, :] = val

    _out_rows(ma_ref, mb_ref, h_ref, w_ref, g_ref, b_ref, store)


def _out_proj_frames(ma, mb, h, w, g, bias):
    b, n, d = h.shape
    half = ma.shape[-1]
    flat = lambda a: a.reshape(b * n, a.shape[-1])
    src = lambda i, j: (pl.multiple_of(i * n + BLK + j * OUT_TM, BLK), 0)
    window = lambda width: pl.BlockSpec((pl.Element(OUT_TM), pl.Element(width)), src)
    const = lambda i, j: (0, 0)
    return pl.pallas_call(
        _out_frames_kernel,
        grid=(b, (n - BLK) // OUT_TM),
        in_specs=[window(half), window(half), window(d),
                  pl.BlockSpec(w.shape, const), pl.BlockSpec((1, d), const), pl.BlockSpec((1, d), const)],
        out_specs=pl.BlockSpec((1, OUT_TM, d), lambda i, j: (i, j, 0)),
        out_shape=jax.ShapeDtypeStruct((b, n - BLK, d), F32),
        compiler_params=pltpu.CompilerParams(
            dimension_semantics=("parallel", "parallel"), vmem_limit_bytes=VMEM_LIMIT),
        name="out_proj_ln_frames",
    )(flat(ma), flat(mb), flat(h), w, g, bias)


def _out_proj(ma, mb, h, w, g, bias, tm):
    rows, d = h.shape
    half = ma.shape[-1]
    row = lambda i: (i, 0)
    const = lambda i: (0, 0)
    return pl.pallas_call(
        _out_kernel,
        grid=(rows // tm,),
        in_specs=[pl.BlockSpec((tm, half), row), pl.BlockSpec((tm, half), row),
                  pl.BlockSpec((tm, d), row),
                  pl.BlockSpec(w.shape, const), pl.BlockSpec((1, d), const), pl.BlockSpec((1, d), const)],
        out_specs=pl.BlockSpec((tm, d), row),
        out_shape=jax.ShapeDtypeStruct((rows, d), F32),
        compiler_params=pltpu.CompilerParams(
            dimension_semantics=("parallel",), vmem_limit_bytes=VMEM_LIMIT),
        name="out_proj_ln",
    )(ma, mb, h, w, g, bias)


def _by_parity(even, odd):
    is_odd = (jnp.arange(N_HEADS) % 2 == 1)[:, None]
    return jnp.where(is_odd, odd, even).reshape(even.shape[:-2] + (N_HEADS * LANES,))


def _mla_head_blocks(nope, rot):
    spare = jnp.zeros(rot.shape[:-1] + (HALF - MLA_ROPE,), rot.dtype)
    return _by_parity(jnp.concatenate([rot, spare, nope], axis=-1),
                      jnp.concatenate([nope, rot, spare], axis=-1))


def _even_weights(w_in, g_cq, g_ckv, w_uq, w_ukv):
    a = N_HEADS * HEAD_DIM
    wq, wk, wv = w_in[:, :a], w_in[:, a:2 * a], w_in[:, 2 * a:3 * a]
    o = 3 * a
    w_c = w_in[:, o:o + 2 * MLA_RANK]
    o += 2 * MLA_RANK
    w_kr = w_in[:, o:o + MLA_ROPE]
    o += MLA_ROPE
    w_g = w_in[:, o:]
    w_sb = jnp.concatenate([wq * HEAD_DIM ** -0.5, wk, wv], axis=1)
    hr = MLA_ROPE // 2
    zpad = jnp.zeros((w_in.shape[0], LANES - MLA_ROPE), w_in.dtype)
    w_kr2 = jnp.concatenate([w_kr, zpad, w_kr[:, hr:], w_kr[:, :hr], zpad], axis=1)

    uq = w_uq.reshape(MLA_RANK, N_HEADS, MLA_NOPE + MLA_ROPE)
    ukv = w_ukv.reshape(MLA_RANK, N_HEADS, MLA_NOPE + HEAD_DIM)
    nope, r1, r2 = uq[..., :MLA_NOPE], uq[..., MLA_NOPE:MLA_NOPE + hr], uq[..., MLA_NOPE + hr:]
    w_a = _mla_head_blocks(nope, jnp.concatenate([r1, r2], axis=-1))
    w_b = _mla_head_blocks(jnp.zeros_like(nope), jnp.concatenate([r2, r1], axis=-1))
    k_nope, v = ukv[..., :MLA_NOPE], ukv[..., MLA_NOPE:]
    w_kv = _by_parity(jnp.concatenate([v, k_nope], axis=-1), ukv)
    bf = lambda t: t.astype(BF16)
    return (bf(w_sb), bf(w_c), bf(w_kr2), bf(w_g), g_cq[None, :], g_ckv[None, :],
            bf(w_a), bf(w_b), bf(w_kv))


def _rope_tables(n):
    hr = MLA_ROPE // 2
    pos = jnp.arange(n, dtype=F32) - LEAD
    inv = ROPE_BASE ** (-jnp.arange(hr, dtype=F32) / hr)
    ang = pos[:, None] * inv[None, :]
    cos, sin = jnp.cos(ang), jnp.sin(ang)
    ones = jnp.ones((n, MLA_NOPE), F32)
    scale = (MLA_NOPE + MLA_ROPE) ** -0.5
    c_rot, s_rot = jnp.concatenate([cos, cos], axis=1), jnp.concatenate([-sin, sin], axis=1)
    spare = jnp.zeros((n, HALF - MLA_ROPE), F32)
    c_q = jnp.concatenate([c_rot, spare, ones, ones, c_rot, spare], axis=1) * scale
    s_q = jnp.concatenate([s_rot, spare, 0.0 * ones, 0.0 * ones, s_rot, spare], axis=1) * scale
    z96 = jnp.zeros((n, LANES - MLA_ROPE), F32)
    c_k = jnp.concatenate([c_rot, z96], axis=1)
    s_k = jnp.concatenate([s_rot, z96], axis=1)
    return c_q, s_q, c_k, s_k


def _odd_weights(w_in, b_forget):
    a = N_HEADS * HEAD_DIM
    kvw = SWA_KV_HEADS * HEAD_DIM
    wqc = w_in[:, :a]
    o = a
    wkc = w_in[:, o:o + kvw]
    o += kvw
    wvc = w_in[:, o:o + kvw]
    o += kvw
    wqd, wkd, wvd = w_in[:, o:o + a], w_in[:, o + a:o + 2 * a], w_in[:, o + 2 * a:o + 3 * a]
    o += 3 * a
    wfz = w_in[:, o:o + N_HEADS]
    o += N_HEADS
    w_g = w_in[:, o:]
    scale = HEAD_DIM ** -0.5
    v0, v1 = wvc[:, :HEAD_DIM], wvc[:, HEAD_DIM:]
    z = jnp.zeros_like(v0)
    w_vc = jnp.concatenate([v0, z, z, v0, v1, z, z, v1], axis=1)
    w_fz = jnp.concatenate([wfz, jnp.zeros((w_in.shape[0], LANES - N_HEADS), w_in.dtype)], axis=1)
    b_f = jnp.concatenate([b_forget, jnp.zeros((LANES - N_HEADS,), b_forget.dtype)])[None, :]
    bf = lambda t: t.astype(BF16)
    return (bf(wqc * scale), bf(wkc), bf(w_vc), bf(wqd * scale), bf(wkd), bf(wvd), bf(w_fz), bf(w_g),
            b_f.astype(F32))


def _piece_placement():
    p_f = np.zeros((3 * LANES, N_HEADS * LANES), np.float32)
    for t in range(3):
        for h in range(N_HEADS):
            p_f[t * LANES + h, h * LANES + (0 if h % 2 else HALF) + t] = 1.0
    return jnp.asarray(p_f, BF16)


def _row_tile(nb, cap):
    g = max(t for t in range(1, cap + 1) if nb % t == 0)
    return g * BLK


def kernel(x, meta_tokens, w_in_even, g_cq, g_ckv, w_uq, w_ukv, w_out_even,
           w_in_odd, b_forget, sink_logits, w_out_odd, ln_gain, ln_bias):
    b, s, d = x.shape
    assert s % SB_TQ == 0 and s % TQ == 0 and d == D_MODEL
    n = BLK + s
    nb = n // BLK
    h = jnp.concatenate([jnp.zeros((b, LEAD, d), x.dtype),
                         jnp.broadcast_to(meta_tokens[None].astype(x.dtype), (b, N_META, d)), x], axis=1)
    tabs = _rope_tables(n)
    tm_proj = _row_tile(nb, 3)
    tm_out = _row_tile(b * nb, OUT_TM // BLK)
    even_wts = jax.vmap(_even_weights)(w_in_even, g_cq, g_ckv, w_uq, w_ukv)
    odd_wts = jax.vmap(_odd_weights)(w_in_odd, b_forget)
    p_f = _piece_placement()
    for layer in range(DEPTH):
        i = layer // 2
        if layer % 2 == 0:
            wts = tuple(t[i] for t in even_wts)
            qs, kst, vs, qm, kmt, vm, gate = _even_proj(h, wts, tabs, tm_proj)
            mix_a = _sb_attention(qs, kst, vs, gate)
            mix_b = _softmax_attention(qm, kmt, vm, gate, "chunk")
            w_out = w_out_even[i]
        else:
            wts = tuple(t[i] for t in odd_wts) + (p_f,)
            qc, kct, vc, qf, kft, vf, gate = _odd_proj(h, wts, tm_proj)
            mix_a = _swa_attention(sink_logits[i].astype(F32), qc, kct, vc, gate)
            mix_b = _softmax_attention(qf, kft, vf, gate, "causal")
            w_out = w_out_odd[i]
        ln = (w_out.astype(BF16), ln_gain[layer][None, :], ln_bias[layer][None, :])
        if layer == DEPTH - 1:
            return _out_proj_frames(mix_a, mix_b, h, *ln)
        rows = b * n
        h = _out_proj(mix_a.reshape(rows, -1), mix_b.reshape(rows, -1), h.reshape(rows, d),
                      *ln, tm_out).reshape(b, n, d)
```

```python
import functools

import jax
import jax.numpy as jnp
import numpy as np
from jax import lax
from jax.experimental import pallas as pl
from jax.experimental.pallas import tpu as pltpu

D_MODEL = 1024
CHUNK = 64
N_META = 16
HEAD_DIM = 64
N_HEADS = 8
MLA_NOPE = 64
MLA_ROPE = 32
MLA_RANK = 256
SWA_KV_HEADS = 2
WINDOW_CHUNKS = 2
ROPE_BASE = 10000.0
DEPTH = 4
DN_ALPHA = (2 * DEPTH) ** 0.25
LN_EPS = 1e-5
RMS_EPS = 1e-6
NEG = -1e30
NO_KEY = 1 << 30
SIGN_BIT = -(1 << 31)

LANES = 128
BLK = 128
TB = 4
TQ = TB * BLK
SB_TB = 8
SB_TQ = SB_TB * BLK
SB_UNROLL = 8
SWA_UNROLL = 3
OUT_TM = 1024
OUT_PART = 256
LEAD = BLK - N_META
HALF = LANES // 2
VMEM_LIMIT = 56 * 1024 * 1024

F32 = jnp.float32
BF16 = jnp.bfloat16


def _dot(a, b):
    return jnp.dot(a, b, preferred_element_type=F32)


def _chunk_of(pos):
    return jnp.maximum((pos >> 6) - 1, 0)


def _log_sigmoid(x):
    return jnp.minimum(x, 0.0) - jnp.log(1.0 + jnp.exp(-jnp.abs(x)))


def _al(x):
    return x if isinstance(x, int) else pl.multiple_of(x, BLK)


def _gated(mixed, gate):
    return (mixed * (gate * jax.nn.sigmoid(gate))).astype(BF16)


def _iotas(rows):
    return (lax.broadcasted_iota(jnp.int32, (rows, BLK), 0),
            lax.broadcasted_iota(jnp.int32, (rows, BLK), 1))


def _sb_kernel(q_ref, kt_ref, v_ref, g_ref, o_ref, u_ref, carry_ref, acc_ref, *, nt):
    r2 = lax.broadcasted_iota(jnp.int32, (2 * BLK, 2 * BLK), 0)
    c2 = lax.broadcasted_iota(jnp.int32, (2 * BLK, 2 * BLK), 1)
    u_ref[...] = jnp.where(((r2 >= BLK) == (c2 >= BLK)) & ((r2 & (BLK - 1)) > (c2 & (BLK - 1))),
                           1.0, 0.0).astype(BF16)
    zero_kt = jnp.zeros((HALF, BLK), BF16)

    def step(row0, off, rows, jb, mask=None):
        r0 = _al(row0 + off)
        k0 = _al(jb * BLK)
        sl = slice(off, off + rows)
        ktb = kt_ref[0, :, pl.ds(k0, BLK)]
        kbd = jnp.concatenate([jnp.concatenate([ktb[:HALF], zero_kt], axis=1),
                               jnp.concatenate([zero_kt, ktb[HALF:]], axis=1)], axis=0)
        z2 = _dot(q_ref[0, pl.ds(r0, rows), :], kbd)
        vst = jnp.concatenate([v_ref[0, pl.ds(k0, BLK), :LANES],
                               v_ref[0, pl.ds(k0, BLK), LANES:]], axis=0)
        if mask == "meta_keys":
            parts = [(0, rows, _iotas(rows)[1] >= LEAD)]
        elif mask is not None:
            rowi, lane = _iotas(BLK)
            local = lane < rowi
            parts = [(0, BLK, local if mask == "diagonal" else local & (lane >= LEAD))]
            if rows > BLK:
                parts.append((BLK, rows, None))

        def masked(x):
            if mask is None:
                return x
            return jnp.concatenate([x[a:b] if m is None else jnp.where(m, x[a:b], 0.0)
                                    for a, b, m in parts], axis=0)

        stays, log_betas = [], []
        for h in range(2):
            z = z2[:, h * BLK:(h + 1) * BLK]
            neg_abs = lax.bitcast_convert_type(
                lax.bitcast_convert_type(z, jnp.int32) | SIGN_BIT, F32)
            stay = jnp.maximum(z, 0.0) + jnp.log(1.0 + jnp.exp(neg_abs))
            log_betas.append(z - stay)
            stays.append(masked(stay))
        cs = _dot(jnp.concatenate([s.astype(BF16) for s in stays], axis=1), u_ref[...])
        ws = []
        for h in range(2):
            carry = carry_ref[h, sl, :]
            w = masked(jnp.exp(log_betas[h] - (cs[:, h * BLK:(h + 1) * BLK] + carry)))
            carry_ref[h, sl, :] = carry + jnp.sum(stays[h], axis=1, keepdims=True)
            ws.append(w.astype(BF16))
        acc_ref[sl, :] += _dot(jnp.concatenate(ws, axis=1), vst)

    def reset(rows):
        carry_ref[:, :rows, :] = jnp.zeros((2, rows, BLK), F32)
        acc_ref[:rows, :] = jnp.zeros((rows, BLK), F32)

    reset(BLK)
    step(0, 0, BLK, 0, "meta_tile")
    o_ref[0, :BLK, :] = _gated(acc_ref[:BLK, :], g_ref[0, :BLK, :])

    def tile(t, _):
        row0 = BLK + t * SB_TQ
        jb0 = 1 + t * SB_TB
        reset(SB_TQ)
        for d in reversed(range(SB_TB)):
            step(row0, d * BLK, SB_TQ - d * BLK, jb0 + d, "diagonal")

        def body(i, c):
            for u in range(SB_UNROLL):
                step(row0, 0, SB_TQ, jb0 - 1 - (i * SB_UNROLL + u))
            return c

        lax.fori_loop(0, t * (SB_TB // SB_UNROLL), body, 0)
        step(row0, 0, SB_TQ, 0, "meta_keys")
        o_ref[0, pl.ds(_al(row0), SB_TQ), :] = _gated(acc_ref[...], g_ref[0, pl.ds(_al(row0), SB_TQ), :])
        return 0

    lax.fori_loop(0, nt, tile, 0)


def _sb_attention(q, kt, v, gate):
    b, n, w = q.shape
    nt = (n - BLK) // SB_TQ
    return pl.pallas_call(
        functools.partial(_sb_kernel, nt=nt),
        grid=(b, w // LANES),
        in_specs=[pl.BlockSpec((1, n, LANES), lambda i, j: (i, 0, j)),
                  pl.BlockSpec((1, LANES, n), lambda i, j: (i, j, 0)),
                  pl.BlockSpec((1, n, 2 * LANES), lambda i, j: (i, 0, j)),
                  pl.BlockSpec((1, n, LANES), lambda i, j: (i, 0, j))],
        out_specs=pl.BlockSpec((1, n, LANES), lambda i, j: (i, 0, j)),
        out_shape=jax.ShapeDtypeStruct((b, n, w), BF16),
        scratch_shapes=[pltpu.VMEM((2 * BLK, 2 * BLK), BF16),
                        pltpu.VMEM((2, SB_TQ, BLK), F32),
                        pltpu.VMEM((SB_TQ, BLK), F32)],
        compiler_params=pltpu.CompilerParams(
            dimension_semantics=("parallel", "parallel"), vmem_limit_bytes=VMEM_LIMIT),
        name="sb_attention",
    )(q, kt, v, gate)


def _softmax_kernel(q_ref, kt_ref, v_ref, g_ref, o_ref, s_ref, so_ref, sd_ref, mx_ref, mb_ref, acc_ref,
                    bias_ref, *, nt, mode):
    def visible(kpos, qpos, lo):
        if mode == "chunk":
            return (kpos >= lo) & (_chunk_of(kpos) <= _chunk_of(qpos))
        return (kpos >= lo) & (kpos <= qpos)

    ql = BLK + lax.broadcasted_iota(jnp.int32, (TQ, TQ), 0)
    kl = BLK + lax.broadcasted_iota(jnp.int32, (TQ, TQ), 1)
    bias_ref[...] = jnp.where(visible(kl, ql, 0), 0.0, NEG)

    def chunk_loop(n_chunks, chunk_fn):
        def quad(i, carry):
            for u in range(4):
                chunk_fn(4 * i + u)
            return carry

        lax.fori_loop(0, n_chunks >> 2, quad, 0)
        done = (n_chunks >> 2) << 2

        @pl.when((n_chunks & 2) != 0)
        def _():
            chunk_fn(done)
            chunk_fn(done + 1)

        @pl.when((n_chunks & 1) != 0)
        def _():
            chunk_fn(n_chunks - 1)

    def emit(row0, rows, n_chunks, frames):
        frs = [slice(h * LANES, (h + 1) * LANES) for h in range(2)]
        r0 = _al(row0)
        rowi, lane = _iotas(rows)

        def chunk_scores(c, h):
            k0 = _al(BLK + c * TQ)
            return _dot(q_ref[0, pl.ds(r0, rows), frs[h]], kt_ref[0, frs[h], pl.ds(k0, TQ)])

        def fold_max(c, h, s):
            s_ref[h, :, pl.ds(_al(BLK + c * TQ), TQ)] = s
            m4 = jnp.maximum(jnp.maximum(s[:, :BLK], s[:, BLK:2 * BLK]),
                             jnp.maximum(s[:, 2 * BLK:3 * BLK], s[:, 3 * BLK:]))
            mx_ref[h] = jnp.maximum(mx_ref[h], m4)

        for h, fr in enumerate(frs):
            s0 = _dot(q_ref[0, pl.ds(r0, rows), fr], kt_ref[0, fr, :BLK])
            s0 = jnp.where(visible(lane, r0 + rowi, LEAD), s0, NEG)
            sd_ref[h, :rows, :] = s0
            mx_ref[h, :rows, :] = s0
        own0 = BLK + n_chunks * TQ
        own_parts = ((0, TQ // 2, TQ // 2), (TQ // 2, TQ, TQ))
        if frames:
            for h, fr in enumerate(frs):
                for a, b, nk in own_parts:
                    s = _dot(q_ref[0, pl.ds(_al(r0 + a), b - a), fr], kt_ref[0, fr, pl.ds(_al(own0), nk)])
                    s = s + bias_ref[a:b, :nk]
                    so_ref[h, a:b, :nk] = s
                    m = s[:, :BLK]
                    for j in range(1, nk // BLK):
                        m = jnp.maximum(m, s[:, j * BLK:(j + 1) * BLK])
                    mx_ref[h, a:b, :] = jnp.maximum(mx_ref[h, a:b, :], m)

            def p1(c):
                for h in range(2):
                    fold_max(c, h, chunk_scores(c, h))

            chunk_loop(n_chunks, p1)

        for h, fr in enumerate(frs):
            m_row = jnp.max(mx_ref[h, :rows, :], axis=1, keepdims=True)
            mb_ref[h, :rows, :] = jnp.broadcast_to(m_row, (rows, BLK))
            p0 = jnp.exp(sd_ref[h, :rows, :] - mb_ref[h, :rows, :]).astype(BF16)
            acc_ref[h, :rows, :] = _dot(p0, v_ref[0, :BLK, fr])
            if frames:
                for a, b, nk in own_parts:
                    mb = mb_ref[h, a:b, :]
                    p = jnp.exp(so_ref[h, a:b, :nk] - jnp.concatenate([mb] * (nk // BLK), axis=1)).astype(BF16)
                    acc_ref[h, a:b, :] += _dot(p, v_ref[0, pl.ds(_al(own0), nk), fr])

        if frames:
            def p2(c):
                k0 = _al(BLK + c * TQ)
                for h, fr in enumerate(frs):
                    mb = mb_ref[h]
                    p = jnp.exp(s_ref[h, :, pl.ds(k0, TQ)]
                                - jnp.concatenate([mb] * TB, axis=1)).astype(BF16)
                    acc_ref[h] += _dot(p, v_ref[0, pl.ds(k0, TQ), fr])

            chunk_loop(n_chunks, p2)

        outs = []
        for h in range(2):
            acc = acc_ref[h, :rows, :]
            outs.append(acc / pltpu.roll(acc, HALF, 1))
        o_ref[0, pl.ds(r0, rows), :] = _gated(jnp.where(lane < HALF, outs[0], outs[1]),
                                              g_ref[0, pl.ds(r0, rows), :])

    emit(0, BLK, 0, False)

    def tile(t, _):
        emit(BLK + t * TQ, TQ, t, True)
        return 0

    lax.fori_loop(0, nt, tile, 0)


def _softmax_attention(q, kt, v, gate, mode):
    b, n, w = q.shape
    nt = (n - BLK) // TQ
    pairs = w // (2 * LANES)
    return pl.pallas_call(
        functools.partial(_softmax_kernel, nt=nt, mode=mode),
        grid=(b, pairs),
        in_specs=[pl.BlockSpec((1, n, 2 * LANES), lambda i, j: (i, 0, j)),
                  pl.BlockSpec((1, 2 * LANES, n), lambda i, j: (i, j, 0)),
                  pl.BlockSpec((1, n, 2 * LANES), lambda i, j: (i, 0, j)),
                  pl.BlockSpec((1, n, LANES), lambda i, j: (i, 0, pairs + j))],
        out_specs=pl.BlockSpec((1, n, LANES), lambda i, j: (i, 0, j)),
        out_shape=jax.ShapeDtypeStruct((b, n, w // 2), BF16),
        scratch_shapes=[pltpu.VMEM((2, TQ, n), F32),
                        pltpu.VMEM((2, TQ, TQ), F32),
                        pltpu.VMEM((2, TQ, BLK), F32),
                        pltpu.VMEM((2, TQ, BLK), F32),
                        pltpu.VMEM((2, TQ, BLK), F32),
                        pltpu.VMEM((2, TQ, BLK), F32),
                        pltpu.VMEM((TQ, TQ), F32)],
        compiler_params=pltpu.CompilerParams(
            dimension_semantics=("parallel", "parallel"), vmem_limit_bytes=VMEM_LIMIT),
        name="softmax_attention_" + mode,
    )(q, kt, v, gate)


def _swa_kernel(sink_ref, q_ref, kt_ref, v_ref, g_ref, o_ref, bias_ref, *, nb):
    rowi, lane = _iotas(BLK)
    group = N_HEADS // SWA_KV_HEADS
    zero_kt = jnp.zeros((HALF, BLK), BF16)
    r2 = lax.broadcasted_iota(jnp.int32, (2 * BLK, BLK), 0)
    c2 = lax.broadcasted_iota(jnp.int32, (2 * BLK, BLK), 1)
    ones_st = jnp.where((r2 >= BLK) == (c2 >= HALF), 1.0, 0.0).astype(BF16)
    slopes = [2.0 ** (-8.0 * (hd + 1) / N_HEADS) for hd in range(N_HEADS)]
    GENERIC = 2

    def key_slots(m):
        return ((0, jnp.where(m >= 1, LEAD, NO_KEY)),
                (jnp.maximum(m - 1, 0), jnp.where(m >= 2, 0, NO_KEY)),
                (m, LEAD))

    def slot_biases(m):
        qpos = m * BLK + rowi
        cq = _chunk_of(qpos)
        out = []
        for jb, lo in key_slots(m):
            kpos = jb * BLK + lane
            ck = _chunk_of(kpos)
            msk = (kpos >= lo) & (ck <= cq) & ((cq - ck <= WINDOW_CHUNKS) | (kpos < BLK))
            out.append((msk, jnp.abs(qpos - kpos).astype(F32)))
        return out

    for slot, (msk, dist) in enumerate(slot_biases(GENERIC)):
        for hd in range(N_HEADS):
            bias_ref[hd, slot] = jnp.where(msk, -slopes[hd] * dist, NEG)

    def q_block(m, generic):
        r0 = _al(m * BLK)
        slots = key_slots(m)
        if generic:
            extra = jnp.float32(BLK) * (m - GENERIC)
            biases = lambda hd: [bias_ref[hd, 0] - slopes[hd] * extra, bias_ref[hd, 1], bias_ref[hd, 2]]
        else:
            special = slot_biases(m)
            biases = lambda hd: [jnp.where(msk, -slopes[hd] * dist, NEG) for msk, dist in special]
        for g in range(SWA_KV_HEADS):
            kbds, wcats = [], []
            for jb, _ in slots:
                k0 = _al(jb * BLK)
                ktg =kt_ref[0, g * HALF:(g + 1) * HALF, pl.ds(k0, BLK)]
                kbds.append(jnp.concatenate([jnp.concatenate([ktg, zero_kt], axis=1),
                                             jnp.concatenate([zero_kt, ktg], axis=1)], axis=0))
                vst = jnp.concatenate([v_ref[0, pl.ds(k0, BLK), (2 * g) * LANES:(2 * g + 1) * LANES],
                                       v_ref[0, pl.ds(k0, BLK), (2 * g + 1) * LANES:(2 * g + 2) * LANES]],
                                      axis=0)
                wcats.append(jnp.concatenate([vst, ones_st], axis=1))
            for pr in range(group // 2):
                pair = g * (group // 2) + pr
                q2 = q_ref[0, pl.ds(r0, BLK), pair * LANES:(pair + 1) * LANES]
                zs = [_dot(q2, kbd) for kbd in kbds]
                ps, sink_terms = [[], []], []
                for h in range(2):
                    head = 2 * pair + h
                    ss = [z[:, h * BLK:(h + 1) * BLK] + bias for z, bias in zip(zs, biases(head))]
                    sink = sink_ref[head]
                    mx = jnp.maximum(jnp.maximum(ss[0], ss[1]), ss[2])
                    m_row = jnp.maximum(jnp.max(mx, axis=1, keepdims=True), sink)
                    ps[h] = [jnp.exp(s - m_row).astype(BF16) for s in ss]
                    sink_terms.append(jnp.exp(sink - m_row))
                res = jnp.zeros((BLK, 2 * BLK), F32)
                for i in range(len(slots)):
                    res = res + _dot(jnp.concatenate([ps[0][i], ps[1][i]], axis=1), wcats[i])
                denom = res[:, BLK:] + jnp.where(lane < HALF, sink_terms[0], sink_terms[1])
                cols = slice(pair * LANES, (pair + 1) * LANES)
                o_ref[0, pl.ds(r0, BLK), cols] = _gated(res[:, :BLK] / denom, g_ref[0, pl.ds(r0, BLK), cols])

    for m in range(GENERIC):
        q_block(m, False)

    def q_blocks(i, carry):
        for u in range(SWA_UNROLL):
            q_block(GENERIC + SWA_UNROLL * i + u, True)
        return carry

    n_generic = nb - GENERIC
    lax.fori_loop(0, n_generic // SWA_UNROLL, q_blocks, 0)
    for m in range(nb - n_generic % SWA_UNROLL, nb):
        q_block(jnp.int32(m), True)


def _swa_attention(sinks, q, kt, v, gate):
    b, n, w = q.shape
    nb = n // BLK
    return pl.pallas_call(
        functools.partial(_swa_kernel, nb=nb),
        grid=(b,),
        in_specs=[
            pl.BlockSpec(memory_space=pltpu.SMEM),
            pl.BlockSpec((1, n, w), lambda i: (i, 0, 0)),
            pl.BlockSpec((1, kt.shape[1], n), lambda i: (i, 0, 0)),
            pl.BlockSpec((1, n, v.shape[2]), lambda i: (i, 0, 0)),
            pl.BlockSpec((1, n, w), lambda i: (i, 0, 0)),
        ],
        out_specs=pl.BlockSpec((1, n, w), lambda i: (i, 0, 0)),
        out_shape=jax.ShapeDtypeStruct((b, n, w), BF16),
        scratch_shapes=[pltpu.VMEM((N_HEADS, 3, BLK, BLK), F32)],
        compiler_params=pltpu.CompilerParams(
            dimension_semantics=("parallel",), vmem_limit_bytes=VMEM_LIMIT),
        name="swa_attention",
    )(sinks, q, kt, v, gate)


def _spread_heads(x, fill):
    lane = lax.broadcasted_iota(jnp.int32, (x.shape[0], LANES), 1)
    blocks = []
    for p in range(x.shape[1] // LANES):
        xp = x[:, p * LANES:(p + 1) * LANES]
        for odd in range(2):
            blk = 2 * p + odd
            if isinstance(fill, float):
                f = fill
            elif callable(fill):
                f = fill(lane)
            else:
                f = fill[:, blk * LANES:(blk + 1) * LANES]
            blocks.append(jnp.where((lane >= HALF) if odd else (lane < HALF), xp, f))
    return jnp.concatenate(blocks, axis=1)


def _even_proj_kernel(x_ref, wsb_ref, wc_ref, wkr_ref, wg_ref, gq_ref, gkv_ref,
                      wa_ref, wb_ref, wkv_ref, cq_ref, sq_ref, ck_ref, sk_ref,
                      qs_ref, kst_ref, vs_ref, qm_ref, kmt_ref, vm_ref, gate_ref):
    xb = x_ref[0].astype(BF16)
    w = N_HEADS * HEAD_DIM
    sb = _dot(xb, wsb_ref[...])
    qs_ref[0] = sb[:, :w].astype(BF16)
    kst_ref[0] = sb[:, w:2 * w].T.astype(BF16)
    vs_ref[0] = _spread_heads(sb[:, 2 * w:], 0.0).astype(BF16)
    gate_ref[0] = _dot(xb, wg_ref[...])

    c = _dot(xb, wc_ref[...])

    def rms(t, g):
        return (t * lax.rsqrt(jnp.mean(t * t, axis=-1, keepdims=True) + RMS_EPS) * g).astype(BF16)

    xq = rms(c[:, :MLA_RANK], gq_ref[...])
    xkv = rms(c[:, MLA_RANK:], gkv_ref[...])
    pairs = N_HEADS // 2
    cq = jnp.concatenate([cq_ref[...]] * pairs, axis=1)
    sq = jnp.concatenate([sq_ref[...]] * pairs, axis=1)
    qm_ref[0] = (_dot(xq, wa_ref[...]) * cq + _dot(xq, wb_ref[...]) * sq).astype(BF16)
    kr = _dot(xb, wkr_ref[...])
    krr = kr[:, :LANES] * ck_ref[...] + kr[:, LANES:] * sk_ref[...]
    krr = (krr, pltpu.roll(krr, HALF, 1))
    kv = _dot(xkv, wkv_ref[...])
    lane = lax.broadcasted_iota(jnp.int32, (kv.shape[0], LANES), 1)
    kblocks, vblocks = [], []
    for hd in range(N_HEADS):
        blk = slice(hd * LANES, (hd + 1) * LANES)
        v_half = (lane >= HALF) if hd % 2 else (lane < HALF)
        kblocks.append(jnp.where(v_half, krr[hd % 2], kv[:, blk]))
        vblocks.append(jnp.where(v_half, kv[:, blk], 1.0))
    kmt_ref[0] = jnp.concatenate(kblocks, axis=1).T.astype(BF16)
    vm_ref[0] = jnp.concatenate(vblocks, axis=1).astype(BF16)


def _even_proj(h, wts, tabs, tm):
    b, n, d = h.shape
    grid = (n // tm, b)
    row = lambda i, j: (j, i, 0)
    col = lambda i, j: (j, 0, i)
    const = lambda i, j: (0, 0)
    tab = lambda i, j: (i, 0)
    full = lambda a: pl.BlockSpec(a.shape, const)
    wide = N_HEADS * LANES
    narrow = N_HEADS * HEAD_DIM
    outs = [(narrow, BF16, False), (narrow, BF16, True), (wide, BF16, False),
            (wide, BF16, False), (wide, BF16, True), (wide, BF16, False), (d, F32, False)]
    return pl.pallas_call(
        _even_proj_kernel,
        grid=grid,
        in_specs=[pl.BlockSpec((1, tm, d), row)] + [full(a) for a in wts] + [
            pl.BlockSpec((tm, 2 * LANES), tab), pl.BlockSpec((tm, 2 * LANES), tab),
            pl.BlockSpec((tm, LANES), tab), pl.BlockSpec((tm, LANES), tab)],
        out_specs=[pl.BlockSpec((1, c, tm), col) if tr else pl.BlockSpec((1, tm, c), row)
                   for c, _, tr in outs],
        out_shape=[jax.ShapeDtypeStruct((b, c, n) if tr else (b, n, c), t) for c, t, tr in outs],
        compiler_params=pltpu.CompilerParams(
            dimension_semantics=("parallel", "parallel"), vmem_limit_bytes=VMEM_LIMIT),
        name="even_proj",
    )(h, *wts, *tabs)


def _odd_proj_kernel(x_ref, wqc_ref, wkc_ref, wvc_ref, wqf_ref, wkf_ref, wvf_ref, wfz_ref, wg_ref,
                     bf_ref, pf_ref,
                     qc_ref, kct_ref, vc_ref, qf_ref, kft_ref, vf_ref, gate_ref, carry_ref, *, tm):
    i = pl.program_id(1)
    xb = x_ref[0].astype(BF16)
    qc_ref[0] = _dot(xb, wqc_ref[...]).astype(BF16)
    kct_ref[0] = _dot(xb, wkc_ref[...]).T.astype(BF16)
    vc_ref[0] = _dot(xb, wvc_ref[...]).astype(BF16)
    gate_ref[0] = _dot(xb, wg_ref[...])

    vf_ref[0] = _spread_heads(_dot(xb, wvf_ref[...]), 1.0).astype(BF16)
    qf_ref[0] = _spread_heads(_dot(xb, wqf_ref[...]),
                              lambda lane: jnp.where((lane & (HALF - 1)) < 3, 1.0, 0.0)).astype(BF16)

    fz = _dot(xb, wfz_ref[...]) + bf_ref[...]
    r = lax.broadcasted_iota(jnp.int32, (tm, LANES), 0) + i * tm
    c = lax.broadcasted_iota(jnp.int32, (tm, LANES), 1)
    lf = jnp.where((c < N_HEADS) & (r >= LEAD), _log_sigmoid(fz), 0.0)

    def split3(t):
        a = t.astype(BF16)
        rem = t - a.astype(F32)
        bb = rem.astype(BF16)
        cc = (rem - bb.astype(F32)).astype(BF16)
        return jnp.concatenate([a, bb, cc], axis=1)

    rr = lax.broadcasted_iota(jnp.int32, (tm, tm), 0)
    cc = lax.broadcasted_iota(jnp.int32, (tm, tm), 1)
    tri = jnp.where(cc <= rr, 1.0, 0.0).astype(BF16)

    @pl.when(i == 0)
    def _():
        carry_ref[...] = jnp.zeros_like(carry_ref)

    c3 = _dot(tri, split3(lf))
    cum = c3[:, :LANES] + c3[:, LANES:2 * LANES] + c3[:, 2 * LANES:] + carry_ref[0:1, :]
    carry_ref[...] = jnp.broadcast_to(cum[tm - 1:tm, :], carry_ref.shape)
    kft_ref[0] = _spread_heads(_dot(xb, wkf_ref[...]), _dot(split3(-cum), pf_ref[...])).T.astype(BF16)


def _odd_proj(h, wts, tm):
    b, n, d = h.shape
    row = lambda j, i: (j, i, 0)
    col = lambda j, i: (j, 0, i)
    const = lambda j, i: (0, 0)
    full = lambda a: pl.BlockSpec(a.shape, const)
    wide = N_HEADS * LANES
    narrow = N_HEADS * HEAD_DIM
    outs = [(narrow, BF16, False), (SWA_KV_HEADS * HEAD_DIM, BF16, True), (narrow, BF16, False),
            (wide, BF16, False), (wide, BF16, True), (wide, BF16, False), (d, F32, False)]
    return pl.pallas_call(
        functools.partial(_odd_proj_kernel, tm=tm),
        grid=(b, n // tm),
        in_specs=[pl.BlockSpec((1, tm, d), row)] + [full(a) for a in wts],
        out_specs=[pl.BlockSpec((1, c, tm), col) if tr else pl.BlockSpec((1, tm, c), row)
                   for c, _, tr in outs],
        out_shape=[jax.ShapeDtypeStruct((b, c, n) if tr else (b, n, c), t) for c, t, tr in outs],
        scratch_shapes=[pltpu.VMEM((8, LANES), F32)],
        compiler_params=pltpu.CompilerParams(
            dimension_semantics=("parallel", "arbitrary"), vmem_limit_bytes=VMEM_LIMIT),
        name="odd_proj",
    )(h, *wts)


def _out_math(ma, mb, h, w_ref, g, b):
    half = ma.shape[-1]
    y = _dot(ma, w_ref[:half, :]) + _dot(mb, w_ref[half:, :])
    r = DN_ALPHA * h + y
    mu = jnp.mean(r, axis=-1, keepdims=True)
    cen = r - mu
    var = jnp.mean(cen * cen, axis=-1, keepdims=True)
    return cen * lax.rsqrt(var + LN_EPS) * g + b


def _out_rows(ma_ref, mb_ref, h_ref, w_ref, g_ref, b_ref, store):
    rows = ma_ref.shape[0]
    part = OUT_PART if rows % OUT_PART == 0 else rows
    for r in range(rows // part):
        sl = slice(r * part, (r + 1) * part)
        store(sl, _out_math(ma_ref[sl, :], mb_ref[sl, :], h_ref[sl, :], w_ref, g_ref[...], b_ref[...]))


def _out_kernel(ma_ref, mb_ref, h_ref, w_ref, g_ref, b_ref, o_ref):
    def store(sl, val):
        o_ref[sl, :] = val

    _out_rows(ma_ref, mb_ref, h_ref, w_ref, g_ref, b_ref, store)


def _out_frames_kernel(ma_ref, mb_ref, h_ref, w_ref, g_ref, b_ref, o_ref):
    def store(sl, val):
        o_ref[0, sl, :] = val

    _out_rows(ma_ref, mb_ref, h_ref, w_ref, g_ref, b_ref, store)


def _out_proj_frames(ma, mb, h, w, g, bias):
    b, n, d = h.shape
    half = ma.shape[-1]
    flat = lambda a: a.reshape(b * n, a.shape[-1])
    src = lambda i, j: (pl.multiple_of(i * n + BLK + j * OUT_TM, BLK), 0)
    window = lambda width: pl.BlockSpec((pl.Element(OUT_TM), pl.Element(width)), src)
    const = lambda i, j: (0, 0)
    return pl.pallas_call(
        _out_frames_kernel,
        grid=(b, (n - BLK) // OUT_TM),
        in_specs=[window(half), window(half), window(d),
                  pl.BlockSpec(w.shape, const), pl.BlockSpec((1, d), const), pl.BlockSpec((1, d), const)],
        out_specs=pl.BlockSpec((1, OUT_TM, d), lambda i, j: (i, j, 0)),
        out_shape=jax.ShapeDtypeStruct((b, n - BLK, d), F32),
        compiler_params=pltpu.CompilerParams(
            dimension_semantics=("parallel", "parallel"), vmem_limit_bytes=VMEM_LIMIT),
        name="out_proj_ln_frames",
    )(flat(ma), flat(mb), flat(h), w, g, bias)


def _out_proj(ma, mb, h, w, g, bias, tm):
    rows, d = h.shape
    half = ma.shape[-1]
    row = lambda i: (i, 0)
    const = lambda i: (0, 0)
    return pl.pallas_call(
        _out_kernel,
        grid=(rows // tm,),
        in_specs=[pl.BlockSpec((tm, half), row), pl.BlockSpec((tm, half), row),
                  pl.BlockSpec((tm, d), row),
                  pl.BlockSpec(w.shape, const), pl.BlockSpec((1, d), const), pl.BlockSpec((1, d), const)],
        out_specs=pl.BlockSpec((tm, d), row),
        out_shape=jax.ShapeDtypeStruct((rows, d), F32),
        compiler_params=pltpu.CompilerParams(
            dimension_semantics=("parallel",), vmem_limit_bytes=VMEM_LIMIT),
        name="out_proj_ln",
    )(ma, mb, h, w, g, bias)


def _by_parity(even, odd):
    is_odd = (jnp.arange(N_HEADS) % 2 == 1)[:, None]
    return jnp.where(is_odd, odd, even).reshape(even.shape[:-2] + (N_HEADS * LANES,))


def _mla_head_blocks(nope, rot):
    spare = jnp.zeros(rot.shape[:-1] + (HALF - MLA_ROPE,), rot.dtype)
    return _by_parity(jnp.concatenate([rot, spare, nope], axis=-1),
                      jnp.concatenate([nope, rot, spare], axis=-1))


def _even_weights(w_in, g_cq, g_ckv, w_uq, w_ukv):
    a = N_HEADS * HEAD_DIM
    wq, wk, wv = w_in[:, :a], w_in[:, a:2 * a], w_in[:, 2 * a:3 * a]
    o = 3 * a
    w_c = w_in[:, o:o + 2 * MLA_RANK]
    o += 2 * MLA_RANK
    w_kr = w_in[:, o:o + MLA_ROPE]
    o += MLA_ROPE
    w_g = w_in[:, o:]
    w_sb = jnp.concatenate([wq * HEAD_DIM ** -0.5, wk, wv], axis=1)
    hr = MLA_ROPE // 2
    zpad = jnp.zeros((w_in.shape[0], LANES - MLA_ROPE), w_in.dtype)
    w_kr2 = jnp.concatenate([w_kr, zpad, w_kr[:, hr:], w_kr[:, :hr], zpad], axis=1)

    uq = w_uq.reshape(MLA_RANK, N_HEADS, MLA_NOPE + MLA_ROPE)
    ukv = w_ukv.reshape(MLA_RANK, N_HEADS, MLA_NOPE + HEAD_DIM)
    nope, r1, r2 = uq[..., :MLA_NOPE], uq[..., MLA_NOPE:MLA_NOPE + hr], uq[..., MLA_NOPE + hr:]
    w_a = _mla_head_blocks(nope, jnp.concatenate([r1, r2], axis=-1))
    w_b = _mla_head_blocks(jnp.zeros_like(nope), jnp.concatenate([r2, r1], axis=-1))
    k_nope, v = ukv[..., :MLA_NOPE], ukv[..., MLA_NOPE:]
    w_kv = _by_parity(jnp.concatenate([v, k_nope], axis=-1), ukv)
    bf = lambda t: t.astype(BF16)
    return (bf(w_sb), bf(w_c), bf(w_kr2), bf(w_g), g_cq[None, :], g_ckv[None, :],
            bf(w_a), bf(w_b), bf(w_kv))


def _rope_tables(n):
    hr = MLA_ROPE // 2
    pos = jnp.arange(n, dtype=F32) - LEAD
    inv = ROPE_BASE ** (-jnp.arange(hr, dtype=F32) / hr)
    ang = pos[:, None] * inv[None, :]
    cos, sin = jnp.cos(ang), jnp.sin(ang)
    ones = jnp.ones((n, MLA_NOPE), F32)
    scale = (MLA_NOPE + MLA_ROPE) ** -0.5
    c_rot, s_rot = jnp.concatenate([cos, cos], axis=1), jnp.concatenate([-sin, sin], axis=1)
    spare = jnp.zeros((n, HALF - MLA_ROPE), F32)
    c_q = jnp.concatenate([c_rot, spare, ones, ones, c_rot, spare], axis=1) * scale
    s_q = jnp.concatenate([s_rot, spare, 0.0 * ones, 0.0 * ones, s_rot, spare], axis=1) * scale
    z96 = jnp.zeros((n, LANES - MLA_ROPE), F32)
    c_k = jnp.concatenate([c_rot, z96], axis=1)
    s_k = jnp.concatenate([s_rot, z96], axis=1)
    return c_q, s_q, c_k, s_k


def _odd_weights(w_in, b_forget):
    a = N_HEADS * HEAD_DIM
    kvw = SWA_KV_HEADS * HEAD_DIM
    wqc = w_in[:, :a]
    o = a
    wkc = w_in[:, o:o + kvw]
    o += kvw
    wvc = w_in[:, o:o + kvw]
    o += kvw
    wqd, wkd, wvd = w_in[:, o:o + a], w_in[:, o + a:o + 2 * a], w_in[:, o + 2 * a:o + 3 * a]
    o += 3 * a
    wfz = w_in[:, o:o + N_HEADS]
    o += N_HEADS
    w_g = w_in[:, o:]
    scale = HEAD_DIM ** -0.5
    v0, v1 = wvc[:, :HEAD_DIM], wvc[:, HEAD_DIM:]
    z = jnp.zeros_like(v0)
    w_vc = jnp.concatenate([v0, z, z, v0, v1, z, z, v1], axis=1)
    w_fz = jnp.concatenate([wfz, jnp.zeros((w_in.shape[0], LANES - N_HEADS), w_in.dtype)], axis=1)
    b_f = jnp.concatenate([b_forget, jnp.zeros((LANES - N_HEADS,), b_forget.dtype)])[None, :]
    bf = lambda t: t.astype(BF16)
    return (bf(wqc * scale), bf(wkc), bf(w_vc), bf(wqd * scale), bf(wkd), bf(wvd), bf(w_fz), bf(w_g),
            b_f.astype(F32))


def _piece_placement():
    p_f = np.zeros((3 * LANES, N_HEADS * LANES), np.float32)
    for t in range(3):
        for h in range(N_HEADS):
            p_f[t * LANES + h, h * LANES + (0 if h % 2 else HALF) + t] = 1.0
    return jnp.asarray(p_f, BF16)


def _row_tile(nb, cap):
    g = max(t for t in range(1, cap + 1) if nb % t == 0)
    return g * BLK


def kernel(x, meta_tokens, w_in_even, g_cq, g_ckv, w_uq, w_ukv, w_out_even,
           w_in_odd, b_forget, sink_logits, w_out_odd, ln_gain, ln_bias):
    b, s, d = x.shape
    assert s % SB_TQ == 0 and s % TQ == 0 and d == D_MODEL
    n = BLK + s
    nb = n // BLK
    h = jnp.concatenate([jnp.zeros((b, LEAD, d), x.dtype),
                         jnp.broadcast_to(meta_tokens[None].astype(x.dtype), (b, N_META, d)), x], axis=1)
    tabs = _rope_tables(n)
    tm_proj = _row_tile(nb, 3)
    tm_out = _row_tile(b * nb, OUT_TM // BLK)
    even_wts = jax.vmap(_even_weights)(w_in_even, g_cq, g_ckv, w_uq, w_ukv)
    odd_wts = jax.vmap(_odd_weights)(w_in_odd, b_forget)
    p_f = _piece_placement()
    for layer in range(DEPTH):
        i = layer // 2
        if layer % 2 == 0:
            wts = tuple(t[i] for t in even_wts)
            qs, kst, vs, qm, kmt, vm, gate = _even_proj(h, wts, tabs, tm_proj)
            mix_a = _sb_attention(qs, kst, vs, gate)
            mix_b = _softmax_attention(qm, kmt, vm, gate, "chunk")
            w_out = w_out_even[i]
        else:
            wts = tuple(t[i] for t in odd_wts) + (p_f,)
            qc, kct, vc, qf, kft, vf, gate = _odd_proj(h, wts, tm_proj)
            mix_a = _swa_attention(sink_logits[i].astype(F32), qc, kct, vc, gate)
            mix_b = _softmax_attention(qf, kft, vf, gate, "causal")
            w_out = w_out_odd[i]
        ln = (w_out.astype(BF16), ln_gain[layer][None, :], ln_bias[layer][None, :])
        if layer == DEPTH - 1:
            return _out_proj_frames(mix_a, mix_b, h, *ln)
        rows = b * n
        h = _out_proj(mix_a.reshape(rows, -1), mix_b.reshape(rows, -1), h.reshape(rows, d),
                      *ln, tm_out).reshape(b, n, d)
```

```python
import functools

import jax
import jax.numpy as jnp
import numpy as np
from jax import lax
from jax.experimental import pallas as pl
from jax.experimental.pallas import tpu as pltpu

D_MODEL = 1024
CHUNK = 64
N_META = 16
HEAD_DIM = 64
N_HEADS = 8
MLA_NOPE = 64
MLA_ROPE = 32
MLA_RANK = 256
SWA_KV_HEADS = 2
WINDOW_CHUNKS = 2
ROPE_BASE = 10000.0
DEPTH = 4
DN_ALPHA = (2 * DEPTH) ** 0.25
LN_EPS = 1e-5
RMS_EPS = 1e-6
NEG = -1e30
NO_KEY = 1 << 30
SIGN_BIT = -(1 << 31)

LANES = 128
BLK = 128
TB = 4
TQ = TB * BLK
SB_TB = 8
SB_TQ = SB_TB * BLK
SB_UNROLL = 8
SWA_UNROLL = 3
OUT_TM = 1024
OUT_PART = 256
LEAD = BLK - N_META
HALF = LANES // 2
VMEM_LIMIT = 56 * 1024 * 1024

F32 = jnp.float32
BF16 = jnp.bfloat16


def _dot(a, b):
    return jnp.dot(a, b, preferred_element_type=F32)


def _chunk_of(pos):
    return jnp.maximum((pos >> 6) - 1, 0)


def _log_sigmoid(x):
    return jnp.minimum(x, 0.0) - jnp.log(1.0 + jnp.exp(-jnp.abs(x)))


def _al(x):
    return x if isinstance(x, int) else pl.multiple_of(x, BLK)


def _gated(mixed, gate):
    return (mixed * (gate * jax.nn.sigmoid(gate))).astype(BF16)


def _iotas(rows):
    return (lax.broadcasted_iota(jnp.int32, (rows, BLK), 0),
            lax.broadcasted_iota(jnp.int32, (rows, BLK), 1))


def _sb_kernel(q_ref, kt_ref, v_ref, g_ref, o_ref, u_ref, carry_ref, acc_ref, *, nt):
    r2 = lax.broadcasted_iota(jnp.int32, (2 * BLK, 2 * BLK), 0)
    c2 = lax.broadcasted_iota(jnp.int32, (2 * BLK, 2 * BLK), 1)
    u_ref[...] = jnp.where(((r2 >= BLK) == (c2 >= BLK)) & ((r2 & (BLK - 1)) > (c2 & (BLK - 1))),
                           1.0, 0.0).astype(BF16)
    zero_kt = jnp.zeros((HALF, BLK), BF16)

    def step(row0, off, rows, jb, mask=None):
        r0 = _al(row0 + off)
        k0 = _al(jb * BLK)
        sl = slice(off, off + rows)
        ktb = kt_ref[0, :, pl.ds(k0, BLK)]
        kbd = jnp.concatenate([jnp.concatenate([ktb[:HALF], zero_kt], axis=1),
                               jnp.concatenate([zero_kt, ktb[HALF:]], axis=1)], axis=0)
        z2 = _dot(q_ref[0, pl.ds(r0, rows), :], kbd)
        vst = jnp.concatenate([v_ref[0, pl.ds(k0, BLK), :LANES],
                               v_ref[0, pl.ds(k0, BLK), LANES:]], axis=0)
        if mask == "meta_keys":
            parts = [(0, rows, _iotas(rows)[1] >= LEAD)]
        elif mask is not None:
            rowi, lane = _iotas(BLK)
            local = lane < rowi
            parts = [(0, BLK, local if mask == "diagonal" else local & (lane >= LEAD))]
            if rows > BLK:
                parts.append((BLK, rows, None))

        def masked(x):
            if mask is None:
                return x
            return jnp.concatenate([x[a:b] if m is None else jnp.where(m, x[a:b], 0.0)
                                    for a, b, m in parts], axis=0)

        stays, log_betas = [], []
        for h in range(2):
            z = z2[:, h * BLK:(h + 1) * BLK]
            neg_abs = lax.bitcast_convert_type(
                lax.bitcast_convert_type(z, jnp.int32) | SIGN_BIT, F32)
            stay = jnp.maximum(z, 0.0) + jnp.log(1.0 + jnp.exp(neg_abs))
            log_betas.append(z - stay)
            stays.append(masked(stay))
        cs = _dot(jnp.concatenate([s.astype(BF16) for s in stays], axis=1), u_ref[...])
        ws = []
        for h in range(2):
            carry = carry_ref[h, sl, :]
            w = masked(jnp.exp(log_betas[h] - (cs[:, h * BLK:(h + 1) * BLK] + carry)))
            carry_ref[h, sl, :] = carry + jnp.sum(stays[h], axis=1, keepdims=True)
            ws.append(w.astype(BF16))
        acc_ref[sl, :] += _dot(jnp.concatenate(ws, axis=1), vst)

    def reset(rows):
        carry_ref[:, :rows, :] = jnp.zeros((2, rows, BLK), F32)
        acc_ref[:rows, :] = jnp.zeros((rows, BLK), F32)

    reset(BLK)
    step(0, 0, BLK, 0, "meta_tile")
    o_ref[0, :BLK, :] = _gated(acc_ref[:BLK, :], g_ref[0, :BLK, :])

    def tile(t, _):
        row0 = BLK + t * SB_TQ
        jb0 = 1 + t * SB_TB
        reset(SB_TQ)
        for d in reversed(range(SB_TB)):
            step(row0, d * BLK, SB_TQ - d * BLK, jb0 + d, "diagonal")

        def body(i, c):
            for u in range(SB_UNROLL):
                step(row0, 0, SB_TQ, jb0 - 1 - (i * SB_UNROLL + u))
            return c

        lax.fori_loop(0, t * (SB_TB // SB_UNROLL), body, 0)
        step(row0, 0, SB_TQ, 0, "meta_keys")
        o_ref[0, pl.ds(_al(row0), SB_TQ), :] = _gated(acc_ref[...], g_ref[0, pl.ds(_al(row0), SB_TQ), :])
        return 0

    lax.fori_loop(0, nt, tile, 0)


def _sb_attention(q, kt, v, gate):
    b, n, w = q.shape
    nt = (n - BLK) // SB_TQ
    return pl.pallas_call(
        functools.partial(_sb_kernel, nt=nt),
        grid=(b, w // LANES),
        in_specs=[pl.BlockSpec((1, n, LANES), lambda i, j: (i, 0, j)),
                  pl.BlockSpec((1, LANES, n), lambda i, j: (i, j, 0)),
                  pl.BlockSpec((1, n, 2 * LANES), lambda i, j: (i, 0, j)),
                  pl.BlockSpec((1, n, LANES), lambda i, j: (i, 0, j))],
        out_specs=pl.BlockSpec((1, n, LANES), lambda i, j: (i, 0, j)),
        out_shape=jax.ShapeDtypeStruct((b, n, w), BF16),
        scratch_shapes=[pltpu.VMEM((2 * BLK, 2 * BLK), BF16),
                        pltpu.VMEM((2, SB_TQ, BLK), F32),
                        pltpu.VMEM((SB_TQ, BLK), F32)],
        compiler_params=pltpu.CompilerParams(
            dimension_semantics=("parallel", "parallel"), vmem_limit_bytes=VMEM_LIMIT),
        name="sb_attention",
    )(q, kt, v, gate)


def _softmax_kernel(q_ref, kt_ref, v_ref, g_ref, o_ref, s_ref, so_ref, sd_ref, mx_ref, mb_ref, acc_ref,
                    bias_ref, *, nt, mode):
    def visible(kpos, qpos, lo):
        if mode == "chunk":
            return (kpos >= lo) & (_chunk_of(kpos) <= _chunk_of(qpos))
        return (kpos >= lo) & (kpos <= qpos)

    ql = BLK + lax.broadcasted_iota(jnp.int32, (TQ, TQ), 0)
    kl = BLK + lax.broadcasted_iota(jnp.int32, (TQ, TQ), 1)
    bias_ref[...] = jnp.where(visible(kl, ql, 0), 0.0, NEG)

    def chunk_loop(n_chunks, chunk_fn):
        def quad(i, carry):
            for u in range(4):
                chunk_fn(4 * i + u)
            return carry

        lax.fori_loop(0, n_chunks >> 2, quad, 0)
        done = (n_chunks >> 2) << 2

        @pl.when((n_chunks & 2) != 0)
        def _():
            chunk_fn(done)
            chunk_fn(done + 1)

        @pl.when((n_chunks & 1) != 0)
        def _():
            chunk_fn(n_chunks - 1)

    def emit(row0, rows, n_chunks, frames):
        frs = [slice(h * LANES, (h + 1) * LANES) for h in range(2)]
        r0 = _al(row0)
        rowi, lane = _iotas(rows)

        def chunk_scores(c, h):
            k0 = _al(BLK + c * TQ)
            return _dot(q_ref[0, pl.ds(r0, rows), frs[h]], kt_ref[0, frs[h], pl.ds(k0, TQ)])

        def fold_max(c, h, s):
            s_ref[h, :, pl.ds(_al(BLK + c * TQ), TQ)] = s
            m4 = jnp.maximum(jnp.maximum(s[:, :BLK], s[:, BLK:2 * BLK]),
                             jnp.maximum(s[:, 2 * BLK:3 * BLK], s[:, 3 * BLK:]))
            mx_ref[h] = jnp.maximum(mx_ref[h], m4)

        for h, fr in enumerate(frs):
            s0 = _dot(q_ref[0, pl.ds(r0, rows), fr], kt_ref[0, fr, :BLK])
            s0 = jnp.where(visible(lane, r0 + rowi, LEAD), s0, NEG)
            sd_ref[h, :rows, :] = s0
            mx_ref[h, :rows, :] = s0
        own0 = BLK + n_chunks * TQ
        own_parts = ((0, TQ // 2, TQ // 2), (TQ // 2, TQ, TQ))
        if frames:
            for h, fr in enumerate(frs):
                for a, b, nk in own_parts:
                    s = _dot(q_ref[0, pl.ds(_al(r0 + a), b - a), fr], kt_ref[0, fr, pl.ds(_al(own0), nk)])
                    s = s + bias_ref[a:b, :nk]
                    so_ref[h, a:b, :nk] = s
                    m = s[:, :BLK]
                    for j in range(1, nk // BLK):
                        m = jnp.maximum(m, s[:, j * BLK:(j + 1) * BLK])
                    mx_ref[h, a:b, :] = jnp.maximum(mx_ref[h, a:b, :], m)

            def p1(c):
                for h in range(2):
                    fold_max(c, h, chunk_scores(c, h))

            chunk_loop(n_chunks, p1)

        for h, fr in enumerate(frs):
            m_row = jnp.max(mx_ref[h, :rows, :], axis=1, keepdims=True)
            mb_ref[h, :rows, :] = jnp.broadcast_to(m_row, (rows, BLK))
            p0 = jnp.exp(sd_ref[h, :rows, :] - mb_ref[h, :rows, :]).astype(BF16)
            acc_ref[h, :rows, :] = _dot(p0, v_ref[0, :BLK, fr])
            if frames:
                for a, b, nk in own_parts:
                    mb = mb_ref[h, a:b, :]
                    p = jnp.exp(so_ref[h, a:b, :nk] - jnp.concatenate([mb] * (nk // BLK), axis=1)).astype(BF16)
                    acc_ref[h, a:b, :] += _dot(p, v_ref[0, pl.ds(_al(own0), nk), fr])

        if frames:
            def p2(c):
                k0 = _al(BLK + c * TQ)
                for h, fr in enumerate(frs):
                    mb = mb_ref[h]
                    p = jnp.exp(s_ref[h, :, pl.ds(k0, TQ)]
                                - jnp.concatenate([mb] * TB, axis=1)).astype(BF16)
                    acc_ref[h] += _dot(p, v_ref[0, pl.ds(k0, TQ), fr])

            chunk_loop(n_chunks, p2)

        outs = []
        for h in range(2):
            acc = acc_ref[h, :rows, :]
            outs.append(acc / pltpu.roll(acc, HALF, 1))
        o_ref[0, pl.ds(r0, rows), :] = _gated(jnp.where(lane < HALF, outs[0], outs[1]),
                                              g_ref[0, pl.ds(r0, rows), :])

    emit(0, BLK, 0, False)

    def tiles(i, _):
        for u in range(2):
            t = 2 * i + u
            emit(BLK + t * TQ, TQ, t, True)
        return 0

    lax.fori_loop(0, nt // 2, tiles, 0)


def _softmax_attention(q, kt, v, gate, mode):
    b, n, w = q.shape
    nt = (n - BLK) // TQ
    assert nt % 2 == 0
    pairs = w // (2 * LANES)
    return pl.pallas_call(
        functools.partial(_softmax_kernel, nt=nt, mode=mode),
        grid=(b, pairs),
        in_specs=[pl.BlockSpec((1, n, 2 * LANES), lambda i, j: (i, 0, j)),
                  pl.BlockSpec((1, 2 * LANES, n), lambda i, j: (i, j, 0)),
                  pl.BlockSpec((1, n, 2 * LANES), lambda i, j: (i, 0, j)),
                  pl.BlockSpec((1, n, LANES), lambda i, j: (i, 0, pairs + j))],
        out_specs=pl.BlockSpec((1, n, LANES), lambda i, j: (i, 0, j)),
        out_shape=jax.ShapeDtypeStruct((b, n, w // 2), BF16),
        scratch_shapes=[pltpu.VMEM((2, TQ, n), F32),
                        pltpu.VMEM((2, TQ, TQ), F32),
                        pltpu.VMEM((2, TQ, BLK), F32),
                        pltpu.VMEM((2, TQ, BLK), F32),
                        pltpu.VMEM((2, TQ, BLK), F32),
                        pltpu.VMEM((2, TQ, BLK), F32),
                        pltpu.VMEM((TQ, TQ), F32)],
        compiler_params=pltpu.CompilerParams(
            dimension_semantics=("parallel", "parallel"), vmem_limit_bytes=VMEM_LIMIT),
        name="softmax_attention_" + mode,
    )(q, kt, v, gate)


def _swa_kernel(sink_ref, q_ref, kt_ref, v_ref, g_ref, o_ref, bias_ref, *, nb):
    rowi, lane = _iotas(BLK)
    group = N_HEADS // SWA_KV_HEADS
    zero_kt = jnp.zeros((HALF, BLK), BF16)
    r2 = lax.broadcasted_iota(jnp.int32, (2 * BLK, BLK), 0)
    c2 = lax.broadcasted_iota(jnp.int32, (2 * BLK, BLK), 1)
    ones_st = jnp.where((r2 >= BLK) == (c2 >= HALF), 1.0, 0.0).astype(BF16)
    slopes = [2.0 ** (-8.0 * (hd + 1) / N_HEADS) for hd in range(N_HEADS)]
    GENERIC = 2

    def key_slots(m):
        return ((0, jnp.where(m >= 1, LEAD, NO_KEY)),
                (jnp.maximum(m - 1, 0), jnp.where(m >= 2, 0, NO_KEY)),
                (m, LEAD))

    def slot_biases(m):
        qpos = m * BLK + rowi
        cq = _chunk_of(qpos)
        out = []
        for jb, lo in key_slots(m):
            kpos = jb * BLK + lane
            ck = _chunk_of(kpos)
            msk = (kpos >= lo) & (ck <= cq) & ((cq - ck <= WINDOW_CHUNKS) | (kpos < BLK))
            out.append((msk, jnp.abs(qpos - kpos).astype(F32)))
        return out

    for slot, (msk, dist) in enumerate(slot_biases(GENERIC)):
        for hd in range(N_HEADS):
            bias_ref[hd, slot] = jnp.where(msk, -slopes[hd] * dist, NEG)

    def q_block(m, generic):
        r0 = _al(m * BLK)
        slots = key_slots(m)
        if generic:
            extra = jnp.float32(BLK) * (m - GENERIC)
            biases = lambda hd: [bias_ref[hd, 0] - slopes[hd] * extra, bias_ref[hd, 1], bias_ref[hd, 2]]
        else:
            special = slot_biases(m)
            biases = lambda hd: [jnp.where(msk, -slopes[hd] * dist, NEG) for msk, dist in special]
        for g in range(SWA_KV_HEADS):
            kbds, wcats = [], []
            for jb, _ in slots:
                k0 = _al(jb * BLK)
                ktg =kt_ref[0, g * HALF:(g + 1) * HALF, pl.ds(k0, BLK)]
                kbds.append(jnp.concatenate([jnp.concatenate([ktg, zero_kt], axis=1),
                                             jnp.concatenate([zero_kt, ktg], axis=1)], axis=0))
                vst = jnp.concatenate([v_ref[0, pl.ds(k0, BLK), (2 * g) * LANES:(2 * g + 1) * LANES],
                                       v_ref[0, pl.ds(k0, BLK), (2 * g + 1) * LANES:(2 * g + 2) * LANES]],
                                      axis=0)
                wcats.append(jnp.concatenate([vst, ones_st], axis=1))
            for pr in range(group // 2):
                pair = g * (group // 2) + pr
                q2 = q_ref[0, pl.ds(r0, BLK), pair * LANES:(pair + 1) * LANES]
                zs = [_dot(q2, kbd) for kbd in kbds]
                ps, sink_terms = [[], []], []
                for h in range(2):
                    head = 2 * pair + h
                    ss = [z[:, h * BLK:(h + 1) * BLK] + bias for z, bias in zip(zs, biases(head))]
                    sink = sink_ref[head]
                    mx = jnp.maximum(jnp.maximum(ss[0], ss[1]), ss[2])
                    m_row = jnp.maximum(jnp.max(mx, axis=1, keepdims=True), sink)
                    ps[h] = [jnp.exp(s - m_row).astype(BF16) for s in ss]
                    sink_terms.append(jnp.exp(sink - m_row))
                res = jnp.zeros((BLK, 2 * BLK), F32)
                for i in range(len(slots)):
                    res = res + _dot(jnp.concatenate([ps[0][i], ps[1][i]], axis=1), wcats[i])
                denom = res[:, BLK:] + jnp.where(lane < HALF, sink_terms[0], sink_terms[1])
                cols = slice(pair * LANES, (pair + 1) * LANES)
                o_ref[0, pl.ds(r0, BLK), cols] = _gated(res[:, :BLK] / denom, g_ref[0, pl.ds(r0, BLK), cols])

    for m in range(GENERIC):
        q_block(m, False)

    def q_blocks(i, carry):
        for u in range(SWA_UNROLL):
            q_block(GENERIC + SWA_UNROLL * i + u, True)
        return carry

    n_generic = nb - GENERIC
    lax.fori_loop(0, n_generic // SWA_UNROLL, q_blocks, 0)
    for m in range(nb - n_generic % SWA_UNROLL, nb):
        q_block(jnp.int32(m), True)


def _swa_attention(sinks, q, kt, v, gate):
    b, n, w = q.shape
    nb = n // BLK
    return pl.pallas_call(
        functools.partial(_swa_kernel, nb=nb),
        grid=(b,),
        in_specs=[
            pl.BlockSpec(memory_space=pltpu.SMEM),
            pl.BlockSpec((1, n, w), lambda i: (i, 0, 0)),
            pl.BlockSpec((1, kt.shape[1], n), lambda i: (i, 0, 0)),
            pl.BlockSpec((1, n, v.shape[2]), lambda i: (i, 0, 0)),
            pl.BlockSpec((1, n, w), lambda i: (i, 0, 0)),
        ],
        out_specs=pl.BlockSpec((1, n, w), lambda i: (i, 0, 0)),
        out_shape=jax.ShapeDtypeStruct((b, n, w), BF16),
        scratch_shapes=[pltpu.VMEM((N_HEADS, 3, BLK, BLK), F32)],
        compiler_params=pltpu.CompilerParams(
            dimension_semantics=("parallel",), vmem_limit_bytes=VMEM_LIMIT),
        name="swa_attention",
    )(sinks, q, kt, v, gate)


def _spread_heads(x, fill):
    lane = lax.broadcasted_iota(jnp.int32, (x.shape[0], LANES), 1)
    blocks = []
    for p in range(x.shape[1] // LANES):
        xp = x[:, p * LANES:(p + 1) * LANES]
        for odd in range(2):
            blk = 2 * p + odd
            if isinstance(fill, float):
                f = fill
            elif callable(fill):
                f = fill(lane)
            else:
                f = fill[:, blk * LANES:(blk + 1) * LANES]
            blocks.append(jnp.where((lane >= HALF) if odd else (lane < HALF), xp, f))
    return jnp.concatenate(blocks, axis=1)


def _even_proj_kernel(x_ref, wsb_ref, wc_ref, wkr_ref, wg_ref, gq_ref, gkv_ref,
                      wa_ref, wb_ref, wkv_ref, cq_ref, sq_ref, ck_ref, sk_ref,
                      qs_ref, kst_ref, vs_ref, qm_ref, kmt_ref, vm_ref, gate_ref):
    xb = x_ref[0].astype(BF16)
    w = N_HEADS * HEAD_DIM
    sb = _dot(xb, wsb_ref[...])
    qs_ref[0] = sb[:, :w].astype(BF16)
    kst_ref[0] = sb[:, w:2 * w].T.astype(BF16)
    vs_ref[0] = _spread_heads(sb[:, 2 * w:], 0.0).astype(BF16)
    gate_ref[0] = _dot(xb, wg_ref[...])

    c = _dot(xb, wc_ref[...])

    def rms(t, g):
        return (t * lax.rsqrt(jnp.mean(t * t, axis=-1, keepdims=True) + RMS_EPS) * g).astype(BF16)

    xq = rms(c[:, :MLA_RANK], gq_ref[...])
    xkv = rms(c[:, MLA_RANK:], gkv_ref[...])
    pairs = N_HEADS // 2
    cq = jnp.concatenate([cq_ref[...]] * pairs, axis=1)
    sq = jnp.concatenate([sq_ref[...]] * pairs, axis=1)
    qm_ref[0] = (_dot(xq, wa_ref[...]) * cq + _dot(xq, wb_ref[...]) * sq).astype(BF16)
    kr = _dot(xb, wkr_ref[...])
    krr = kr[:, :LANES] * ck_ref[...] + kr[:, LANES:] * sk_ref[...]
    krr = (krr, pltpu.roll(krr, HALF, 1))
    kv = _dot(xkv, wkv_ref[...])
    lane = lax.broadcasted_iota(jnp.int32, (kv.shape[0], LANES), 1)
    kblocks, vblocks = [], []
    for hd in range(N_HEADS):
        blk = slice(hd * LANES, (hd + 1) * LANES)
        v_half = (lane >= HALF) if hd % 2 else (lane < HALF)
        kblocks.append(jnp.where(v_half, krr[hd % 2], kv[:, blk]))
        vblocks.append(jnp.where(v_half, kv[:, blk], 1.0))
    kmt_ref[0] = jnp.concatenate(kblocks, axis=1).T.astype(BF16)
    vm_ref[0] = jnp.concatenate(vblocks, axis=1).astype(BF16)


def _even_proj(h, wts, tabs, tm):
    b, n, d = h.shape
    grid = (n // tm, b)
    row = lambda i, j: (j, i, 0)
    col = lambda i, j: (j, 0, i)
    const = lambda i, j: (0, 0)
    tab = lambda i, j: (i, 0)
    full = lambda a: pl.BlockSpec(a.shape, const)
    wide = N_HEADS * LANES
    narrow = N_HEADS * HEAD_DIM
    outs = [(narrow, BF16, False), (narrow, BF16, True), (wide, BF16, False),
            (wide, BF16, False), (wide, BF16, True), (wide, BF16, False), (d, F32, False)]
    return pl.pallas_call(
        _even_proj_kernel,
        grid=grid,
        in_specs=[pl.BlockSpec((1, tm, d), row)] + [full(a) for a in wts] + [
            pl.BlockSpec((tm, 2 * LANES), tab), pl.BlockSpec((tm, 2 * LANES), tab),
            pl.BlockSpec((tm, LANES), tab), pl.BlockSpec((tm, LANES), tab)],
        out_specs=[pl.BlockSpec((1, c, tm), col) if tr else pl.BlockSpec((1, tm, c), row)
                   for c, _, tr in outs],
        out_shape=[jax.ShapeDtypeStruct((b, c, n) if tr else (b, n, c), t) for c, t, tr in outs],
        compiler_params=pltpu.CompilerParams(
            dimension_semantics=("parallel", "parallel"), vmem_limit_bytes=VMEM_LIMIT),
        name="even_proj",
    )(h, *wts, *tabs)


def _odd_proj_kernel(x_ref, wqc_ref, wkc_ref, wvc_ref, wqf_ref, wkf_ref, wvf_ref, wfz_ref, wg_ref,
                     bf_ref, pf_ref,
                     qc_ref, kct_ref, vc_ref, qf_ref, kft_ref, vf_ref, gate_ref, carry_ref, *, tm):
    i = pl.program_id(1)
    xb = x_ref[0].astype(BF16)
    qc_ref[0] = _dot(xb, wqc_ref[...]).astype(BF16)
    kct_ref[0] = _dot(xb, wkc_ref[...]).T.astype(BF16)
    vc_ref[0] = _dot(xb, wvc_ref[...]).astype(BF16)
    gate_ref[0] = _dot(xb, wg_ref[...])

    vf_ref[0] = _spread_heads(_dot(xb, wvf_ref[...]), 1.0).astype(BF16)
    qf_ref[0] = _spread_heads(_dot(xb, wqf_ref[...]),
                              lambda lane: jnp.where((lane & (HALF - 1)) < 3, 1.0, 0.0)).astype(BF16)

    fz = _dot(xb, wfz_ref[...]) + bf_ref[...]
    r = lax.broadcasted_iota(jnp.int32, (tm, LANES), 0) + i * tm
    c = lax.broadcasted_iota(jnp.int32, (tm, LANES), 1)
    lf = jnp.where((c < N_HEADS) & (r >= LEAD), _log_sigmoid(fz), 0.0)

    def split3(t):
        a = t.astype(BF16)
        rem = t - a.astype(F32)
        bb = rem.astype(BF16)
        cc = (rem - bb.astype(F32)).astype(BF16)
        return jnp.concatenate([a, bb, cc], axis=1)

    rr = lax.broadcasted_iota(jnp.int32, (tm, tm), 0)
    cc = lax.broadcasted_iota(jnp.int32, (tm, tm), 1)
    tri = jnp.where(cc <= rr, 1.0, 0.0).astype(BF16)

    @pl.when(i == 0)
    def _():
        carry_ref[...] = jnp.zeros_like(carry_ref)

    c3 = _dot(tri, split3(lf))
    cum = c3[:, :LANES] + c3[:, LANES:2 * LANES] + c3[:, 2 * LANES:] + carry_ref[0:1, :]
    carry_ref[...] = jnp.broadcast_to(cum[tm - 1:tm, :], carry_ref.shape)
    kft_ref[0] = _spread_heads(_dot(xb, wkf_ref[...]), _dot(split3(-cum), pf_ref[...])).T.astype(BF16)


def _odd_proj(h, wts, tm):
    b, n, d = h.shape
    row = lambda j, i: (j, i, 0)
    col = lambda j, i: (j, 0, i)
    const = lambda j, i: (0, 0)
    full = lambda a: pl.BlockSpec(a.shape, const)
    wide = N_HEADS * LANES
    narrow = N_HEADS * HEAD_DIM
    outs = [(narrow, BF16, False), (SWA_KV_HEADS * HEAD_DIM, BF16, True), (narrow, BF16, False),
            (wide, BF16, False), (wide, BF16, True), (wide, BF16, False), (d, F32, False)]
    return pl.pallas_call(
        functools.partial(_odd_proj_kernel, tm=tm),
        grid=(b, n // tm),
        in_specs=[pl.BlockSpec((1, tm, d), row)] + [full(a) for a in wts],
        out_specs=[pl.BlockSpec((1, c, tm), col) if tr else pl.BlockSpec((1, tm, c), row)
                   for c, _, tr in outs],
        out_shape=[jax.ShapeDtypeStruct((b, c, n) if tr else (b, n, c), t) for c, t, tr in outs],
        scratch_shapes=[pltpu.VMEM((8, LANES), F32)],
        compiler_params=pltpu.CompilerParams(
            dimension_semantics=("parallel", "arbitrary"), vmem_limit_bytes=VMEM_LIMIT),
        name="odd_proj",
    )(h, *wts)


def _out_math(ma, mb, h, w_ref, g, b):
    half = ma.shape[-1]
    y = _dot(ma, w_ref[:half, :]) + _dot(mb, w_ref[half:, :])
    r = DN_ALPHA * h + y
    mu = jnp.mean(r, axis=-1, keepdims=True)
    cen = r - mu
    var = jnp.mean(cen * cen, axis=-1, keepdims=True)
    return cen * lax.rsqrt(var + LN_EPS) * g + b


def _out_rows(ma_ref, mb_ref, h_ref, w_ref, g_ref, b_ref, store):
    rows = ma_ref.shape[0]
    part = OUT_PART if rows % OUT_PART == 0 else rows
    for r in range(rows // part):
        sl = slice(r * part, (r + 1) * part)
        store(sl, _out_math(ma_ref[sl, :], mb_ref[sl, :], h_ref[sl, :], w_ref, g_ref[...], b_ref[...]))


def _out_kernel(ma_ref, mb_ref, h_ref, w_ref, g_ref, b_ref, o_ref):
    def store(sl, val):
        o_ref[sl, :] = val

    _out_rows(ma_ref, mb_ref, h_ref, w_ref, g_ref, b_ref, store)


def _out_frames_kernel(ma_ref, mb_ref, h_ref, w_ref, g_ref, b_ref, o_ref):
    def store(sl, val):
        o_ref[0, sl, :] = val

    _out_rows(ma_ref, mb_ref, h_ref, w_ref, g_ref, b_ref, store)


def _out_proj_frames(ma, mb, h, w, g, bias):
    b, n, d = h.shape
    half = ma.shape[-1]
    flat = lambda a: a.reshape(b * n, a.shape[-1])
    src = lambda i, j: (pl.multiple_of(i * n + BLK + j * OUT_TM, BLK), 0)
    window = lambda width: pl.BlockSpec((pl.Element(OUT_TM), pl.Element(width)), src)
    const = lambda i, j: (0, 0)
    return pl.pallas_call(
        _out_frames_kernel,
        grid=(b, (n - BLK) // OUT_TM),
        in_specs=[window(half), window(half), window(d),
                  pl.BlockSpec(w.shape, const), pl.BlockSpec((1, d), const), pl.BlockSpec((1, d), const)],
        out_specs=pl.BlockSpec((1, OUT_TM, d), lambda i, j: (i, j, 0)),
        out_shape=jax.ShapeDtypeStruct((b, n - BLK, d), F32),
        compiler_params=pltpu.CompilerParams(
            dimension_semantics=("parallel", "parallel"), vmem_limit_bytes=VMEM_LIMIT),
        name="out_proj_ln_frames",
    )(flat(ma), flat(mb), flat(h), w, g, bias)


def _out_proj(ma, mb, h, w, g, bias, tm):
    rows, d = h.shape
    half = ma.shape[-1]
    row = lambda i: (i, 0)
    const = lambda i: (0, 0)
    return pl.pallas_call(
        _out_kernel,
        grid=(rows // tm,),
        in_specs=[pl.BlockSpec((tm, half), row), pl.BlockSpec((tm, half), row),
                  pl.BlockSpec((tm, d), row),
                  pl.BlockSpec(w.shape, const), pl.BlockSpec((1, d), const), pl.BlockSpec((1, d), const)],
        out_specs=pl.BlockSpec((tm, d), row),
        out_shape=jax.ShapeDtypeStruct((rows, d), F32),
        compiler_params=pltpu.CompilerParams(
            dimension_semantics=("parallel",), vmem_limit_bytes=VMEM_LIMIT),
        name="out_proj_ln",
    )(ma, mb, h, w, g, bias)


def _by_parity(even, odd):
    is_odd = (jnp.arange(N_HEADS) % 2 == 1)[:, None]
    return jnp.where(is_odd, odd, even).reshape(even.shape[:-2] + (N_HEADS * LANES,))


def _mla_head_blocks(nope, rot):
    spare = jnp.zeros(rot.shape[:-1] + (HALF - MLA_ROPE,), rot.dtype)
    return _by_parity(jnp.concatenate([rot, spare, nope], axis=-1),
                      jnp.concatenate([nope, rot, spare], axis=-1))


def _even_weights(w_in, g_cq, g_ckv, w_uq, w_ukv):
    a = N_HEADS * HEAD_DIM
    wq, wk, wv = w_in[:, :a], w_in[:, a:2 * a], w_in[:, 2 * a:3 * a]
    o = 3 * a
    w_c = w_in[:, o:o + 2 * MLA_RANK]
    o += 2 * MLA_RANK
    w_kr = w_in[:, o:o + MLA_ROPE]
    o += MLA_ROPE
    w_g = w_in[:, o:]
    w_sb = jnp.concatenate([wq * HEAD_DIM ** -0.5, wk, wv], axis=1)
    hr = MLA_ROPE // 2
    zpad = jnp.zeros((w_in.shape[0], LANES - MLA_ROPE), w_in.dtype)
    w_kr2 = jnp.concatenate([w_kr, zpad, w_kr[:, hr:], w_kr[:, :hr], zpad], axis=1)

    uq = w_uq.reshape(MLA_RANK, N_HEADS, MLA_NOPE + MLA_ROPE)
    ukv = w_ukv.reshape(MLA_RANK, N_HEADS, MLA_NOPE + HEAD_DIM)
    nope, r1, r2 = uq[..., :MLA_NOPE], uq[..., MLA_NOPE:MLA_NOPE + hr], uq[..., MLA_NOPE + hr:]
    w_a = _mla_head_blocks(nope, jnp.concatenate([r1, r2], axis=-1))
    w_b = _mla_head_blocks(jnp.zeros_like(nope), jnp.concatenate([r2, r1], axis=-1))
    k_nope, v = ukv[..., :MLA_NOPE], ukv[..., MLA_NOPE:]
    w_kv = _by_parity(jnp.concatenate([v, k_nope], axis=-1), ukv)
    bf = lambda t: t.astype(BF16)
    return (bf(w_sb), bf(w_c), bf(w_kr2), bf(w_g), g_cq[None, :], g_ckv[None, :],
            bf(w_a), bf(w_b), bf(w_kv))


def _rope_tables(n):
    hr = MLA_ROPE // 2
    pos = jnp.arange(n, dtype=F32) - LEAD
    inv = ROPE_BASE ** (-jnp.arange(hr, dtype=F32) / hr)
    ang = pos[:, None] * inv[None, :]
    cos, sin = jnp.cos(ang), jnp.sin(ang)
    ones = jnp.ones((n, MLA_NOPE), F32)
    scale = (MLA_NOPE + MLA_ROPE) ** -0.5
    c_rot, s_rot = jnp.concatenate([cos, cos], axis=1), jnp.concatenate([-sin, sin], axis=1)
    spare = jnp.zeros((n, HALF - MLA_ROPE), F32)
    c_q = jnp.concatenate([c_rot, spare, ones, ones, c_rot, spare], axis=1) * scale
    s_q = jnp.concatenate([s_rot, spare, 0.0 * ones, 0.0 * ones, s_rot, spare], axis=1) * scale
    z96 = jnp.zeros((n, LANES - MLA_ROPE), F32)
    c_k = jnp.concatenate([c_rot, z96], axis=1)
    s_k = jnp.concatenate([s_rot, z96], axis=1)
    return c_q, s_q, c_k, s_k


def _odd_weights(w_in, b_forget):
    a = N_HEADS * HEAD_DIM
    kvw = SWA_KV_HEADS * HEAD_DIM
    wqc = w_in[:, :a]
    o = a
    wkc = w_in[:, o:o + kvw]
    o += kvw
    wvc = w_in[:, o:o + kvw]
    o += kvw
    wqd, wkd, wvd = w_in[:, o:o + a], w_in[:, o + a:o + 2 * a], w_in[:, o + 2 * a:o + 3 * a]
    o += 3 * a
    wfz = w_in[:, o:o + N_HEADS]
    o += N_HEADS
    w_g = w_in[:, o:]
    scale = HEAD_DIM ** -0.5
    v0, v1 = wvc[:, :HEAD_DIM], wvc[:, HEAD_DIM:]
    z = jnp.zeros_like(v0)
    w_vc = jnp.concatenate([v0, z, z, v0, v1, z, z, v1], axis=1)
    w_fz = jnp.concatenate([wfz, jnp.zeros((w_in.shape[0], LANES - N_HEADS), w_in.dtype)], axis=1)
    b_f = jnp.concatenate([b_forget, jnp.zeros((LANES - N_HEADS,), b_forget.dtype)])[None, :]
    bf = lambda t: t.astype(BF16)
    return (bf(wqc * scale), bf(wkc), bf(w_vc), bf(wqd * scale), bf(wkd), bf(wvd), bf(w_fz), bf(w_g),
            b_f.astype(F32))


def _piece_placement():
    p_f = np.zeros((3 * LANES, N_HEADS * LANES), np.float32)
    for t in range(3):
        for h in range(N_HEADS):
            p_f[t * LANES + h, h * LANES + (0 if h % 2 else HALF) + t] = 1.0
    return jnp.asarray(p_f, BF16)


def _row_tile(nb, cap):
    g = max(t for t in range(1, cap + 1) if nb % t == 0)
    return g * BLK


def kernel(x, meta_tokens, w_in_even, g_cq, g_ckv, w_uq, w_ukv, w_out_even,
           w_in_odd, b_forget, sink_logits, w_out_odd, ln_gain, ln_bias):
    b, s, d = x.shape
    assert s % SB_TQ == 0 and s % TQ == 0 and d == D_MODEL
    n = BLK + s
    nb = n // BLK
    h = jnp.concatenate([jnp.zeros((b, LEAD, d), x.dtype),
                         jnp.broadcast_to(meta_tokens[None].astype(x.dtype), (b, N_META, d)), x], axis=1)
    tabs = _rope_tables(n)
    tm_proj = _row_tile(nb, 3)
    tm_out = _row_tile(b * nb, OUT_TM // BLK)
    even_wts = jax.vmap(_even_weights)(w_in_even, g_cq, g_ckv, w_uq, w_ukv)
    odd_wts = jax.vmap(_odd_weights)(w_in_odd, b_forget)
    p_f = _piece_placement()
    for layer in range(DEPTH):
        i = layer // 2
        if layer % 2 == 0:
            wts = tuple(t[i] for t in even_wts)
            qs, kst, vs, qm, kmt, vm, gate = _even_proj(h, wts, tabs, tm_proj)
            mix_a = _sb_attention(qs, kst, vs, gate)
            mix_b = _softmax_attention(qm, kmt, vm, gate, "chunk")
            w_out = w_out_even[i]
        else:
            wts = tuple(t[i] for t in odd_wts) + (p_f,)
            qc, kct, vc, qf, kft, vf, gate = _odd_proj(h, wts, tm_proj)
            mix_a = _swa_attention(sink_logits[i].astype(F32), qc, kct, vc, gate)
            mix_b = _softmax_attention(qf, kft, vf, gate, "causal")
            w_out = w_out_odd[i]
        ln = (w_out.astype(BF16), ln_gain[layer][None, :], ln_bias[layer][None, :])
        if layer == DEPTH - 1:
            return _out_proj_frames(mix_a, mix_b, h, *ln)
        rows = b * n
        h = _out_proj(mix_a.reshape(rows, -1), mix_b.reshape(rows, -1), h.reshape(rows, d),
                      *ln, tm_out).reshape(b, n, d)
```

```python
import functools

import jax
import jax.numpy as jnp
import numpy as np
from jax import lax
from jax.experimental import pallas as pl
from jax.experimental.pallas import tpu as pltpu

D_MODEL = 1024
CHUNK = 64
N_META = 16
HEAD_DIM = 64
N_HEADS = 8
MLA_NOPE = 64
MLA_ROPE = 32
MLA_RANK = 256
SWA_KV_HEADS = 2
WINDOW_CHUNKS = 2
ROPE_BASE = 10000.0
DEPTH = 4
DN_ALPHA = (2 * DEPTH) ** 0.25
LN_EPS = 1e-5
RMS_EPS = 1e-6
NEG = -1e30
NO_KEY = 1 << 30
SIGN_BIT = -(1 << 31)
LOG2E = 1.4426950408889634

LANES = 128
BLK = 128
TB = 4
TQ = TB * BLK
SB_TB = 8
SB_TQ = SB_TB * BLK
SB_UNROLL = 8
SWA_UNROLL = 3
OUT_TM = 1024
OUT_PART = 256
LEAD = BLK - N_META
HALF = LANES // 2
VMEM_LIMIT = 56 * 1024 * 1024

F32 = jnp.float32
BF16 = jnp.bfloat16


def _dot(a, b):
    return jnp.dot(a, b, preferred_element_type=F32)


def _chunk_of(pos):
    return jnp.maximum((pos >> 6) - 1, 0)


def _log_sigmoid(x):
    return jnp.minimum(x, 0.0) - jnp.log(1.0 + jnp.exp(-jnp.abs(x)))


def _al(x):
    return x if isinstance(x, int) else pl.multiple_of(x, BLK)


def _gated(mixed, gate):
    return (mixed * (gate * jax.nn.sigmoid(gate))).astype(BF16)


def _iotas(rows):
    return (lax.broadcasted_iota(jnp.int32, (rows, BLK), 0),
            lax.broadcasted_iota(jnp.int32, (rows, BLK), 1))


def _sb_kernel(q_ref, kt_ref, v_ref, g_ref, o_ref, u_ref, carry_ref, acc_ref, *, nt):
    r2 = lax.broadcasted_iota(jnp.int32, (2 * BLK, 2 * BLK), 0)
    c2 = lax.broadcasted_iota(jnp.int32, (2 * BLK, 2 * BLK), 1)
    u_ref[...] = jnp.where(((r2 >= BLK) == (c2 >= BLK)) & ((r2 & (BLK - 1)) > (c2 & (BLK - 1))),
                           1.0, 0.0).astype(BF16)
    zero_kt = jnp.zeros((HALF, BLK), BF16)

    def step(row0, off, rows, jb, mask=None):
        r0 = _al(row0 + off)
        k0 = _al(jb * BLK)
        sl = slice(off, off + rows)
        ktb = kt_ref[0, :, pl.ds(k0, BLK)]
        kbd = jnp.concatenate([jnp.concatenate([ktb[:HALF], zero_kt], axis=1),
                               jnp.concatenate([zero_kt, ktb[HALF:]], axis=1)], axis=0)
        z2 = _dot(q_ref[0, pl.ds(r0, rows), :], kbd)
        vst = jnp.concatenate([v_ref[0, pl.ds(k0, BLK), :LANES],
                               v_ref[0, pl.ds(k0, BLK), LANES:]], axis=0)
        if mask == "meta_keys":
            parts = [(0, rows, _iotas(rows)[1] >= LEAD)]
        elif mask is not None:
            rowi, lane = _iotas(BLK)
            local = lane < rowi
            parts = [(0, BLK, local if mask == "diagonal" else local & (lane >= LEAD))]
            if rows > BLK:
                parts.append((BLK, rows, None))

        def masked(x):
            if mask is None:
                return x
            return jnp.concatenate([x[a:b] if m is None else jnp.where(m, x[a:b], 0.0)
                                    for a, b, m in parts], axis=0)

        stays, log_betas = [], []
        for h in range(2):
            z = z2[:, h * BLK:(h + 1) * BLK]
            neg_abs = lax.bitcast_convert_type(
                lax.bitcast_convert_type(z, jnp.int32) | SIGN_BIT, F32)
            stay = jnp.maximum(z, 0.0) + jnp.log(1.0 + jnp.exp2(neg_abs)) * LOG2E
            log_betas.append(z - stay)
            stays.append(masked(stay))
        cs = _dot(jnp.concatenate([s.astype(BF16) for s in stays], axis=1), u_ref[...])
        ws = []
        for h in range(2):
            carry = carry_ref[h, sl, :]
            w = masked(jnp.exp2(log_betas[h] - (cs[:, h * BLK:(h + 1) * BLK] + carry)))
            carry_ref[h, sl, :] = carry + jnp.sum(stays[h], axis=1, keepdims=True)
            ws.append(w.astype(BF16))
        acc_ref[sl, :] += _dot(jnp.concatenate(ws, axis=1), vst)

    def reset(rows):
        carry_ref[:, :rows, :] = jnp.zeros((2, rows, BLK), F32)
        acc_ref[:rows, :] = jnp.zeros((rows, BLK), F32)

    reset(BLK)
    step(0, 0, BLK, 0, "meta_tile")
    o_ref[0, :BLK, :] = _gated(acc_ref[:BLK, :], g_ref[0, :BLK, :])

    def tile(t, _):
        row0 = BLK + t * SB_TQ
        jb0 = 1 + t * SB_TB
        reset(SB_TQ)
        for d in reversed(range(SB_TB)):
            step(row0, d * BLK, SB_TQ - d * BLK, jb0 + d, "diagonal")

        def body(i, c):
            for u in range(SB_UNROLL):
                step(row0, 0, SB_TQ, jb0 - 1 - (i * SB_UNROLL + u))
            return c

        lax.fori_loop(0, t * (SB_TB // SB_UNROLL), body, 0)
        step(row0, 0, SB_TQ, 0, "meta_keys")
        o_ref[0, pl.ds(_al(row0), SB_TQ), :] = _gated(acc_ref[...], g_ref[0, pl.ds(_al(row0), SB_TQ), :])
        return 0

    lax.fori_loop(0, nt, tile, 0)


def _sb_attention(q, kt, v, gate):
    b, n, w = q.shape
    nt = (n - BLK) // SB_TQ
    return pl.pallas_call(
        functools.partial(_sb_kernel, nt=nt),
        grid=(b, w // LANES),
        in_specs=[pl.BlockSpec((1, n, LANES), lambda i, j: (i, 0, j)),
                  pl.BlockSpec((1, LANES, n), lambda i, j: (i, j, 0)),
                  pl.BlockSpec((1, n, 2 * LANES), lambda i, j: (i, 0, j)),
                  pl.BlockSpec((1, n, LANES), lambda i, j: (i, 0, j))],
        out_specs=pl.BlockSpec((1, n, LANES), lambda i, j: (i, 0, j)),
        out_shape=jax.ShapeDtypeStruct((b, n, w), BF16),
        scratch_shapes=[pltpu.VMEM((2 * BLK, 2 * BLK), BF16),
                        pltpu.VMEM((2, SB_TQ, BLK), F32),
                        pltpu.VMEM((SB_TQ, BLK), F32)],
        compiler_params=pltpu.CompilerParams(
            dimension_semantics=("parallel", "parallel"), vmem_limit_bytes=VMEM_LIMIT),
        name="sb_attention",
    )(q, kt, v, gate)


def _softmax_kernel(q_ref, kt_ref, v_ref, g_ref, o_ref, s_ref, so_ref, sd_ref, mx_ref, mb_ref, acc_ref,
                    bias_ref, *, nt, mode):
    def visible(kpos, qpos, lo):
        if mode == "chunk":
            return (kpos >= lo) & (_chunk_of(kpos) <= _chunk_of(qpos))
        return (kpos >= lo) & (kpos <= qpos)

    ql = BLK + lax.broadcasted_iota(jnp.int32, (TQ, TQ), 0)
    kl = BLK + lax.broadcasted_iota(jnp.int32, (TQ, TQ), 1)
    bias_ref[...] = jnp.where(visible(kl, ql, 0), 0.0, NEG)

    def chunk_loop(n_chunks, chunk_fn):
        def quad(i, carry):
            for u in range(4):
                chunk_fn(4 * i + u)
            return carry

        lax.fori_loop(0, n_chunks >> 2, quad, 0)
        done = (n_chunks >> 2) << 2

        @pl.when((n_chunks & 2) != 0)
        def _():
            chunk_fn(done)
            chunk_fn(done + 1)

        @pl.when((n_chunks & 1) != 0)
        def _():
            chunk_fn(n_chunks - 1)

    def emit(row0, rows, n_chunks, frames):
        frs = [slice(h * LANES, (h + 1) * LANES) for h in range(2)]
        r0 = _al(row0)
        rowi, lane = _iotas(rows)

        def chunk_scores(c, h):
            k0 = _al(BLK + c * TQ)
            return _dot(q_ref[0, pl.ds(r0, rows), frs[h]], kt_ref[0, frs[h], pl.ds(k0, TQ)])

        def fold_max(c, h, s):
            s_ref[h, :, pl.ds(_al(BLK + c * TQ), TQ)] = s
            m4 = jnp.maximum(jnp.maximum(s[:, :BLK], s[:, BLK:2 * BLK]),
                             jnp.maximum(s[:, 2 * BLK:3 * BLK], s[:, 3 * BLK:]))
            mx_ref[h] = jnp.maximum(mx_ref[h], m4)

        for h, fr in enumerate(frs):
            s0 = _dot(q_ref[0, pl.ds(r0, rows), fr], kt_ref[0, fr, :BLK])
            s0 = jnp.where(visible(lane, r0 + rowi, LEAD), s0, NEG)
            sd_ref[h, :rows, :] = s0
            mx_ref[h, :rows, :] = s0
        own0 = BLK + n_chunks * TQ
        own_parts = ((0, TQ // 2, TQ // 2), (TQ // 2, TQ, TQ))
        if frames:
            for h, fr in enumerate(frs):
                for a, b, nk in own_parts:
                    s = _dot(q_ref[0, pl.ds(_al(r0 + a), b - a), fr], kt_ref[0, fr, pl.ds(_al(own0), nk)])
                    s = s + bias_ref[a:b, :nk]
                    so_ref[h, a:b, :nk] = s
                    m = s[:, :BLK]
                    for j in range(1, nk // BLK):
                        m = jnp.maximum(m, s[:, j * BLK:(j + 1) * BLK])
                    mx_ref[h, a:b, :] = jnp.maximum(mx_ref[h, a:b, :], m)

            def p1(c):
                for h in range(2):
                    fold_max(c, h, chunk_scores(c, h))

            chunk_loop(n_chunks, p1)

        for h, fr in enumerate(frs):
            m_row = jnp.max(mx_ref[h, :rows, :], axis=1, keepdims=True)
            mb_ref[h, :rows, :] = jnp.broadcast_to(m_row, (rows, BLK))
            p0 = jnp.exp(sd_ref[h, :rows, :] - mb_ref[h, :rows, :]).astype(BF16)
            acc_ref[h, :rows, :] = _dot(p0, v_ref[0, :BLK, fr])
            if frames:
                for a, b, nk in own_parts:
                    mb = mb_ref[h, a:b, :]
                    p = jnp.exp(so_ref[h, a:b, :nk] - jnp.concatenate([mb] * (nk // BLK), axis=1)).astype(BF16)
                    acc_ref[h, a:b, :] += _dot(p, v_ref[0, pl.ds(_al(own0), nk), fr])

        if frames:
            def p2(c):
                k0 = _al(BLK + c * TQ)
                for h, fr in enumerate(frs):
                    mb = mb_ref[h]
                    p = jnp.exp(s_ref[h, :, pl.ds(k0, TQ)]
                                - jnp.concatenate([mb] * TB, axis=1)).astype(BF16)
                    acc_ref[h] += _dot(p, v_ref[0, pl.ds(k0, TQ), fr])

            chunk_loop(n_chunks, p2)

        outs = []
        for h in range(2):
            acc = acc_ref[h, :rows, :]
            outs.append(acc / pltpu.roll(acc, HALF, 1))
        o_ref[0, pl.ds(r0, rows), :] = _gated(jnp.where(lane < HALF, outs[0], outs[1]),
                                              g_ref[0, pl.ds(r0, rows), :])

    emit(0, BLK, 0, False)

    def tiles(i, _):
        for u in range(2):
            t = 2 * i + u
            emit(BLK + t * TQ, TQ, t, True)
        return 0

    lax.fori_loop(0, nt // 2, tiles, 0)


def _softmax_attention(q, kt, v, gate, mode):
    b, n, w = q.shape
    nt = (n - BLK) // TQ
    assert nt % 2 == 0
    pairs = w // (2 * LANES)
    return pl.pallas_call(
        functools.partial(_softmax_kernel, nt=nt, mode=mode),
        grid=(b, pairs),
        in_specs=[pl.BlockSpec((1, n, 2 * LANES), lambda i, j: (i, 0, j)),
                  pl.BlockSpec((1, 2 * LANES, n), lambda i, j: (i, j, 0)),
                  pl.BlockSpec((1, n, 2 * LANES), lambda i, j: (i, 0, j)),
                  pl.BlockSpec((1, n, LANES), lambda i, j: (i, 0, pairs + j))],
        out_specs=pl.BlockSpec((1, n, LANES), lambda i, j: (i, 0, j)),
        out_shape=jax.ShapeDtypeStruct((b, n, w // 2), BF16),
        scratch_shapes=[pltpu.VMEM((2, TQ, n), F32),
                        pltpu.VMEM((2, TQ, TQ), F32),
                        pltpu.VMEM((2, TQ, BLK), F32),
                        pltpu.VMEM((2, TQ, BLK), F32),
                        pltpu.VMEM((2, TQ, BLK), F32),
                        pltpu.VMEM((2, TQ, BLK), F32),
                        pltpu.VMEM((TQ, TQ), F32)],
        compiler_params=pltpu.CompilerParams(
            dimension_semantics=("parallel", "parallel"), vmem_limit_bytes=VMEM_LIMIT),
        name="softmax_attention_" + mode,
    )(q, kt, v, gate)


def _swa_kernel(sink_ref, q_ref, kt_ref, v_ref, g_ref, o_ref, bias_ref, *, nb):
    rowi, lane = _iotas(BLK)
    group = N_HEADS // SWA_KV_HEADS
    zero_kt = jnp.zeros((HALF, BLK), BF16)
    r2 = lax.broadcasted_iota(jnp.int32, (2 * BLK, BLK), 0)
    c2 = lax.broadcasted_iota(jnp.int32, (2 * BLK, BLK), 1)
    ones_st = jnp.where((r2 >= BLK) == (c2 >= HALF), 1.0, 0.0).astype(BF16)
    slopes = [2.0 ** (-8.0 * (hd + 1) / N_HEADS) for hd in range(N_HEADS)]
    GENERIC = 2

    def key_slots(m):
        return ((0, jnp.where(m >= 1, LEAD, NO_KEY)),
                (jnp.maximum(m - 1, 0), jnp.where(m >= 2, 0, NO_KEY)),
                (m, LEAD))

    def slot_biases(m):
        qpos = m * BLK + rowi
        cq = _chunk_of(qpos)
        out = []
        for jb, lo in key_slots(m):
            kpos = jb * BLK + lane
            ck = _chunk_of(kpos)
            msk = (kpos >= lo) & (ck <= cq) & ((cq - ck <= WINDOW_CHUNKS) | (kpos < BLK))
            out.append((msk, jnp.abs(qpos - kpos).astype(F32)))
        return out

    for slot, (msk, dist) in enumerate(slot_biases(GENERIC)):
        for hd in range(N_HEADS):
            bias_ref[hd, slot] = jnp.where(msk, -slopes[hd] * dist, NEG)

    def q_block(m, generic):
        r0 = _al(m * BLK)
        slots = key_slots(m)
        if generic:
            extra = jnp.float32(BLK) * (m - GENERIC)
            biases = lambda hd: [bias_ref[hd, 0] - slopes[hd] * extra, bias_ref[hd, 1], bias_ref[hd, 2]]
        else:
            special = slot_biases(m)
            biases = lambda hd: [jnp.where(msk, -slopes[hd] * dist, NEG) for msk, dist in special]
        for g in range(SWA_KV_HEADS):
            kbds, wcats = [], []
            for jb, _ in slots:
                k0 = _al(jb * BLK)
                ktg =kt_ref[0, g * HALF:(g + 1) * HALF, pl.ds(k0, BLK)]
                kbds.append(jnp.concatenate([jnp.concatenate([ktg, zero_kt], axis=1),
                                             jnp.concatenate([zero_kt, ktg], axis=1)], axis=0))
                vst = jnp.concatenate([v_ref[0, pl.ds(k0, BLK), (2 * g) * LANES:(2 * g + 1) * LANES],
                                       v_ref[0, pl.ds(k0, BLK), (2 * g + 1) * LANES:(2 * g + 2) * LANES]],
                                      axis=0)
                wcats.append(jnp.concatenate([vst, ones_st], axis=1))
            for pr in range(group // 2):
                pair = g * (group // 2) + pr
                q2 = q_ref[0, pl.ds(r0, BLK), pair * LANES:(pair + 1) * LANES]
                zs = [_dot(q2, kbd) for kbd in kbds]
                ps, sink_terms = [[], []], []
                for h in range(2):
                    head = 2 * pair + h
                    ss = [z[:, h * BLK:(h + 1) * BLK] + bias for z, bias in zip(zs, biases(head))]
                    sink = sink_ref[head]
                    mx = jnp.maximum(jnp.maximum(ss[0], ss[1]), ss[2])
                    m_row = jnp.maximum(jnp.max(mx, axis=1, keepdims=True), sink)
                    ps[h] = [jnp.exp(s - m_row).astype(BF16) for s in ss]
                    sink_terms.append(jnp.exp(sink - m_row))
                res = jnp.zeros((BLK, 2 * BLK), F32)
                for i in range(len(slots)):
                    res = res + _dot(jnp.concatenate([ps[0][i], ps[1][i]], axis=1), wcats[i])
                denom = res[:, BLK:] + jnp.where(lane < HALF, sink_terms[0], sink_terms[1])
                cols = slice(pair * LANES, (pair + 1) * LANES)
                o_ref[0, pl.ds(r0, BLK), cols] = _gated(res[:, :BLK] / denom, g_ref[0, pl.ds(r0, BLK), cols])

    for m in range(GENERIC):
        q_block(m, False)

    def q_blocks(i, carry):
        for u in range(SWA_UNROLL):
            q_block(GENERIC + SWA_UNROLL * i + u, True)
        return carry

    n_generic = nb - GENERIC
    lax.fori_loop(0, n_generic // SWA_UNROLL, q_blocks, 0)
    for m in range(nb - n_generic % SWA_UNROLL, nb):
        q_block(jnp.int32(m), True)


def _swa_attention(sinks, q, kt, v, gate):
    b, n, w = q.shape
    nb = n // BLK
    return pl.pallas_call(
        functools.partial(_swa_kernel, nb=nb),
        grid=(b,),
        in_specs=[
            pl.BlockSpec(memory_space=pltpu.SMEM),
            pl.BlockSpec((1, n, w), lambda i: (i, 0, 0)),
            pl.BlockSpec((1, kt.shape[1], n), lambda i: (i, 0, 0)),
            pl.BlockSpec((1, n, v.shape[2]), lambda i: (i, 0, 0)),
            pl.BlockSpec((1, n, w), lambda i: (i, 0, 0)),
        ],
        out_specs=pl.BlockSpec((1, n, w), lambda i: (i, 0, 0)),
        out_shape=jax.ShapeDtypeStruct((b, n, w), BF16),
        scratch_shapes=[pltpu.VMEM((N_HEADS, 3, BLK, BLK), F32)],
        compiler_params=pltpu.CompilerParams(
            dimension_semantics=("parallel",), vmem_limit_bytes=VMEM_LIMIT),
        name="swa_attention",
    )(sinks, q, kt, v, gate)


def _spread_heads(x, fill):
    lane = lax.broadcasted_iota(jnp.int32, (x.shape[0], LANES), 1)
    blocks = []
    for p in range(x.shape[1] // LANES):
        xp = x[:, p * LANES:(p + 1) * LANES]
        for odd in range(2):
            blk = 2 * p + odd
            if isinstance(fill, float):
                f = fill
            elif callable(fill):
                f = fill(lane)
            else:
                f = fill[:, blk * LANES:(blk + 1) * LANES]
            blocks.append(jnp.where((lane >= HALF) if odd else (lane < HALF), xp, f))
    return jnp.concatenate(blocks, axis=1)


def _even_proj_kernel(x_ref, wsb_ref, wc_ref, wkr_ref, wg_ref, gq_ref, gkv_ref,
                      wa_ref, wb_ref, wkv_ref, cq_ref, sq_ref, ck_ref, sk_ref,
                      qs_ref, kst_ref, vs_ref, qm_ref, kmt_ref, vm_ref, gate_ref):
    xb = x_ref[0].astype(BF16)
    w = N_HEADS * HEAD_DIM
    sb = _dot(xb, wsb_ref[...])
    qs_ref[0] = (sb[:, :w] * LOG2E).astype(BF16)
    kst_ref[0] = sb[:, w:2 * w].T.astype(BF16)
    vs_ref[0] = _spread_heads(sb[:, 2 * w:], 0.0).astype(BF16)
    gate_ref[0] = _dot(xb, wg_ref[...])

    c = _dot(xb, wc_ref[...])

    def rms(t, g):
        return (t * lax.rsqrt(jnp.mean(t * t, axis=-1, keepdims=True) + RMS_EPS) * g).astype(BF16)

    xq = rms(c[:, :MLA_RANK], gq_ref[...])
    xkv = rms(c[:, MLA_RANK:], gkv_ref[...])
    pairs = N_HEADS // 2
    cq = jnp.concatenate([cq_ref[...]] * pairs, axis=1)
    sq = jnp.concatenate([sq_ref[...]] * pairs, axis=1)
    qm_ref[0] = (_dot(xq, wa_ref[...]) * cq + _dot(xq, wb_ref[...]) * sq).astype(BF16)
    kr = _dot(xb, wkr_ref[...])
    krr = kr[:, :LANES] * ck_ref[...] + kr[:, LANES:] * sk_ref[...]
    krr = (krr, pltpu.roll(krr, HALF, 1))
    kv = _dot(xkv, wkv_ref[...])
    lane = lax.broadcasted_iota(jnp.int32, (kv.shape[0], LANES), 1)
    kblocks, vblocks = [], []
    for hd in range(N_HEADS):
        blk = slice(hd * LANES, (hd + 1) * LANES)
        v_half = (lane >= HALF) if hd % 2 else (lane < HALF)
        kblocks.append(jnp.where(v_half, krr[hd % 2], kv[:, blk]))
        vblocks.append(jnp.where(v_half, kv[:, blk], 1.0))
    kmt_ref[0] = jnp.concatenate(kblocks, axis=1).T.astype(BF16)
    vm_ref[0] = jnp.concatenate(vblocks, axis=1).astype(BF16)


def _even_proj(h, wts, tabs, tm):
    b, n, d = h.shape
    grid = (n // tm, b)
    row = lambda i, j: (j, i, 0)
    col = lambda i, j: (j, 0, i)
    const = lambda i, j: (0, 0)
    tab = lambda i, j: (i, 0)
    full = lambda a: pl.BlockSpec(a.shape, const)
    wide = N_HEADS * LANES
    narrow = N_HEADS * HEAD_DIM
    outs = [(narrow, BF16, False), (narrow, BF16, True), (wide, BF16, False),
            (wide, BF16, False), (wide, BF16, True), (wide, BF16, False), (d, F32, False)]
    return pl.pallas_call(
        _even_proj_kernel,
        grid=grid,
        in_specs=[pl.BlockSpec((1, tm, d), row)] + [full(a) for a in wts] + [
            pl.BlockSpec((tm, 2 * LANES), tab), pl.BlockSpec((tm, 2 * LANES), tab),
            pl.BlockSpec((tm, LANES), tab), pl.BlockSpec((tm, LANES), tab)],
        out_specs=[pl.BlockSpec((1, c, tm), col) if tr else pl.BlockSpec((1, tm, c), row)
                   for c, _, tr in outs],
        out_shape=[jax.ShapeDtypeStruct((b, c, n) if tr else (b, n, c), t) for c, t, tr in outs],
        compiler_params=pltpu.CompilerParams(
            dimension_semantics=("parallel", "parallel"), vmem_limit_bytes=VMEM_LIMIT),
        name="even_proj",
    )(h, *wts, *tabs)


def _odd_proj_kernel(x_ref, wqc_ref, wkc_ref, wvc_ref, wqf_ref, wkf_ref, wvf_ref, wfz_ref, wg_ref,
                     bf_ref, pf_ref,
                     qc_ref, kct_ref, vc_ref, qf_ref, kft_ref, vf_ref, gate_ref, carry_ref, *, tm):
    i = pl.program_id(1)
    xb = x_ref[0].astype(BF16)
    qc_ref[0] = _dot(xb, wqc_ref[...]).astype(BF16)
    kct_ref[0] = _dot(xb, wkc_ref[...]).T.astype(BF16)
    vc_ref[0] = _dot(xb, wvc_ref[...]).astype(BF16)
    gate_ref[0] = _dot(xb, wg_ref[...])

    vf_ref[0] = _spread_heads(_dot(xb, wvf_ref[...]), 1.0).astype(BF16)
    qf_ref[0] = _spread_heads(_dot(xb, wqf_ref[...]),
                              lambda lane: jnp.where((lane & (HALF - 1)) < 3, 1.0, 0.0)).astype(BF16)

    fz = _dot(xb, wfz_ref[...]) + bf_ref[...]
    r = lax.broadcasted_iota(jnp.int32, (tm, LANES), 0) + i * tm
    c = lax.broadcasted_iota(jnp.int32, (tm, LANES), 1)
    lf = jnp.where((c < N_HEADS) & (r >= LEAD), _log_sigmoid(fz), 0.0)

    def split3(t):
        a = t.astype(BF16)
        rem = t - a.astype(F32)
        bb = rem.astype(BF16)
        cc = (rem - bb.astype(F32)).astype(BF16)
        return jnp.concatenate([a, bb, cc], axis=1)

    rr = lax.broadcasted_iota(jnp.int32, (tm, tm), 0)
    cc = lax.broadcasted_iota(jnp.int32, (tm, tm), 1)
    tri = jnp.where(cc <= rr, 1.0, 0.0).astype(BF16)

    @pl.when(i == 0)
    def _():
        carry_ref[...] = jnp.zeros_like(carry_ref)

    c3 = _dot(tri, split3(lf))
    cum = c3[:, :LANES] + c3[:, LANES:2 * LANES] + c3[:, 2 * LANES:] + carry_ref[0:1, :]
    carry_ref[...] = jnp.broadcast_to(cum[tm - 1:tm, :], carry_ref.shape)
    kft_ref[0] = _spread_heads(_dot(xb, wkf_ref[...]), _dot(split3(-cum), pf_ref[...])).T.astype(BF16)


def _odd_proj(h, wts, tm):
    b, n, d = h.shape
    row = lambda j, i: (j, i, 0)
    col = lambda j, i: (j, 0, i)
    const = lambda j, i: (0, 0)
    full = lambda a: pl.BlockSpec(a.shape, const)
    wide = N_HEADS * LANES
    narrow = N_HEADS * HEAD_DIM
    outs = [(narrow, BF16, False), (SWA_KV_HEADS * HEAD_DIM, BF16, True), (narrow, BF16, False),
            (wide, BF16, False), (wide, BF16, True), (wide, BF16, False), (d, F32, False)]
    return pl.pallas_call(
        functools.partial(_odd_proj_kernel, tm=tm),
        grid=(b, n // tm),
        in_specs=[pl.BlockSpec((1, tm, d), row)] + [full(a) for a in wts],
        out_specs=[pl.BlockSpec((1, c, tm), col) if tr else pl.BlockSpec((1, tm, c), row)
                   for c, _, tr in outs],
        out_shape=[jax.ShapeDtypeStruct((b, c, n) if tr else (b, n, c), t) for c, t, tr in outs],
        scratch_shapes=[pltpu.VMEM((8, LANES), F32)],
        compiler_params=pltpu.CompilerParams(
            dimension_semantics=("parallel", "arbitrary"), vmem_limit_bytes=VMEM_LIMIT),
        name="odd_proj",
    )(h, *wts)


def _out_math(ma, mb, h, w_ref, g, b):
    half = ma.shape[-1]
    y = _dot(ma, w_ref[:half, :]) + _dot(mb, w_ref[half:, :])
    r = DN_ALPHA * h + y
    mu = jnp.mean(r, axis=-1, keepdims=True)
    cen = r - mu
    var = jnp.mean(cen * cen, axis=-1, keepdims=True)
    return cen * lax.rsqrt(var + LN_EPS) * g + b


def _out_rows(ma_ref, mb_ref, h_ref, w_ref, g_ref, b_ref, store):
    rows = ma_ref.shape[0]
    part = OUT_PART if rows % OUT_PART == 0 else rows
    for r in range(rows // part):
        sl = slice(r * part, (r + 1) * part)
        store(sl, _out_math(ma_ref[sl, :], mb_ref[sl, :], h_ref[sl, :], w_ref, g_ref[...], b_ref[...]))


def _out_kernel(ma_ref, mb_ref, h_ref, w_ref, g_ref, b_ref, o_ref):
    def store(sl, val):
        o_ref[sl, :] = val

    _out_rows(ma_ref, mb_ref, h_ref, w_ref, g_ref, b_ref, store)


def _out_frames_kernel(ma_ref, mb_ref, h_ref, w_ref, g_ref, b_ref, o_ref):
    def store(sl, val):
        o_ref[0, sl, :] = val

    _out_rows(ma_ref, mb_ref, h_ref, w_ref, g_ref, b_ref, store)


def _out_proj_frames(ma, mb, h, w, g, bias):
    b, n, d = h.shape
    half = ma.shape[-1]
    flat = lambda a: a.reshape(b * n, a.shape[-1])
    src = lambda i, j: (pl.multiple_of(i * n + BLK + j * OUT_TM, BLK), 0)
    window = lambda width: pl.BlockSpec((pl.Element(OUT_TM), pl.Element(width)), src)
    const = lambda i, j: (0, 0)
    return pl.pallas_call(
        _out_frames_kernel,
        grid=(b, (n - BLK) // OUT_TM),
        in_specs=[window(half), window(half), window(d),
                  pl.BlockSpec(w.shape, const), pl.BlockSpec((1, d), const), pl.BlockSpec((1, d), const)],
        out_specs=pl.BlockSpec((1, OUT_TM, d), lambda i, j: (i, j, 0)),
        out_shape=jax.ShapeDtypeStruct((b, n - BLK, d), F32),
        compiler_params=pltpu.CompilerParams(
            dimension_semantics=("parallel", "parallel"), vmem_limit_bytes=VMEM_LIMIT),
        name="out_proj_ln_frames",
    )(flat(ma), flat(mb), flat(h), w, g, bias)


def _out_proj(ma, mb, h, w, g, bias, tm):
    rows, d = h.shape
    half = ma.shape[-1]
    row = lambda i: (i, 0)
    const = lambda i: (0, 0)
    return pl.pallas_call(
        _out_kernel,
        grid=(rows // tm,),
        in_specs=[pl.BlockSpec((tm, half), row), pl.BlockSpec((tm, half), row),
                  pl.BlockSpec((tm, d), row),
                  pl.BlockSpec(w.shape, const), pl.BlockSpec((1, d), const), pl.BlockSpec((1, d), const)],
        out_specs=pl.BlockSpec((tm, d), row),
        out_shape=jax.ShapeDtypeStruct((rows, d), F32),
        compiler_params=pltpu.CompilerParams(
            dimension_semantics=("parallel",), vmem_limit_bytes=VMEM_LIMIT),
        name="out_proj_ln",
    )(ma, mb, h, w, g, bias)


def _by_parity(even, odd):
    is_odd = (jnp.arange(N_HEADS) % 2 == 1)[:, None]
    return jnp.where(is_odd, odd, even).reshape(even.shape[:-2] + (N_HEADS * LANES,))


def _mla_head_blocks(nope, rot):
    spare = jnp.zeros(rot.shape[:-1] + (HALF - MLA_ROPE,), rot.dtype)
    return _by_parity(jnp.concatenate([rot, spare, nope], axis=-1),
                      jnp.concatenate([nope, rot, spare], axis=-1))


def _even_weights(w_in, g_cq, g_ckv, w_uq, w_ukv):
    a = N_HEADS * HEAD_DIM
    wq, wk, wv = w_in[:, :a], w_in[:, a:2 * a], w_in[:, 2 * a:3 * a]
    o = 3 * a
    w_c = w_in[:, o:o + 2 * MLA_RANK]
    o += 2 * MLA_RANK
    w_kr = w_in[:, o:o + MLA_ROPE]
    o += MLA_ROPE
    w_g = w_in[:, o:]
    w_sb = jnp.concatenate([wq * HEAD_DIM ** -0.5, wk, wv], axis=1)
    hr = MLA_ROPE // 2
    zpad = jnp.zeros((w_in.shape[0], LANES - MLA_ROPE), w_in.dtype)
    w_kr2 = jnp.concatenate([w_kr, zpad, w_kr[:, hr:], w_kr[:, :hr], zpad], axis=1)

    uq = w_uq.reshape(MLA_RANK, N_HEADS, MLA_NOPE + MLA_ROPE)
    ukv = w_ukv.reshape(MLA_RANK, N_HEADS, MLA_NOPE + HEAD_DIM)
    nope, r1, r2 = uq[..., :MLA_NOPE], uq[..., MLA_NOPE:MLA_NOPE + hr], uq[..., MLA_NOPE + hr:]
    w_a = _mla_head_blocks(nope, jnp.concatenate([r1, r2], axis=-1))
    w_b = _mla_head_blocks(jnp.zeros_like(nope), jnp.concatenate([r2, r1], axis=-1))
    k_nope, v = ukv[..., :MLA_NOPE], ukv[..., MLA_NOPE:]
    w_kv = _by_parity(jnp.concatenate([v, k_nope], axis=-1), ukv)
    bf = lambda t: t.astype(BF16)
    return (bf(w_sb), bf(w_c), bf(w_kr2), bf(w_g), g_cq[None, :], g_ckv[None, :],
            bf(w_a), bf(w_b), bf(w_kv))


def _rope_tables(n):
    hr = MLA_ROPE // 2
    pos = jnp.arange(n, dtype=F32) - LEAD
    inv = ROPE_BASE ** (-jnp.arange(hr, dtype=F32) / hr)
    ang = pos[:, None] * inv[None, :]
    cos, sin = jnp.cos(ang), jnp.sin(ang)
    ones = jnp.ones((n, MLA_NOPE), F32)
    scale = (MLA_NOPE + MLA_ROPE) ** -0.5
    c_rot, s_rot = jnp.concatenate([cos, cos], axis=1), jnp.concatenate([-sin, sin], axis=1)
    spare = jnp.zeros((n, HALF - MLA_ROPE), F32)
    c_q = jnp.concatenate([c_rot, spare, ones, ones, c_rot, spare], axis=1) * scale
    s_q = jnp.concatenate([s_rot, spare, 0.0 * ones, 0.0 * ones, s_rot, spare], axis=1) * scale
    z96 = jnp.zeros((n, LANES - MLA_ROPE), F32)
    c_k = jnp.concatenate([c_rot, z96], axis=1)
    s_k = jnp.concatenate([s_rot, z96], axis=1)
    return c_q, s_q, c_k, s_k


def _odd_weights(w_in, b_forget):
    a = N_HEADS * HEAD_DIM
    kvw = SWA_KV_HEADS * HEAD_DIM
    wqc = w_in[:, :a]
    o = a
    wkc = w_in[:, o:o + kvw]
    o += kvw
    wvc = w_in[:, o:o + kvw]
    o += kvw
    wqd, wkd, wvd = w_in[:, o:o + a], w_in[:, o + a:o + 2 * a], w_in[:, o + 2 * a:o + 3 * a]
    o += 3 * a
    wfz = w_in[:, o:o + N_HEADS]
    o += N_HEADS
    w_g = w_in[:, o:]
    scale = HEAD_DIM ** -0.5
    v0, v1 = wvc[:, :HEAD_DIM], wvc[:, HEAD_DIM:]
    z = jnp.zeros_like(v0)
    w_vc = jnp.concatenate([v0, z, z, v0, v1, z, z, v1], axis=1)
    w_fz = jnp.concatenate([wfz, jnp.zeros((w_in.shape[0], LANES - N_HEADS), w_in.dtype)], axis=1)
    b_f = jnp.concatenate([b_forget, jnp.zeros((LANES - N_HEADS,), b_forget.dtype)])[None, :]
    bf = lambda t: t.astype(BF16)
    return (bf(wqc * scale), bf(wkc), bf(w_vc), bf(wqd * scale), bf(wkd), bf(wvd), bf(w_fz), bf(w_g),
            b_f.astype(F32))


def _piece_placement():
    p_f = np.zeros((3 * LANES, N_HEADS * LANES), np.float32)
    for t in range(3):
        for h in range(N_HEADS):
            p_f[t * LANES + h, h * LANES + (0 if h % 2 else HALF) + t] = 1.0
    return jnp.asarray(p_f, BF16)


def _row_tile(nb, cap):
    g = max(t for t in range(1, cap + 1) if nb % t == 0)
    return g * BLK


def kernel(x, meta_tokens, w_in_even, g_cq, g_ckv, w_uq, w_ukv, w_out_even,
           w_in_odd, b_forget, sink_logits, w_out_odd, ln_gain, ln_bias):
    b, s, d = x.shape
    assert s % SB_TQ == 0 and s % TQ == 0 and d == D_MODEL
    n = BLK + s
    nb = n // BLK
    h = jnp.concatenate([jnp.zeros((b, LEAD, d), x.dtype),
                         jnp.broadcast_to(meta_tokens[None].astype(x.dtype), (b, N_META, d)), x], axis=1)
    tabs = _rope_tables(n)
    tm_proj = _row_tile(nb, 3)
    tm_out = _row_tile(b * nb, OUT_TM // BLK)
    even_wts = jax.vmap(_even_weights)(w_in_even, g_cq, g_ckv, w_uq, w_ukv)
    odd_wts = jax.vmap(_odd_weights)(w_in_odd, b_forget)
    p_f = _piece_placement()
    for layer in range(DEPTH):
        i = layer // 2
        if layer % 2 == 0:
            wts = tuple(t[i] for t in even_wts)
            qs, kst, vs, qm, kmt, vm, gate = _even_proj(h, wts, tabs, tm_proj)
            mix_a = _sb_attention(qs, kst, vs, gate)
            mix_b = _softmax_attention(qm, kmt, vm, gate, "chunk")
            w_out = w_out_even[i]
        else:
            wts = tuple(t[i] for t in odd_wts) + (p_f,)
            qc, kct, vc, qf, kft, vf, gate = _odd_proj(h, wts, tm_proj)
            mix_a = _swa_attention(sink_logits[i].astype(F32), qc, kct, vc, gate)
            mix_b = _softmax_attention(qf, kft, vf, gate, "causal")
            w_out = w_out_odd[i]
        ln = (w_out.astype(BF16), ln_gain[layer][None, :], ln_bias[layer][None, :])
        if layer == DEPTH - 1:
            return _out_proj_frames(mix_a, mix_b, h, *ln)
        rows = b * n
        h = _out_proj(mix_a.reshape(rows, -1), mix_b.reshape(rows, -1), h.reshape(rows, d),
                      *ln, tm_out).reshape(b, n, d)
```

```python
import functools

import jax
import jax.numpy as jnp
import numpy as np
from jax import lax
from jax.experimental import pallas as pl
from jax.experimental.pallas import tpu as pltpu

D_MODEL = 1024
CHUNK = 64
N_META = 16
HEAD_DIM = 64
N_HEADS = 8
MLA_NOPE = 64
MLA_ROPE = 32
MLA_RANK = 256
SWA_KV_HEADS = 2
WINDOW_CHUNKS = 2
ROPE_BASE = 10000.0
DEPTH = 4
DN_ALPHA = (2 * DEPTH) ** 0.25
LN_EPS = 1e-5
RMS_EPS = 1e-6
NEG = -1e30
NO_KEY = 1 << 30
SIGN_BIT = -(1 << 31)
LOG2E = 1.4426950408889634

LANES = 128
BLK = 128
TB = 4
TQ = TB * BLK
SB_TB = 8
SB_TQ = SB_TB * BLK
SB_UNROLL = 8
SWA_UNROLL = 5
OUT_TM = 1024
OUT_PART = 256
LEAD = BLK - N_META
HALF = LANES // 2
VMEM_LIMIT = 56 * 1024 * 1024

F32 = jnp.float32
BF16 = jnp.bfloat16


def _dot(a, b):
    return jnp.dot(a, b, preferred_element_type=F32)


def _chunk_of(pos):
    return jnp.maximum((pos >> 6) - 1, 0)


def _log_sigmoid(x):
    return jnp.minimum(x, 0.0) - jnp.log(1.0 + jnp.exp(-jnp.abs(x)))


def _al(x):
    return x if isinstance(x, int) else pl.multiple_of(x, BLK)


def _gated(mixed, gate):
    return (mixed * (gate * jax.nn.sigmoid(gate))).astype(BF16)


def _iotas(rows):
    return (lax.broadcasted_iota(jnp.int32, (rows, BLK), 0),
            lax.broadcasted_iota(jnp.int32, (rows, BLK), 1))


def _sb_kernel(q_ref, kt_ref, v_ref, g_ref, o_ref, u_ref, carry_ref, acc_ref, *, nt):
    r2 = lax.broadcasted_iota(jnp.int32, (2 * BLK, 2 * BLK), 0)
    c2 = lax.broadcasted_iota(jnp.int32, (2 * BLK, 2 * BLK), 1)
    u_ref[...] = jnp.where(((r2 >= BLK) == (c2 >= BLK)) & ((r2 & (BLK - 1)) > (c2 & (BLK - 1))),
                           1.0, 0.0).astype(BF16)
    zero_kt = jnp.zeros((HALF, BLK), BF16)

    def step(row0, off, rows, jb, mask=None):
        r0 = _al(row0 + off)
        k0 = _al(jb * BLK)
        sl = slice(off, off + rows)
        ktb = kt_ref[0, :, pl.ds(k0, BLK)]
        kbd = jnp.concatenate([jnp.concatenate([ktb[:HALF], zero_kt], axis=1),
                               jnp.concatenate([zero_kt, ktb[HALF:]], axis=1)], axis=0)
        z2 = _dot(q_ref[0, pl.ds(r0, rows), :], kbd)
        vst = jnp.concatenate([v_ref[0, pl.ds(k0, BLK), :LANES],
                               v_ref[0, pl.ds(k0, BLK), LANES:]], axis=0)
        if mask == "meta_keys":
            parts = [(0, rows, _iotas(rows)[1] >= LEAD)]
        elif mask is not None:
            rowi, lane = _iotas(BLK)
            local = lane < rowi
            parts = [(0, BLK, local if mask == "diagonal" else local & (lane >= LEAD))]
            if rows > BLK:
                parts.append((BLK, rows, None))

        def masked(x):
            if mask is None:
                return x
            return jnp.concatenate([x[a:b] if m is None else jnp.where(m, x[a:b], 0.0)
                                    for a, b, m in parts], axis=0)

        stays, log_betas = [], []
        for h in range(2):
            z = z2[:, h * BLK:(h + 1) * BLK]
            neg_abs = lax.bitcast_convert_type(
                lax.bitcast_convert_type(z, jnp.int32) | SIGN_BIT, F32)
            stay = jnp.maximum(z, 0.0) + jnp.log(1.0 + jnp.exp2(neg_abs)) * LOG2E
            log_betas.append(z - stay)
            stays.append(masked(stay))
        cs = _dot(jnp.concatenate([s.astype(BF16) for s in stays], axis=1), u_ref[...])
        ws = []
        for h in range(2):
            carry = carry_ref[h, sl, :]
            w = masked(jnp.exp2(log_betas[h] - (cs[:, h * BLK:(h + 1) * BLK] + carry)))
            carry_ref[h, sl, :] = carry + jnp.sum(stays[h], axis=1, keepdims=True)
            ws.append(w.astype(BF16))
        acc_ref[sl, :] += _dot(jnp.concatenate(ws, axis=1), vst)

    def reset(rows):
        carry_ref[:, :rows, :] = jnp.zeros((2, rows, BLK), F32)
        acc_ref[:rows, :] = jnp.zeros((rows, BLK), F32)

    reset(BLK)
    step(0, 0, BLK, 0, "meta_tile")
    o_ref[0, :BLK, :] = _gated(acc_ref[:BLK, :], g_ref[0, :BLK, :])

    def tile(t, _):
        row0 = BLK + t * SB_TQ
        jb0 = 1 + t * SB_TB
        reset(SB_TQ)
        for d in reversed(range(SB_TB)):
            step(row0, d * BLK, SB_TQ - d * BLK, jb0 + d, "diagonal")

        def body(i, c):
            for u in range(SB_UNROLL):
                step(row0, 0, SB_TQ, jb0 - 1 - (i * SB_UNROLL + u))
            return c

        lax.fori_loop(0, t * (SB_TB // SB_UNROLL), body, 0)
        step(row0, 0, SB_TQ, 0, "meta_keys")
        o_ref[0, pl.ds(_al(row0), SB_TQ), :] = _gated(acc_ref[...], g_ref[0, pl.ds(_al(row0), SB_TQ), :])
        return 0

    lax.fori_loop(0, nt, tile, 0)


def _sb_attention(q, kt, v, gate):
    b, n, w = q.shape
    nt = (n - BLK) // SB_TQ
    return pl.pallas_call(
        functools.partial(_sb_kernel, nt=nt),
        grid=(b, w // LANES),
        in_specs=[pl.BlockSpec((1, n, LANES), lambda i, j: (i, 0, j)),
                  pl.BlockSpec((1, LANES, n), lambda i, j: (i, j, 0)),
                  pl.BlockSpec((1, n, 2 * LANES), lambda i, j: (i, 0, j)),
                  pl.BlockSpec((1, n, LANES), lambda i, j: (i, 0, j))],
        out_specs=pl.BlockSpec((1, n, LANES), lambda i, j: (i, 0, j)),
        out_shape=jax.ShapeDtypeStruct((b, n, w), BF16),
        scratch_shapes=[pltpu.VMEM((2 * BLK, 2 * BLK), BF16),
                        pltpu.VMEM((2, SB_TQ, BLK), F32),
                        pltpu.VMEM((SB_TQ, BLK), F32)],
        compiler_params=pltpu.CompilerParams(
            dimension_semantics=("parallel", "parallel"), vmem_limit_bytes=VMEM_LIMIT),
        name="sb_attention",
    )(q, kt, v, gate)


def _softmax_kernel(q_ref, kt_ref, v_ref, g_ref, o_ref, s_ref, so_ref, sd_ref, mx_ref, mb_ref, acc_ref,
                    bias_ref, *, nt, mode):
    def visible(kpos, qpos, lo):
        if mode == "chunk":
            return (kpos >= lo) & (_chunk_of(kpos) <= _chunk_of(qpos))
        return (kpos >= lo) & (kpos <= qpos)

    ql = BLK + lax.broadcasted_iota(jnp.int32, (TQ, TQ), 0)
    kl = BLK + lax.broadcasted_iota(jnp.int32, (TQ, TQ), 1)
    bias_ref[...] = jnp.where(visible(kl, ql, 0), 0.0, NEG)

    def chunk_loop(n_chunks, chunk_fn):
        def quad(i, carry):
            for u in range(4):
                chunk_fn(4 * i + u)
            return carry

        lax.fori_loop(0, n_chunks >> 2, quad, 0)
        done = (n_chunks >> 2) << 2

        @pl.when((n_chunks & 2) != 0)
        def _():
            chunk_fn(done)
            chunk_fn(done + 1)

        @pl.when((n_chunks & 1) != 0)
        def _():
            chunk_fn(n_chunks - 1)

    def emit(row0, rows, n_chunks, frames):
        frs = [slice(h * LANES, (h + 1) * LANES) for h in range(2)]
        r0 = _al(row0)
        rowi, lane = _iotas(rows)

        def chunk_scores(c, h):
            k0 = _al(BLK + c * TQ)
            return _dot(q_ref[0, pl.ds(r0, rows), frs[h]], kt_ref[0, frs[h], pl.ds(k0, TQ)])

        def fold_max(c, h, s):
            s_ref[h, :, pl.ds(_al(BLK + c * TQ), TQ)] = s
            m4 = jnp.maximum(jnp.maximum(s[:, :BLK], s[:, BLK:2 * BLK]),
                             jnp.maximum(s[:, 2 * BLK:3 * BLK], s[:, 3 * BLK:]))
            mx_ref[h] = jnp.maximum(mx_ref[h], m4)

        for h, fr in enumerate(frs):
            s0 = _dot(q_ref[0, pl.ds(r0, rows), fr], kt_ref[0, fr, :BLK])
            s0 = jnp.where(visible(lane, r0 + rowi, LEAD), s0, NEG)
            sd_ref[h, :rows, :] = s0
            mx_ref[h, :rows, :] = s0
        own0 = BLK + n_chunks * TQ
        own_parts = ((0, TQ // 2, TQ // 2), (TQ // 2, TQ, TQ))
        if frames:
            for h, fr in enumerate(frs):
                for a, b, nk in own_parts:
                    s = _dot(q_ref[0, pl.ds(_al(r0 + a), b - a), fr], kt_ref[0, fr, pl.ds(_al(own0), nk)])
                    s = s + bias_ref[a:b, :nk]
                    so_ref[h, a:b, :nk] = s
                    m = s[:, :BLK]
                    for j in range(1, nk // BLK):
                        m = jnp.maximum(m, s[:, j * BLK:(j + 1) * BLK])
                    mx_ref[h, a:b, :] = jnp.maximum(mx_ref[h, a:b, :], m)

            def p1(c):
                for h in range(2):
                    fold_max(c, h, chunk_scores(c, h))

            chunk_loop(n_chunks, p1)

        for h, fr in enumerate(frs):
            m_row = jnp.max(mx_ref[h, :rows, :], axis=1, keepdims=True)
            mb_ref[h, :rows, :] = jnp.broadcast_to(m_row, (rows, BLK))
            p0 = jnp.exp(sd_ref[h, :rows, :] - mb_ref[h, :rows, :]).astype(BF16)
            acc_ref[h, :rows, :] = _dot(p0, v_ref[0, :BLK, fr])
            if frames:
                for a, b, nk in own_parts:
                    mb = mb_ref[h, a:b, :]
                    p = jnp.exp(so_ref[h, a:b, :nk] - jnp.concatenate([mb] * (nk // BLK), axis=1)).astype(BF16)
                    acc_ref[h, a:b, :] += _dot(p, v_ref[0, pl.ds(_al(own0), nk), fr])

        if frames:
            def p2(c):
                k0 = _al(BLK + c * TQ)
                for h, fr in enumerate(frs):
                    mb = mb_ref[h]
                    p = jnp.exp(s_ref[h, :, pl.ds(k0, TQ)]
                                - jnp.concatenate([mb] * TB, axis=1)).astype(BF16)
                    acc_ref[h] += _dot(p, v_ref[0, pl.ds(k0, TQ), fr])

            chunk_loop(n_chunks, p2)

        outs = []
        for h in range(2):
            acc = acc_ref[h, :rows, :]
            outs.append(acc / pltpu.roll(acc, HALF, 1))
        o_ref[0, pl.ds(r0, rows), :] = _gated(jnp.where(lane < HALF, outs[0], outs[1]),
                                              g_ref[0, pl.ds(r0, rows), :])

    emit(0, BLK, 0, False)

    def tiles(i, _):
        for u in range(2):
            t = 2 * i + u
            emit(BLK + t * TQ, TQ, t, True)
        return 0

    lax.fori_loop(0, nt // 2, tiles, 0)


def _softmax_attention(q, kt, v, gate, mode):
    b, n, w = q.shape
    nt = (n - BLK) // TQ
    assert nt % 2 == 0
    pairs = w // (2 * LANES)
    return pl.pallas_call(
        functools.partial(_softmax_kernel, nt=nt, mode=mode),
        grid=(b, pairs),
        in_specs=[pl.BlockSpec((1, n, 2 * LANES), lambda i, j: (i, 0, j)),
                  pl.BlockSpec((1, 2 * LANES, n), lambda i, j: (i, j, 0)),
                  pl.BlockSpec((1, n, 2 * LANES), lambda i, j: (i, 0, j)),
                  pl.BlockSpec((1, n, LANES), lambda i, j: (i, 0, pairs + j))],
        out_specs=pl.BlockSpec((1, n, LANES), lambda i, j: (i, 0, j)),
        out_shape=jax.ShapeDtypeStruct((b, n, w // 2), BF16),
        scratch_shapes=[pltpu.VMEM((2, TQ, n), F32),
                        pltpu.VMEM((2, TQ, TQ), F32),
                        pltpu.VMEM((2, TQ, BLK), F32),
                        pltpu.VMEM((2, TQ, BLK), F32),
                        pltpu.VMEM((2, TQ, BLK), F32),
                        pltpu.VMEM((2, TQ, BLK), F32),
                        pltpu.VMEM((TQ, TQ), F32)],
        compiler_params=pltpu.CompilerParams(
            dimension_semantics=("parallel", "parallel"), vmem_limit_bytes=VMEM_LIMIT),
        name="softmax_attention_" + mode,
    )(q, kt, v, gate)


def _swa_kernel(sink_ref, q_ref, kt_ref, v_ref, g_ref, o_ref, bias_ref, *, nb):
    rowi, lane = _iotas(BLK)
    group = N_HEADS // SWA_KV_HEADS
    zero_kt = jnp.zeros((HALF, BLK), BF16)
    r2 = lax.broadcasted_iota(jnp.int32, (2 * BLK, BLK), 0)
    c2 = lax.broadcasted_iota(jnp.int32, (2 * BLK, BLK), 1)
    ones_st = jnp.where((r2 >= BLK) == (c2 >= HALF), 1.0, 0.0).astype(BF16)
    slopes = [2.0 ** (-8.0 * (hd + 1) / N_HEADS) for hd in range(N_HEADS)]
    GENERIC = 2

    def key_slots(m):
        return ((0, jnp.where(m >= 1, LEAD, NO_KEY)),
                (jnp.maximum(m - 1, 0), jnp.where(m >= 2, 0, NO_KEY)),
                (m, LEAD))

    def slot_biases(m):
        qpos = m * BLK + rowi
        cq = _chunk_of(qpos)
        out = []
        for jb, lo in key_slots(m):
            kpos = jb * BLK + lane
            ck = _chunk_of(kpos)
            msk = (kpos >= lo) & (ck <= cq) & ((cq - ck <= WINDOW_CHUNKS) | (kpos < BLK))
            out.append((msk, jnp.abs(qpos - kpos).astype(F32)))
        return out

    for slot, (msk, dist) in enumerate(slot_biases(GENERIC)):
        for hd in range(N_HEADS):
            bias_ref[hd, slot] = jnp.where(msk, -slopes[hd] * dist, NEG)

    def q_block(m, generic):
        r0 = _al(m * BLK)
        slots = key_slots(m)
        if generic:
            extra = jnp.float32(BLK) * (m - GENERIC)
            biases = lambda hd: [bias_ref[hd, 0] - slopes[hd] * extra, bias_ref[hd, 1], bias_ref[hd, 2]]
        else:
            special = slot_biases(m)
            biases = lambda hd: [jnp.where(msk, -slopes[hd] * dist, NEG) for msk, dist in special]
        for g in range(SWA_KV_HEADS):
            kbds, wcats = [], []
            for jb, _ in slots:
                k0 = _al(jb * BLK)
                ktg =kt_ref[0, g * HALF:(g + 1) * HALF, pl.ds(k0, BLK)]
                kbds.append(jnp.concatenate([jnp.concatenate([ktg, zero_kt], axis=1),
                                             jnp.concatenate([zero_kt, ktg], axis=1)], axis=0))
                vst = jnp.concatenate([v_ref[0, pl.ds(k0, BLK), (2 * g) * LANES:(2 * g + 1) * LANES],
                                       v_ref[0, pl.ds(k0, BLK), (2 * g + 1) * LANES:(2 * g + 2) * LANES]],
                                      axis=0)
                wcats.append(jnp.concatenate([vst, ones_st], axis=1))
            for pr in range(group // 2):
                pair = g * (group // 2) + pr
                q2 = q_ref[0, pl.ds(r0, BLK), pair * LANES:(pair + 1) * LANES]
                zs = [_dot(q2, kbd) for kbd in kbds]
                ps, sink_terms = [[], []], []
                for h in range(2):
                    head = 2 * pair + h
                    ss = [z[:, h * BLK:(h + 1) * BLK] + bias for z, bias in zip(zs, biases(head))]
                    sink = sink_ref[head]
                    mx = jnp.maximum(jnp.maximum(ss[0], ss[1]), ss[2])
                    m_row = jnp.maximum(jnp.max(mx, axis=1, keepdims=True), sink)
                    ps[h] = [jnp.exp(s - m_row).astype(BF16) for s in ss]
                    sink_terms.append(jnp.exp(sink - m_row))
                res = jnp.zeros((BLK, 2 * BLK), F32)
                for i in range(len(slots)):
                    res = res + _dot(jnp.concatenate([ps[0][i], ps[1][i]], axis=1), wcats[i])
                denom = res[:, BLK:] + jnp.where(lane < HALF, sink_terms[0], sink_terms[1])
                cols = slice(pair * LANES, (pair + 1) * LANES)
                o_ref[0, pl.ds(r0, BLK), cols] = _gated(res[:, :BLK] / denom, g_ref[0, pl.ds(r0, BLK), cols])

    for m in range(GENERIC):
        q_block(m, False)

    def q_blocks(i, carry):
        for u in range(SWA_UNROLL):
            q_block(GENERIC + SWA_UNROLL * i + u, True)
        return carry

    n_generic = nb - GENERIC
    lax.fori_loop(0, n_generic // SWA_UNROLL, q_blocks, 0)
    for m in range(nb - n_generic % SWA_UNROLL, nb):
        q_block(jnp.int32(m), True)


def _swa_attention(sinks, q, kt, v, gate):
    b, n, w = q.shape
    nb = n // BLK
    return pl.pallas_call(
        functools.partial(_swa_kernel, nb=nb),
        grid=(b,),
        in_specs=[
            pl.BlockSpec(memory_space=pltpu.SMEM),
            pl.BlockSpec((1, n, w), lambda i: (i, 0, 0)),
            pl.BlockSpec((1, kt.shape[1], n), lambda i: (i, 0, 0)),
            pl.BlockSpec((1, n, v.shape[2]), lambda i: (i, 0, 0)),
            pl.BlockSpec((1, n, w), lambda i: (i, 0, 0)),
        ],
        out_specs=pl.BlockSpec((1, n, w), lambda i: (i, 0, 0)),
        out_shape=jax.ShapeDtypeStruct((b, n, w), BF16),
        scratch_shapes=[pltpu.VMEM((N_HEADS, 3, BLK, BLK), F32)],
        compiler_params=pltpu.CompilerParams(
            dimension_semantics=("parallel",), vmem_limit_bytes=VMEM_LIMIT),
        name="swa_attention",
    )(sinks, q, kt, v, gate)


def _spread_heads(x, fill):
    lane = lax.broadcasted_iota(jnp.int32, (x.shape[0], LANES), 1)
    blocks = []
    for p in range(x.shape[1] // LANES):
        xp = x[:, p * LANES:(p + 1) * LANES]
        for odd in range(2):
            blk = 2 * p + odd
            if isinstance(fill, float):
                f = fill
            elif callable(fill):
                f = fill(lane)
            else:
                f = fill[:, blk * LANES:(blk + 1) * LANES]
            blocks.append(jnp.where((lane >= HALF) if odd else (lane < HALF), xp, f))
    return jnp.concatenate(blocks, axis=1)


def _even_proj_kernel(x_ref, wsb_ref, wc_ref, wkr_ref, wg_ref, gq_ref, gkv_ref,
                      wa_ref, wb_ref, wkv_ref, cq_ref, sq_ref, ck_ref, sk_ref,
                      qs_ref, kst_ref, vs_ref, qm_ref, kmt_ref, vm_ref, gate_ref):
    xb = x_ref[0].astype(BF16)
    w = N_HEADS * HEAD_DIM
    sb = _dot(xb, wsb_ref[...])
    qs_ref[0] = (sb[:, :w] * LOG2E).astype(BF16)
    kst_ref[0] = sb[:, w:2 * w].T.astype(BF16)
    vs_ref[0] = _spread_heads(sb[:, 2 * w:], 0.0).astype(BF16)
    gate_ref[0] = _dot(xb, wg_ref[...])

    c = _dot(xb, wc_ref[...])

    def rms(t, g):
        return (t * lax.rsqrt(jnp.mean(t * t, axis=-1, keepdims=True) + RMS_EPS) * g).astype(BF16)

    xq = rms(c[:, :MLA_RANK], gq_ref[...])
    xkv = rms(c[:, MLA_RANK:], gkv_ref[...])
    pairs = N_HEADS // 2
    cq = jnp.concatenate([cq_ref[...]] * pairs, axis=1)
    sq = jnp.concatenate([sq_ref[...]] * pairs, axis=1)
    qm_ref[0] = (_dot(xq, wa_ref[...]) * cq + _dot(xq, wb_ref[...]) * sq).astype(BF16)
    kr = _dot(xb, wkr_ref[...])
    krr = kr[:, :LANES] * ck_ref[...] + kr[:, LANES:] * sk_ref[...]
    krr = (krr, pltpu.roll(krr, HALF, 1))
    kv = _dot(xkv, wkv_ref[...])
    lane = lax.broadcasted_iota(jnp.int32, (kv.shape[0], LANES), 1)
    kblocks, vblocks = [], []
    for hd in range(N_HEADS):
        blk = slice(hd * LANES, (hd + 1) * LANES)
        v_half = (lane >= HALF) if hd % 2 else (lane < HALF)
        kblocks.append(jnp.where(v_half, krr[hd % 2], kv[:, blk]))
        vblocks.append(jnp.where(v_half, kv[:, blk], 1.0))
    kmt_ref[0] = jnp.concatenate(kblocks, axis=1).T.astype(BF16)
    vm_ref[0] = jnp.concatenate(vblocks, axis=1).astype(BF16)


def _even_proj(h, wts, tabs, tm):
    b, n, d = h.shape
    grid = (n // tm, b)
    row = lambda i, j: (j, i, 0)
    col = lambda i, j: (j, 0, i)
    const = lambda i, j: (0, 0)
    tab = lambda i, j: (i, 0)
    full = lambda a: pl.BlockSpec(a.shape, const)
    wide = N_HEADS * LANES
    narrow = N_HEADS * HEAD_DIM
    outs = [(narrow, BF16, False), (narrow, BF16, True), (wide, BF16, False),
            (wide, BF16, False), (wide, BF16, True), (wide, BF16, False), (d, F32, False)]
    return pl.pallas_call(
        _even_proj_kernel,
        grid=grid,
        in_specs=[pl.BlockSpec((1, tm, d), row)] + [full(a) for a in wts] + [
            pl.BlockSpec((tm, 2 * LANES), tab), pl.BlockSpec((tm, 2 * LANES), tab),
            pl.BlockSpec((tm, LANES), tab), pl.BlockSpec((tm, LANES), tab)],
        out_specs=[pl.BlockSpec((1, c, tm), col) if tr else pl.BlockSpec((1, tm, c), row)
                   for c, _, tr in outs],
        out_shape=[jax.ShapeDtypeStruct((b, c, n) if tr else (b, n, c), t) for c, t, tr in outs],
        compiler_params=pltpu.CompilerParams(
            dimension_semantics=("parallel", "parallel"), vmem_limit_bytes=VMEM_LIMIT),
        name="even_proj",
    )(h, *wts, *tabs)


def _odd_proj_kernel(x_ref, wqc_ref, wkc_ref, wvc_ref, wqf_ref, wkf_ref, wvf_ref, wfz_ref, wg_ref,
                     bf_ref, pf_ref,
                     qc_ref, kct_ref, vc_ref, qf_ref, kft_ref, vf_ref, gate_ref, carry_ref, *, tm):
    i = pl.program_id(1)
    xb = x_ref[0].astype(BF16)
    qc_ref[0] = _dot(xb, wqc_ref[...]).astype(BF16)
    kct_ref[0] = _dot(xb, wkc_ref[...]).T.astype(BF16)
    vc_ref[0] = _dot(xb, wvc_ref[...]).astype(BF16)
    gate_ref[0] = _dot(xb, wg_ref[...])

    vf_ref[0] = _spread_heads(_dot(xb, wvf_ref[...]), 1.0).astype(BF16)
    qf_ref[0] = _spread_heads(_dot(xb, wqf_ref[...]),
                              lambda lane: jnp.where((lane & (HALF - 1)) < 3, 1.0, 0.0)).astype(BF16)

    fz = _dot(xb, wfz_ref[...]) + bf_ref[...]
    r = lax.broadcasted_iota(jnp.int32, (tm, LANES), 0) + i * tm
    c = lax.broadcasted_iota(jnp.int32, (tm, LANES), 1)
    lf = jnp.where((c < N_HEADS) & (r >= LEAD), _log_sigmoid(fz), 0.0)

    def split3(t):
        a = t.astype(BF16)
        rem = t - a.astype(F32)
        bb = rem.astype(BF16)
        cc = (rem - bb.astype(F32)).astype(BF16)
        return jnp.concatenate([a, bb, cc], axis=1)

    rr = lax.broadcasted_iota(jnp.int32, (tm, tm), 0)
    cc = lax.broadcasted_iota(jnp.int32, (tm, tm), 1)
    tri = jnp.where(cc <= rr, 1.0, 0.0).astype(BF16)

    @pl.when(i == 0)
    def _():
        carry_ref[...] = jnp.zeros_like(carry_ref)

    c3 = _dot(tri, split3(lf))
    cum = c3[:, :LANES] + c3[:, LANES:2 * LANES] + c3[:, 2 * LANES:] + carry_ref[0:1, :]
    carry_ref[...] = jnp.broadcast_to(cum[tm - 1:tm, :], carry_ref.shape)
    kft_ref[0] = _spread_heads(_dot(xb, wkf_ref[...]), _dot(split3(-cum), pf_ref[...])).T.astype(BF16)


def _odd_proj(h, wts, tm):
    b, n, d = h.shape
    row = lambda j, i: (j, i, 0)
    col = lambda j, i: (j, 0, i)
    const = lambda j, i: (0, 0)
    full = lambda a: pl.BlockSpec(a.shape, const)
    wide = N_HEADS * LANES
    narrow = N_HEADS * HEAD_DIM
    outs = [(narrow, BF16, False), (SWA_KV_HEADS * HEAD_DIM, BF16, True), (narrow, BF16, False),
            (wide, BF16, False), (wide, BF16, True), (wide, BF16, False), (d, F32, False)]
    return pl.pallas_call(
        functools.partial(_odd_proj_kernel, tm=tm),
        grid=(b, n // tm),
        in_specs=[pl.BlockSpec((1, tm, d), row)] + [full(a) for a in wts],
        out_specs=[pl.BlockSpec((1, c, tm), col) if tr else pl.BlockSpec((1, tm, c), row)
                   for c, _, tr in outs],
        out_shape=[jax.ShapeDtypeStruct((b, c, n) if tr else (b, n, c), t) for c, t, tr in outs],
        scratch_shapes=[pltpu.VMEM((8, LANES), F32)],
        compiler_params=pltpu.CompilerParams(
            dimension_semantics=("parallel", "arbitrary"), vmem_limit_bytes=VMEM_LIMIT),
        name="odd_proj",
    )(h, *wts)


def _out_math(ma, mb, h, w_ref, g, b):
    half = ma.shape[-1]
    y = _dot(ma, w_ref[:half, :]) + _dot(mb, w_ref[half:, :])
    r = DN_ALPHA * h + y
    mu = jnp.mean(r, axis=-1, keepdims=True)
    cen = r - mu
    var = jnp.mean(cen * cen, axis=-1, keepdims=True)
    return cen * lax.rsqrt(var + LN_EPS) * g + b


def _out_rows(ma_ref, mb_ref, h_ref, w_ref, g_ref, b_ref, store):
    rows = ma_ref.shape[0]
    part = OUT_PART if rows % OUT_PART == 0 else rows
    for r in range(rows // part):
        sl = slice(r * part, (r + 1) * part)
        store(sl, _out_math(ma_ref[sl, :], mb_ref[sl, :], h_ref[sl, :], w_ref, g_ref[...], b_ref[...]))


def _out_kernel(ma_ref, mb_ref, h_ref, w_ref, g_ref, b_ref, o_ref):
    def store(sl, val):
        o_ref[sl, :] = val

    _out_rows(ma_ref, mb_ref, h_ref, w_ref, g_ref, b_ref, store)


def _out_frames_kernel(ma_ref, mb_ref, h_ref, w_ref, g_ref, b_ref, o_ref):
    def store(sl, val):
        o_ref[0, sl, :] = val

    _out_rows(ma_ref, mb_ref, h_ref, w_ref, g_ref, b_ref, store)


def _out_proj_frames(ma, mb, h, w, g, bias):
    b, n, d = h.shape
    half = ma.shape[-1]
    flat = lambda a: a.reshape(b * n, a.shape[-1])
    src = lambda i, j: (pl.multiple_of(i * n + BLK + j * OUT_TM, BLK), 0)
    window = lambda width: pl.BlockSpec((pl.Element(OUT_TM), pl.Element(width)), src)
    const = lambda i, j: (0, 0)
    return pl.pallas_call(
        _out_frames_kernel,
        grid=(b, (n - BLK) // OUT_TM),
        in_specs=[window(half), window(half), window(d),
                  pl.BlockSpec(w.shape, const), pl.BlockSpec((1, d), const), pl.BlockSpec((1, d), const)],
        out_specs=pl.BlockSpec((1, OUT_TM, d), lambda i, j: (i, j, 0)),
        out_shape=jax.ShapeDtypeStruct((b, n - BLK, d), F32),
        compiler_params=pltpu.CompilerParams(
            dimension_semantics=("parallel", "parallel"), vmem_limit_bytes=VMEM_LIMIT),
        name="out_proj_ln_frames",
    )(flat(ma), flat(mb), flat(h), w, g, bias)


def _out_proj(ma, mb, h, w, g, bias, tm):
    rows, d = h.shape
    half = ma.shape[-1]
    row = lambda i: (i, 0)
    const = lambda i: (0, 0)
    return pl.pallas_call(
        _out_kernel,
        grid=(rows // tm,),
        in_specs=[pl.BlockSpec((tm, half), row), pl.BlockSpec((tm, half), row),
                  pl.BlockSpec((tm, d), row),
                  pl.BlockSpec(w.shape, const), pl.BlockSpec((1, d), const), pl.BlockSpec((1, d), const)],
        out_specs=pl.BlockSpec((tm, d), row),
        out_shape=jax.ShapeDtypeStruct((rows, d), F32),
        compiler_params=pltpu.CompilerParams(
            dimension_semantics=("parallel",), vmem_limit_bytes=VMEM_LIMIT),
        name="out_proj_ln",
    )(ma, mb, h, w, g, bias)


def _by_parity(even, odd):
    is_odd = (jnp.arange(N_HEADS) % 2 == 1)[:, None]
    return jnp.where(is_odd, odd, even).reshape(even.shape[:-2] + (N_HEADS * LANES,))


def _mla_head_blocks(nope, rot):
    spare = jnp.zeros(rot.shape[:-1] + (HALF - MLA_ROPE,), rot.dtype)
    return _by_parity(jnp.concatenate([rot, spare, nope], axis=-1),
                      jnp.concatenate([nope, rot, spare], axis=-1))


def _even_weights(w_in, g_cq, g_ckv, w_uq, w_ukv):
    a = N_HEADS * HEAD_DIM
    wq, wk, wv = w_in[:, :a], w_in[:, a:2 * a], w_in[:, 2 * a:3 * a]
    o = 3 * a
    w_c = w_in[:, o:o + 2 * MLA_RANK]
    o += 2 * MLA_RANK
    w_kr = w_in[:, o:o + MLA_ROPE]
    o += MLA_ROPE
    w_g = w_in[:, o:]
    w_sb = jnp.concatenate([wq * HEAD_DIM ** -0.5, wk, wv], axis=1)
    hr = MLA_ROPE // 2
    zpad = jnp.zeros((w_in.shape[0], LANES - MLA_ROPE), w_in.dtype)
    w_kr2 = jnp.concatenate([w_kr, zpad, w_kr[:, hr:], w_kr[:, :hr], zpad], axis=1)

    uq = w_uq.reshape(MLA_RANK, N_HEADS, MLA_NOPE + MLA_ROPE)
    ukv = w_ukv.reshape(MLA_RANK, N_HEADS, MLA_NOPE + HEAD_DIM)
    nope, r1, r2 = uq[..., :MLA_NOPE], uq[..., MLA_NOPE:MLA_NOPE + hr], uq[..., MLA_NOPE + hr:]
    w_a = _mla_head_blocks(nope, jnp.concatenate([r1, r2], axis=-1))
    w_b = _mla_head_blocks(jnp.zeros_like(nope), jnp.concatenate([r2, r1], axis=-1))
    k_nope, v = ukv[..., :MLA_NOPE], ukv[..., MLA_NOPE:]
    w_kv = _by_parity(jnp.concatenate([v, k_nope], axis=-1), ukv)
    bf = lambda t: t.astype(BF16)
    return (bf(w_sb), bf(w_c), bf(w_kr2), bf(w_g), g_cq[None, :], g_ckv[None, :],
            bf(w_a), bf(w_b), bf(w_kv))


def _rope_tables(n):
    hr = MLA_ROPE // 2
    pos = jnp.arange(n, dtype=F32) - LEAD
    inv = ROPE_BASE ** (-jnp.arange(hr, dtype=F32) / hr)
    ang = pos[:, None] * inv[None, :]
    cos, sin = jnp.cos(ang), jnp.sin(ang)
    ones = jnp.ones((n, MLA_NOPE), F32)
    scale = (MLA_NOPE + MLA_ROPE) ** -0.5
    c_rot, s_rot = jnp.concatenate([cos, cos], axis=1), jnp.concatenate([-sin, sin], axis=1)
    spare = jnp.zeros((n, HALF - MLA_ROPE), F32)
    c_q = jnp.concatenate([c_rot, spare, ones, ones, c_rot, spare], axis=1) * scale
    s_q = jnp.concatenate([s_rot, spare, 0.0 * ones, 0.0 * ones, s_rot, spare], axis=1) * scale
    z96 = jnp.zeros((n, LANES - MLA_ROPE), F32)
    c_k = jnp.concatenate([c_rot, z96], axis=1)
    s_k = jnp.concatenate([s_rot, z96], axis=1)
    return c_q, s_q, c_k, s_k


def _odd_weights(w_in, b_forget):
    a = N_HEADS * HEAD_DIM
    kvw = SWA_KV_HEADS * HEAD_DIM
    wqc = w_in[:, :a]
    o = a
    wkc = w_in[:, o:o + kvw]
    o += kvw
    wvc = w_in[:, o:o + kvw]
    o += kvw
    wqd, wkd, wvd = w_in[:, o:o + a], w_in[:, o + a:o + 2 * a], w_in[:, o + 2 * a:o + 3 * a]
    o += 3 * a
    wfz = w_in[:, o:o + N_HEADS]
    o += N_HEADS
    w_g = w_in[:, o:]
    scale = HEAD_DIM ** -0.5
    v0, v1 = wvc[:, :HEAD_DIM], wvc[:, HEAD_DIM:]
    z = jnp.zeros_like(v0)
    w_vc = jnp.concatenate([v0, z, z, v0, v1, z, z, v1], axis=1)
    w_fz = jnp.concatenate([wfz, jnp.zeros((w_in.shape[0], LANES - N_HEADS), w_in.dtype)], axis=1)
    b_f = jnp.concatenate([b_forget, jnp.zeros((LANES - N_HEADS,), b_forget.dtype)])[None, :]
    bf = lambda t: t.astype(BF16)
    return (bf(wqc * scale), bf(wkc), bf(w_vc), bf(wqd * scale), bf(wkd), bf(wvd), bf(w_fz), bf(w_g),
            b_f.astype(F32))


def _piece_placement():
    p_f = np.zeros((3 * LANES, N_HEADS * LANES), np.float32)
    for t in range(3):
        for h in range(N_HEADS):
            p_f[t * LANES + h, h * LANES + (0 if h % 2 else HALF) + t] = 1.0
    return jnp.asarray(p_f, BF16)


def _row_tile(nb, cap):
    g = max(t for t in range(1, cap + 1) if nb % t == 0)
    return g * BLK


def kernel(x, meta_tokens, w_in_even, g_cq, g_ckv, w_uq, w_ukv, w_out_even,
           w_in_odd, b_forget, sink_logits, w_out_odd, ln_gain, ln_bias):
    b, s, d = x.shape
    assert s % SB_TQ == 0 and s % TQ == 0 and d == D_MODEL
    n = BLK + s
    nb = n // BLK
    h = jnp.concatenate([jnp.zeros((b, LEAD, d), x.dtype),
                         jnp.broadcast_to(meta_tokens[None].astype(x.dtype), (b, N_META, d)), x], axis=1)
    tabs = _rope_tables(n)
    tm_proj = _row_tile(nb, 3)
    tm_out = _row_tile(b * nb, OUT_TM // BLK)
    even_wts = jax.vmap(_even_weights)(w_in_even, g_cq, g_ckv, w_uq, w_ukv)
    odd_wts = jax.vmap(_odd_weights)(w_in_odd, b_forget)
    p_f = _piece_placement()
    for layer in range(DEPTH):
        i = layer // 2
        if layer % 2 == 0:
            wts = tuple(t[i] for t in even_wts)
            qs, kst, vs, qm, kmt, vm, gate = _even_proj(h, wts, tabs, tm_proj)
            mix_a = _sb_attention(qs, kst, vs, gate)
            mix_b = _softmax_attention(qm, kmt, vm, gate, "chunk")
            w_out = w_out_even[i]
        else:
            wts = tuple(t[i] for t in odd_wts) + (p_f,)
            qc, kct, vc, qf, kft, vf, gate = _odd_proj(h, wts, tm_proj)
            mix_a = _swa_attention(sink_logits[i].astype(F32), qc, kct, vc, gate)
            mix_b = _softmax_attention(qf, kft, vf, gate, "causal")
            w_out = w_out_odd[i]
        ln = (w_out.astype(BF16), ln_gain[layer][None, :], ln_bias[layer][None, :])
        if layer == DEPTH - 1:
            return _out_proj_frames(mix_a, mix_b, h, *ln)
        rows = b * n
        h = _out_proj(mix_a.reshape(rows, -1), mix_b.reshape(rows, -1), h.reshape(rows, d),
                      *ln, tm_out).reshape(b, n, d)
```
